```python
import math
import jax, jax.numpy as jnp
from jax import lax
import numpy as np

D_MODEL = 1024
BATCH = 8
SEQ = 4096
DEPTH = 1

CHUNK = 64
Q_BLOCK = 128
A_HEADS = 4
A_DH = D_MODEL // 16
B_HEADS = 8
B_DH = D_MODEL // 16
IDX_HEADS = 16
IDX_DH = 64
TOPK_MAX = 256
D_FF = 4 * D_MODEL
EPS = 1e-6
D_MIX = A_HEADS * 2 * A_DH + B_HEADS * B_DH
IN_SIZES = (A_HEADS * 2 * A_DH, A_HEADS * 2 * A_DH, A_HEADS * 2 * A_DH,
            B_HEADS * B_DH, B_DH, B_DH,
            IDX_HEADS * IDX_DH, IDX_DH, IDX_HEADS)
N_IN = sum(IN_SIZES)

kernel_name = "hybrid_diffattn_dsa_adaln_block"


def rmsnorm(x, g):
    xf = x.astype(jnp.float32)
    y = xf * lax.rsqrt(jnp.mean(xf * xf, axis=-1, keepdims=True) + EPS)
    return y.astype(x.dtype) * g


def alibi_slopes(n):
    return jnp.asarray(2.0 ** (-8.0 * np.arange(1, n + 1, dtype=np.float32) / n), dtype=jnp.float32)


def chunk_allowed(q_pos, k_pos):
    return k_pos[None, :] < (q_pos[:, None] // CHUNK + 1) * CHUNK


def split_columns(proj):
    offsets = []
    acc = 0
    for size in IN_SIZES[:-1]:
        acc += size
        offsets.append(acc)
    return jnp.split(proj, offsets, axis=-1)


def diff_attention_block(q1, q2, k1, k2, v, lam, lam_init, subln_g, q_pos, slopes):
    L = k1.shape[1]
    k_pos = jnp.arange(L, dtype=jnp.int32)
    dist = jnp.abs(q_pos[:, None] - k_pos[None, :]).astype(jnp.float32)
    bias = jnp.where(chunk_allowed(q_pos, k_pos)[None],
                     -slopes[:, None, None] * dist[None], -jnp.inf)
    scale = A_DH ** -0.5
    s1 = jnp.einsum('bqhd,bkhd->bhqk', q1, k1).astype(jnp.float32) * scale + bias
    s2 = jnp.einsum('bqhd,bkhd->bhqk', q2, k2).astype(jnp.float32) * scale + bias
    attn = jax.nn.softmax(s1, axis=-1) - lam.astype(jnp.float32) * jax.nn.softmax(s2, axis=-1)
    o = jnp.einsum('bhqk,bkhv->bqhv', attn.astype(v.dtype), v)
    o = rmsnorm(o, subln_g) * (1.0 - lam_init)
    return o.reshape(o.shape[0], o.shape[1], -1)


def dsa_block(q, k, v, q_idx, k_idx, w_idx, q_pos, slopes, topk):
    L = k.shape[1]
    k_pos = jnp.arange(L, dtype=jnp.int32)
    rel = jax.nn.relu(jnp.einsum('bqhd,bkd->bqhk', q_idx, k_idx))
    score = jnp.einsum('bqh,bqhk->bqk', w_idx, rel).astype(jnp.float32)
    score = jnp.where(chunk_allowed(q_pos, k_pos)[None], score, -jnp.inf)
    kk = min(topk, L)
    top_val, top_idx = lax.top_k(score, kk)
    valid = top_val > -jnp.inf
    k_sel = jax.vmap(lambda kb, ib: kb[ib])(k, top_idx)
    v_sel = jax.vmap(lambda vb, ib: vb[ib])(v, top_idx)
    logits = jnp.einsum('bqhd,bqkd->bhqk', q, k_sel).astype(jnp.float32) * (B_DH ** -0.5)
    dist = jnp.abs(q_pos[None, :, None] - top_idx).astype(jnp.float32)
    logits = logits - slopes[None, :, None, None] * dist[:, None]
    logits = jnp.where(valid[:, None], logits, -jnp.inf)
    p = jax.nn.softmax(logits, axis=-1)
    o = jnp.einsum('bhqk,bqkd->bqhd', p.astype(v_sel.dtype), v_sel)
    return o.reshape(o.shape[0], o.shape[1], -1)


def setup_inputs(seed: int = 0) -> dict:
    key = jax.random.key(seed)
    ks = jax.random.split(key, 16)
    nrm = jax.random.normal
    f32 = jnp.float32
    return {
        "x": nrm(ks[0], (BATCH, SEQ, D_MODEL), f32),
        "c": nrm(ks[1], (BATCH, D_MODEL), f32),
        "norm1_g": 1.0 + 0.02 * nrm(ks[2], (DEPTH, D_MODEL), f32),
        "norm2_g": 1.0 + 0.02 * nrm(ks[3], (DEPTH, D_MODEL), f32),
        "w_ada": nrm(ks[4], (DEPTH, D_MODEL, 6 * D_MODEL), f32) * D_MODEL ** -0.5,
        "b_ada": 0.02 * nrm(ks[5], (DEPTH, 6 * D_MODEL), f32),
        "w_in": nrm(ks[6], (DEPTH, D_MODEL, N_IN), f32) * D_MODEL ** -0.5,
        "lam_q1": 0.1 * nrm(ks[7], (DEPTH, A_DH), f32),
        "lam_k1": 0.1 * nrm(ks[8], (DEPTH, A_DH), f32),
        "lam_q2": 0.1 * nrm(ks[9], (DEPTH, A_DH), f32),
        "lam_k2": 0.1 * nrm(ks[10], (DEPTH, A_DH), f32),
        "subln_g": 1.0 + 0.02 * nrm(ks[11], (DEPTH, 2 * A_DH), f32),
        "w_out": nrm(ks[12], (DEPTH, D_MIX, D_MODEL), f32) * D_MIX ** -0.5,
        "w_ff1": nrm(ks[13], (DEPTH, D_MODEL, D_FF), f32) * D_MODEL ** -0.5,
        "w_ff2": nrm(ks[14], (DEPTH, D_FF, D_MODEL), f32) * D_FF ** -0.5,
        "final_g": 1.0 + 0.02 * nrm(ks[15], (D_MODEL,), f32),
    }


def reference(x, c, norm1_g, norm2_g, w_ada, b_ada, w_in, lam_q1, lam_k1, lam_q2, lam_k2,
              subln_g, w_out, w_ff1, w_ff2, final_g):
    B, S, _ = x.shape
    topk = min(TOPK_MAX, S // 4)
    slopes_a = alibi_slopes(A_HEADS)
    slopes_b = alibi_slopes(B_HEADS)
    pos = jnp.arange(S, dtype=jnp.int32)
    cond = jax.nn.silu(c)
    for l in range(DEPTH):
        lam_init = 0.8 - 0.6 * math.exp(-0.3 * l)
        mod = jnp.einsum('bd,de->be', cond, w_ada[l]) + b_ada[l]
        sh1, sc1, g1, sh2, sc2, g2 = [m[:, None, :] for m in jnp.split(mod, 6, axis=-1)]

        h = rmsnorm(x, norm1_g[l]) * (1.0 + sc1) + sh1
        proj = jnp.einsum('bsd,de->bse', h, w_in[l])
        aq, ak, av, bq, bk, bv, iq, ik, iw = split_columns(proj)
        aq = aq.reshape(B, S, A_HEADS, 2, A_DH)
        ak = ak.reshape(B, S, A_HEADS, 2, A_DH)
        av = av.reshape(B, S, A_HEADS, 2 * A_DH)
        bq = bq.reshape(B, S, B_HEADS, B_DH)
        iq = iq.reshape(B, S, IDX_HEADS, IDX_DH)
        iw = iw * (IDX_HEADS * IDX_DH) ** -0.5
        lam = (jnp.exp(jnp.sum(lam_q1[l] * lam_k1[l])) - jnp.exp(jnp.sum(lam_q2[l] * lam_k2[l]))
               + lam_init)
        blocks = []
        for i in range(S // Q_BLOCK):
            q0, q_end = i * Q_BLOCK, (i + 1) * Q_BLOCK
            q_pos = pos[q0:q_end]
            oa = diff_attention_block(aq[:, q0:q_end, :, 0], aq[:, q0:q_end, :, 1],
                                      ak[:, :q_end, :, 0], ak[:, :q_end, :, 1], av[:, :q_end],
                                      lam, lam_init, subln_g[l], q_pos, slopes_a)
            ob = dsa_block(bq[:, q0:q_end], bk[:, :q_end], bv[:, :q_end],
                           iq[:, q0:q_end], ik[:, :q_end], iw[:, q0:q_end],
                           q_pos, slopes_b, topk)
            blocks.append(jnp.concatenate([oa, ob], axis=-1))
        mix = jnp.concatenate(blocks, axis=1)
        x = x + g1 * jnp.einsum('bsm,md->bsd', mix, w_out[l])

        h2 = rmsnorm(x, norm2_g[l]) * (1.0 + sc2) + sh2
        ff = jnp.einsum('bsf,fd->bsd',
                        jnp.square(jax.nn.relu(jnp.einsum('bsd,df->bsf', h2, w_ff1[l]))), w_ff2[l])
        x = x + g2 * ff
    return rmsnorm(x, final_g)
```

```python
import functools
import math

import jax
import jax.numpy as jnp
import numpy as np
from jax import lax
from jax.experimental import pallas as pl
from jax.experimental.pallas import tpu as pltpu

CHUNK = 64
A_HEADS = 4
A_DH = 64
B_HEADS = 8
B_DH = 64
IDX_HEADS = 16
IDX_DH = 64
TOPK_MAX = 256
EPS = 1e-6

TQ = 256
TK = 256
TS_PROJ = 512
TM_FFN = 512
FF_CHUNK = 1024
VMEM_LIMIT_BYTES = 56 * 1024 * 1024

NEG = -1e30
INT_MIN = -(2 ** 31)
BF16 = jnp.bfloat16
F32 = jnp.float32
I32 = jnp.int32

_NT = (((1,), (1,)), ((), ()))
_TN = (((0,), (0,)), ((), ()))


def _alibi_slopes(n):
    return [2.0 ** (-8.0 * (i + 1) / n) for i in range(n)]


def _const_spec(shape):
    nd = len(shape)
    return pl.BlockSpec(shape, lambda *_: (0,) * nd, pipeline_mode=pl.Buffered(1))


def _rms(x):
    return x * lax.rsqrt(jnp.mean(x * x, axis=-1, keepdims=True) + EPS)


def _mod_kernel(c_ref, w_ref, b_ref, o_ref):
    c = c_ref[...]
    cond = c * (1.0 / (1.0 + jnp.exp(-c)))
    o_ref[...] = jnp.dot(cond.astype(BF16), w_ref[...], preferred_element_type=F32) + b_ref[...]


def _adaln_mod(c, w_ada, b_ada):
    bsz, d = c.shape
    n = w_ada.shape[1]
    return pl.pallas_call(
        _mod_kernel,
        grid=(n // d,),
        in_specs=[
            pl.BlockSpec((bsz, d), lambda j: (0, 0)),
            pl.BlockSpec((d, d), lambda j: (0, j)),
            pl.BlockSpec((1, d), lambda j: (0, j)),
        ],
        out_specs=pl.BlockSpec((bsz, d), lambda j: (0, j)),
        out_shape=jax.ShapeDtypeStruct((bsz, n), F32),
        compiler_params=pltpu.CompilerParams(dimension_semantics=("arbitrary",)),
        name="adaln_mod",
    )(c, w_ada, b_ada)


def _inproj_kernel(x_ref, mod_ref, g_ref, waq_ref, wak_ref, wbq_ref, wiq_ref, wkk_ref,
                   wav_ref, wbv_ref, wiw_ref,
                   aq_ref, ak_ref, bq_ref, iq_ref, bkk_ref, ikk_ref, av_ref, bv_ref, iw_ref):
    x = x_ref[0]
    sh1 = mod_ref[0, 0:1, :]
    sc1 = mod_ref[0, 1:2, :]
    h = (_rms(x) * g_ref[...]) * (1.0 + sc1) + sh1
    hb = h.astype(BF16)

    def nat(w_ref):
        return jnp.dot(hb, w_ref[...], preferred_element_type=F32)

    aq_ref[0] = nat(waq_ref).astype(BF16)
    ak_ref[0] = nat(wak_ref).astype(BF16)
    bq_ref[0] = nat(wbq_ref).astype(BF16)
    iq_ref[0] = nat(wiq_ref).astype(BF16)
    kk = nat(wkk_ref).astype(BF16)
    bkk_ref[0] = kk[:, 0:128]
    ikk_ref[0] = kk[:, 128:256]

    def tr(w_ref):
        return lax.dot_general(w_ref[...], hb, _NT, preferred_element_type=F32)

    avt = tr(wav_ref).astype(BF16)
    bvt = tr(wbv_ref).astype(BF16)
    iwt = tr(wiw_ref)
    for c in range(TS_PROJ // TK):
        av_ref[0, c] = avt[:, c * TK:(c + 1) * TK]
        bv_ref[0, c] = bvt[:, c * TK:(c + 1) * TK]
        iw_ref[0, c] = iwt[:, c * TK:(c + 1) * TK]


def _in_proj(x, mod, g, ws):
    bsz, s, d = x.shape
    ts = TS_PROJ
    nk = s // TK
    cpt = ts // TK

    def tok(cols):
        return pl.BlockSpec((1, ts, cols), lambda b, t: (b, t, 0))

    def trs(rows):
        return pl.BlockSpec((1, cpt, rows, TK), lambda b, t: (b, t, 0, 0))

    a_cols = A_HEADS * 2 * A_DH
    b_cols = B_HEADS * B_DH
    i_cols = IDX_HEADS * IDX_DH
    out_shape = [
        jax.ShapeDtypeStruct((bsz, s, a_cols), BF16),
        jax.ShapeDtypeStruct((bsz, s, a_cols), BF16),
        jax.ShapeDtypeStruct((bsz, s, b_cols), BF16),
        jax.ShapeDtypeStruct((bsz, s, i_cols), BF16),
        jax.ShapeDtypeStruct((bsz, s, 128), BF16),
        jax.ShapeDtypeStruct((bsz, s, 128), BF16),
        jax.ShapeDtypeStruct((bsz, nk, a_cols, TK), BF16),
        jax.ShapeDtypeStruct((bsz, nk, B_DH, TK), BF16),
        jax.ShapeDtypeStruct((bsz, nk, IDX_HEADS, TK), F32),
    ]
    out_specs = [tok(a_cols), tok(a_cols), tok(b_cols), tok(i_cols), tok(128), tok(128),
                 trs(a_cols), trs(B_DH), trs(IDX_HEADS)]
    in_specs = [
        pl.BlockSpec((1, ts, d), lambda b, t: (b, t, 0)),
        pl.BlockSpec((1, 6, d), lambda b, t: (b, 0, 0)),
        _const_spec(g.shape),
    ] + [_const_spec(w.shape) for w in ws]
    return pl.pallas_call(
        _inproj_kernel,
        grid=(bsz, s // ts),
        in_specs=in_specs,
        out_specs=out_specs,
        out_shape=out_shape,
        compiler_params=pltpu.CompilerParams(
            dimension_semantics=("arbitrary", "arbitrary"), vmem_limit_bytes=VMEM_LIMIT_BYTES),
        name="in_proj",
    )(x, mod, g, *ws)


def _online_update(s, v, acc_ref, m_ref, l_ref, idx):
    m_prev = m_ref[idx:idx + 1, :]
    m_new = jnp.maximum(m_prev, jnp.max(s, axis=0, keepdims=True))
    alpha = jnp.exp(m_prev - m_new)
    p = jnp.exp(s - m_new)
    l_ref[idx:idx + 1, :] = alpha * l_ref[idx:idx + 1, :] + jnp.sum(p, axis=0, keepdims=True)
    m_ref[idx:idx + 1, :] = m_new
    pv = jnp.dot(v, p.astype(BF16), preferred_element_type=F32)
    acc_ref[idx] = alpha * acc_ref[idx] + pv


def _attn_kernel(lam_init, aq_ref, bq_ref, iq_ref, iw_ref, ak_ref, av_ref, bkk_ref, ikk_ref,
                 bv_ref, lamv_ref, g_ref, out_ref,
                 key_ref, dl_ref, dlim_ref, qa_ref, qb_ref, qi_ref, w_ref,
                 acc_a_ref, m_a_ref, l_a_ref, acc_b_ref, m_b_ref, l_b_ref):
    i = pl.program_id(1)
    nkv = i + 1
    slopes_a = _alibi_slopes(A_HEADS)
    slopes_b = _alibi_slopes(B_HEADS)

    row = lax.broadcasted_iota(I32, (TK, TQ), 0)
    col = lax.broadcasted_iota(I32, (TK, TQ), 1)
    dl_ref[...] = (col - row).astype(F32)
    dlim_ref[...] = (col // CHUNK + 1) * CHUNK - row

    lane = lax.broadcasted_iota(I32, (TQ, 128), 1)
    lo = lane < 64

    def halves(tile_bf16, scale):
        t = tile_bf16.astype(F32) * scale
        z = jnp.zeros_like(t)
        return jnp.where(lo, t, z).astype(BF16), jnp.where(lo, z, t).astype(BF16)

    qscale = A_DH ** -0.5
    for h in range(A_HEADS):
        q1, q2 = halves(aq_ref[0, :, h * 128:(h + 1) * 128], qscale)
        qa_ref[2 * h] = q1
        qa_ref[2 * h + 1] = q2
    for t in range(B_HEADS // 2):
        q1, q2 = halves(bq_ref[0, :, t * 128:(t + 1) * 128], B_DH ** -0.5)
        qb_ref[2 * t] = q1
        qb_ref[2 * t + 1] = q2
    for t in range(IDX_HEADS // 2):
        q1, q2 = halves(iq_ref[0, :, t * 128:(t + 1) * 128], 1.0)
        qi_ref[2 * t] = q1
        qi_ref[2 * t + 1] = q2
    w_ref[...] = iw_ref[0, 0] * ((IDX_HEADS * IDX_DH) ** -0.5)

    acc_a_ref[...] = jnp.zeros_like(acc_a_ref)
    acc_b_ref[...] = jnp.zeros_like(acc_b_ref)
    m_a_ref[...] = jnp.full_like(m_a_ref, NEG)
    m_b_ref[...] = jnp.full_like(m_b_ref, NEG)
    l_a_ref[...] = jnp.zeros_like(l_a_ref)
    l_b_ref[...] = jnp.zeros_like(l_b_ref)

    def block_geometry(j):
        base = ((i - j) * TK).astype(F32)
        dist = jnp.abs(dl_ref[...] + base)
        off = jnp.where(j < i, jnp.int32(1 << 20), jnp.int32(0))
        allowed = (dlim_ref[...] + off) > 0
        return dist, allowed

    def phase1(j, carry):
        k0 = pl.multiple_of(j * TK, TK)
        dist, allowed = block_geometry(j)
        for h in range(A_HEADS):
            kh = ak_ref[0, pl.ds(k0, TK), h * 128:(h + 1) * 128]
            vh = av_ref[0, j, h * 128:(h + 1) * 128, :]
            sd = slopes_a[h] * dist
            for mp in range(2):
                s = lax.dot_general(kh, qa_ref[2 * h + mp], _NT, preferred_element_type=F32) - sd
                s = jnp.where(allowed, s, NEG)
                _online_update(s, vh, acc_a_ref, m_a_ref, l_a_ref, 2 * h + mp)
        kk = ikk_ref[0, pl.ds(k0, TK), :]
        score = jnp.zeros((TK, TQ), F32)
        for hh in range(IDX_HEADS):
            xh = lax.dot_general(kk, qi_ref[hh], _NT, preferred_element_type=F32)
            score = score + w_ref[hh:hh + 1, :] * jnp.maximum(xh, 0.0)
        bits = pltpu.bitcast(score, I32)
        key = bits ^ ((bits >> 31) & jnp.int32(0x7FFFFFFF))
        key_ref[pl.ds(k0, TK), :] = jnp.where(allowed, key, jnp.int32(INT_MIN))
        return carry

    lax.fori_loop(0, nkv, phase1, 0)

    def count_ge(cand):
        def body(c, cnt):
            kt = key_ref[pl.ds(pl.multiple_of(c * TK, TK), TK), :]
            ind = jnp.where(kt >= cand, jnp.int32(1), jnp.int32(0))
            return cnt + jnp.sum(ind.reshape(TK // 8, 8, TQ), axis=0)
        cnt8 = lax.fori_loop(0, nkv, body, jnp.zeros((8, TQ), I32))
        return jnp.sum(cnt8, axis=0, keepdims=True)

    def bit_body(t, carry):
        pref, cnt_at = carry
        bit = jnp.left_shift(jnp.int32(1), 31 - t)
        trial = pref | bit
        cnt = count_ge(trial ^ jnp.int32(INT_MIN))
        take = cnt >= TOPK_MAX
        return jnp.where(take, trial, pref), jnp.where(take, cnt, cnt_at)

    pref, cnt_at = lax.fori_loop(
        0, 32, bit_body, (jnp.zeros((1, TQ), I32), jnp.full((1, TQ), TOPK_MAX, I32)))
    tau = jnp.maximum(pref ^ jnp.int32(INT_MIN), jnp.int32(INT_MIN + 1))

    @pl.when(jnp.max(cnt_at) > TOPK_MAX)
    def _():
        def count_where(pred_fn):
            def body(c, cnt):
                k0 = pl.multiple_of(c * TK, TK)
                kt = key_ref[pl.ds(k0, TK), :]
                pos = lax.broadcasted_iota(I32, (TK, TQ), 0) + c * TK
                ind = jnp.where(pred_fn(kt, pos), jnp.int32(1), jnp.int32(0))
                return cnt + jnp.sum(ind.reshape(TK // 8, 8, TQ), axis=0)
            cnt8 = lax.fori_loop(0, nkv, body, jnp.zeros((8, TQ), I32))
            return jnp.sum(cnt8, axis=0, keepdims=True)

        need = TOPK_MAX - count_where(lambda kt, pos: kt > tau)

        def idx_body(t, bound):
            trial = bound | jnp.left_shift(jnp.int32(1), 13 - t)
            cnt = count_where(lambda kt, pos: jnp.where(kt == tau, pos, jnp.int32(1 << 30)) < trial)
            return jnp.where(cnt <= need, trial, bound)

        bound = lax.fori_loop(0, 14, idx_body, jnp.zeros((1, TQ), I32))

        def demote(c, carry):
            k0 = pl.multiple_of(c * TK, TK)
            kt = key_ref[pl.ds(k0, TK), :]
            pos = lax.broadcasted_iota(I32, (TK, TQ), 0) + c * TK
            drop = jnp.where(kt == tau, pos, jnp.int32(-1)) >= bound
            key_ref[pl.ds(k0, TK), :] = jnp.where(drop, kt - 1, kt)
            return carry

        lax.fori_loop(0, nkv, demote, 0)

    def phase3(j, carry):
        k0 = pl.multiple_of(j * TK, TK)
        dist, _ = block_geometry(j)
        sel = key_ref[pl.ds(k0, TK), :] >= tau
        kb = bkk_ref[0, pl.ds(k0, TK), :]
        vb = bv_ref[0, j]
        for h in range(B_HEADS):
            s = lax.dot_general(kb, qb_ref[h], _NT, preferred_element_type=F32) - slopes_b[h] * dist
            s = jnp.where(sel, s, NEG)
            _online_update(s, vb, acc_b_ref, m_b_ref, l_b_ref, h)
        return carry

    lax.fori_loop(0, nkv, phase3, 0)

    lv = lamv_ref[...]
    lam = (jnp.exp(jnp.sum(lv[0:1] * lv[1:2], axis=1, keepdims=True))
           - jnp.exp(jnp.sum(lv[2:3] * lv[3:4], axis=1, keepdims=True)) + lam_init)
    g = g_ref[...]
    for h in range(A_HEADS):
        o1 = acc_a_ref[2 * h] / l_a_ref[2 * h:2 * h + 1, :]
        o2 = acc_a_ref[2 * h + 1] / l_a_ref[2 * h + 1:2 * h + 2, :]
        o = o1 - lam * o2
        y = o * lax.rsqrt(jnp.mean(o * o, axis=0, keepdims=True) + EPS)
        out_ref[0, 0, h * 128:(h + 1) * 128, :] = ((y * g) * (1.0 - lam_init)).astype(BF16)
    a_rows = A_HEADS * 2 * A_DH
    for h in range(B_HEADS):
        ob = acc_b_ref[h] / l_b_ref[h:h + 1, :]
        out_ref[0, 0, a_rows + h * B_DH:a_rows + (h + 1) * B_DH, :] = ob.astype(BF16)


def _attention(lam_init, aq, bq, iq, iwt, ak, avt, bkk, ikk, bvt, lamv, gt):
    bsz, s, a_cols = aq.shape
    nq = s // TQ
    nk = s // TK
    b_cols = bq.shape[2]
    i_cols = iq.shape[2]
    d_mix = a_cols + b_cols
    in_specs = [
        pl.BlockSpec((1, TQ, a_cols), lambda b, i: (b, i, 0)),
        pl.BlockSpec((1, TQ, b_cols), lambda b, i: (b, i, 0)),
        pl.BlockSpec((1, TQ, i_cols), lambda b, i: (b, i, 0)),
        pl.BlockSpec((1, 1, IDX_HEADS, TQ), lambda b, i: (b, i, 0, 0)),
        pl.BlockSpec((1, s, a_cols), lambda b, i: (b, 0, 0)),
        pl.BlockSpec((1, nk, a_cols, TK), lambda b, i: (b, 0, 0, 0)),
        pl.BlockSpec((1, s, 128), lambda b, i: (b, 0, 0)),
        pl.BlockSpec((1, s, 128), lambda b, i: (b, 0, 0)),
        pl.BlockSpec((1, nk, B_DH, TK), lambda b, i: (b, 0, 0, 0)),
        _const_spec(lamv.shape),
        _const_spec(gt.shape),
    ]
    scratch = [
        pltpu.VMEM((s, TQ), I32),
        pltpu.VMEM((TK, TQ), F32),
        pltpu.VMEM((TK, TQ), I32),
        pltpu.VMEM((2 * A_HEADS, TQ, 128), BF16),
        pltpu.VMEM((B_HEADS, TQ, 128), BF16),
        pltpu.VMEM((IDX_HEADS, TQ, 128), BF16),
        pltpu.VMEM((IDX_HEADS, TQ), F32),
        pltpu.VMEM((2 * A_HEADS, 2 * A_DH, TQ), F32),
        pltpu.VMEM((2 * A_HEADS, TQ), F32),
        pltpu.VMEM((2 * A_HEADS, TQ), F32),
        pltpu.VMEM((B_HEADS, B_DH, TQ), F32),
        pltpu.VMEM((B_HEADS, TQ), F32),
        pltpu.VMEM((B_HEADS, TQ), F32),
    ]
    return pl.pallas_call(
        functools.partial(_attn_kernel, lam_init),
        grid=(bsz, nq),
        in_specs=in_specs,
        out_specs=pl.BlockSpec((1, 1, d_mix, TQ), lambda b, i: (b, i, 0, 0)),
        out_shape=jax.ShapeDtypeStruct((bsz, nq, d_mix, TQ), BF16),
        scratch_shapes=scratch,
        compiler_params=pltpu.CompilerParams(
            dimension_semantics=("arbitrary", "arbitrary"), vmem_limit_bytes=VMEM_LIMIT_BYTES),
        name="attention",
    )(aq, bq, iq, iwt, ak, avt, bkk, ikk, bvt, lamv, gt)


def _ffn_kernel(final, mix_ref, x_ref, mod_ref, n2_ref, fg_ref, wo_ref, w1_ref, w2_ref, o_ref):
    wo = wo_ref[...]
    y = jnp.concatenate(
        [lax.dot_general(mix_ref[0, c], wo, _TN, preferred_element_type=F32)
         for c in range(TM_FFN // TQ)], axis=0)
    g1 = mod_ref[0, 2:3, :]
    sh2 = mod_ref[0, 3:4, :]
    sc2 = mod_ref[0, 4:5, :]
    g2 = mod_ref[0, 5:6, :]
    x1 = x_ref[0] + g1 * y
    h2 = ((_rms(x1) * n2_ref[...]) * (1.0 + sc2) + sh2).astype(BF16)
    ff = jnp.zeros_like(x1)
    for c in range(w1_ref.shape[1] // FF_CHUNK):
        u = jnp.dot(h2, w1_ref[:, c * FF_CHUNK:(c + 1) * FF_CHUNK], preferred_element_type=F32)
        u = jnp.square(jnp.maximum(u, 0.0)).astype(BF16)
        ff = ff + jnp.dot(u, w2_ref[c * FF_CHUNK:(c + 1) * FF_CHUNK, :], preferred_element_type=F32)
    x2 = x1 + g2 * ff
    if final:
        x2 = _rms(x2) * fg_ref[...]
    o_ref[0] = x2


def _out_ffn(final, mixt, x, mod, n2g, fg, wo, w1, w2):
    bsz, s, d = x.shape
    d_mix = mixt.shape[2]
    tm = TM_FFN
    cpt = tm // TQ
    return pl.pallas_call(
        functools.partial(_ffn_kernel, final),
        grid=(bsz, s // tm),
        in_specs=[
            pl.BlockSpec((1, cpt, d_mix, TQ), lambda b, t: (b, t, 0, 0)),
            pl.BlockSpec((1, tm, d), lambda b, t: (b, t, 0)),
            pl.BlockSpec((1, 6, d), lambda b, t: (b, 0, 0)),
            _const_spec(n2g.shape),
            _const_spec(fg.shape),
            _const_spec(wo.shape),
            _const_spec(w1.shape),
            _const_spec(w2.shape),
        ],
        out_specs=pl.BlockSpec((1, tm, d), lambda b, t: (b, t, 0)),
        out_shape=jax.ShapeDtypeStruct((bsz, s, d), F32),
        compiler_params=pltpu.CompilerParams(
            dimension_semantics=("arbitrary", "arbitrary"), vmem_limit_bytes=VMEM_LIMIT_BYTES),
        name="out_ffn",
    )(mixt, x, mod, n2g, fg, wo, w1, w2)


def _split_w_in(w):
    a = A_HEADS * 2 * A_DH
    sizes = (a, a, a, B_HEADS * B_DH, B_DH, B_DH, IDX_HEADS * IDX_DH, IDX_DH, IDX_HEADS)
    offs = np.cumsum((0,) + sizes)
    aq, ak, av, bq, bk, bv, iq, ik, iw = [w[:, offs[n]:offs[n + 1]] for n in range(len(sizes))]
    kk = jnp.concatenate([bk, bk, ik, ik], axis=1)
    nat = [aq, ak, bq, iq, kk]
    trn = [av.T, bv.T, iw.T]
    return [m.astype(BF16) for m in nat + trn]


def kernel(x, c, norm1_g, norm2_g, w_ada, b_ada, w_in, lam_q1, lam_k1, lam_q2, lam_k2,
           subln_g, w_out, w_ff1, w_ff2, final_g):
    bsz, s, d = x.shape
    depth = w_in.shape[0]
    assert s % TS_PROJ == 0 and s % TM_FFN == 0 and s % TQ == 0 and TQ == TK
    assert min(TOPK_MAX, s // 4) == TOPK_MAX
    fg = final_g.reshape(1, d)
    for l in range(depth):
        lam_init = 0.8 - 0.6 * math.exp(-0.3 * l)
        mod = _adaln_mod(c, w_ada[l].astype(BF16), b_ada[l].reshape(1, -1)).reshape(bsz, 6, d)
        aq, ak, bq, iq, bkk, ikk, avt, bvt, iwt = _in_proj(
            x, mod, norm1_g[l].reshape(1, d), _split_w_in(w_in[l]))
        lamv = jnp.stack([lam_q1[l], lam_k1[l], lam_q2[l], lam_k2[l]])
        gt = jnp.broadcast_to(subln_g[l][:, None], (2 * A_DH, TQ))
        mixt = _attention(lam_init, aq, bq, iq, iwt, ak, avt, bkk, ikk, bvt, lamv, gt)
        x = _out_ffn(l == depth - 1, mixt, x, mod, norm2_g[l].reshape(1, d), fg,
                     w_out[l].astype(BF16), w_ff1[l].astype(BF16), w_ff2[l].astype(BF16))
    return x
```

```python
import functools
import math

import jax
import jax.numpy as jnp
import numpy as np
from jax import lax
from jax.experimental import pallas as pl
from jax.experimental.pallas import tpu as pltpu

CHUNK = 64
A_HEADS = 4
A_DH = 64
B_HEADS = 8
B_DH = 64
IDX_HEADS = 16
IDX_DH = 64
TOPK_MAX = 256
EPS = 1e-6

TQ = 256
TK = 256
TS_PROJ = 512
TM_FFN = 512
FF_CHUNK = 1024
VMEM_LIMIT_BYTES = 56 * 1024 * 1024

NEG = -1e30
INT_MIN = -(2 ** 31)
BF16 = jnp.bfloat16
F32 = jnp.float32
I32 = jnp.int32

_NT = (((1,), (1,)), ((), ()))
_TN = (((0,), (0,)), ((), ()))


def _alibi_slopes(n):
    return [2.0 ** (-8.0 * (i + 1) / n) for i in range(n)]


def _const_spec(shape):
    nd = len(shape)
    return pl.BlockSpec(shape, lambda *_: (0,) * nd, pipeline_mode=pl.Buffered(1))


def _rms(x):
    return x * lax.rsqrt(jnp.mean(x * x, axis=-1, keepdims=True) + EPS)


def _mod_kernel(c_ref, w_ref, b_ref, o_ref):
    c = c_ref[...]
    cond = c * (1.0 / (1.0 + jnp.exp(-c)))
    o_ref[...] = jnp.dot(cond.astype(BF16), w_ref[...], preferred_element_type=F32) + b_ref[...]


def _adaln_mod(c, w_ada, b_ada):
    bsz, d = c.shape
    n = w_ada.shape[1]
    return pl.pallas_call(
        _mod_kernel,
        grid=(n // d,),
        in_specs=[
            pl.BlockSpec((bsz, d), lambda j: (0, 0)),
            pl.BlockSpec((d, d), lambda j: (0, j)),
            pl.BlockSpec((1, d), lambda j: (0, j)),
        ],
        out_specs=pl.BlockSpec((bsz, d), lambda j: (0, j)),
        out_shape=jax.ShapeDtypeStruct((bsz, n), F32),
        compiler_params=pltpu.CompilerParams(dimension_semantics=("arbitrary",)),
        name="adaln_mod",
    )(c, w_ada, b_ada)


def _inproj_kernel(x_ref, mod_ref, g_ref, waq_ref, wak_ref, wbq_ref, wiq_ref, wkk_ref,
                   wav_ref, wbv_ref, wiw_ref,
                   aq_ref, ak_ref, bq_ref, iq_ref, bkk_ref, ikk_ref, av_ref, bv_ref, iw_ref):
    x = x_ref[0]
    sh1 = mod_ref[0, 0:1, :]
    sc1 = mod_ref[0, 1:2, :]
    h = (_rms(x) * g_ref[...]) * (1.0 + sc1) + sh1
    hb = h.astype(BF16)

    def nat(w_ref):
        return jnp.dot(hb, w_ref[...], preferred_element_type=F32)

    aq_ref[0] = nat(waq_ref).astype(BF16)
    ak_ref[0] = nat(wak_ref).astype(BF16)
    bq_ref[0] = nat(wbq_ref).astype(BF16)
    iq_ref[0] = nat(wiq_ref).astype(BF16)
    kk = nat(wkk_ref).astype(BF16)
    bkk_ref[0] = kk[:, 0:128]
    ikk_ref[0] = kk[:, 128:256]

    def tr(w_ref):
        return lax.dot_general(w_ref[...], hb, _NT, preferred_element_type=F32)

    avt = tr(wav_ref).astype(BF16)
    bvt = tr(wbv_ref).astype(BF16)
    iwt = tr(wiw_ref)
    for c in range(TS_PROJ // TK):
        av_ref[0, c] = avt[:, c * TK:(c + 1) * TK]
        bv_ref[0, c] = bvt[:, c * TK:(c + 1) * TK]
        iw_ref[0, c] = iwt[:, c * TK:(c + 1) * TK]


def _in_proj(x, mod, g, ws):
    bsz, s, d = x.shape
    ts = TS_PROJ
    nk = s // TK
    cpt = ts // TK

    def tok(cols):
        return pl.BlockSpec((1, ts, cols), lambda b, t: (b, t, 0))

    def trs(rows):
        return pl.BlockSpec((1, cpt, rows, TK), lambda b, t: (b, t, 0, 0))

    a_cols = A_HEADS * 2 * A_DH
    b_cols = B_HEADS * B_DH
    i_cols = IDX_HEADS * IDX_DH
    out_shape = [
        jax.ShapeDtypeStruct((bsz, s, a_cols), BF16),
        jax.ShapeDtypeStruct((bsz, s, a_cols), BF16),
        jax.ShapeDtypeStruct((bsz, s, b_cols), BF16),
        jax.ShapeDtypeStruct((bsz, s, i_cols), BF16),
        jax.ShapeDtypeStruct((bsz, s, 128), BF16),
        jax.ShapeDtypeStruct((bsz, s, 128), BF16),
        jax.ShapeDtypeStruct((bsz, nk, a_cols, TK), BF16),
        jax.ShapeDtypeStruct((bsz, nk, B_DH, TK), BF16),
        jax.ShapeDtypeStruct((bsz, nk, IDX_HEADS, TK), F32),
    ]
    out_specs = [tok(a_cols), tok(a_cols), tok(b_cols), tok(i_cols), tok(128), tok(128),
                 trs(a_cols), trs(B_DH), trs(IDX_HEADS)]
    in_specs = [
        pl.BlockSpec((1, ts, d), lambda b, t: (b, t, 0)),
        pl.BlockSpec((1, 6, d), lambda b, t: (b, 0, 0)),
        _const_spec(g.shape),
    ] + [_const_spec(w.shape) for w in ws]
    return pl.pallas_call(
        _inproj_kernel,
        grid=(bsz, s // ts),
        in_specs=in_specs,
        out_specs=out_specs,
        out_shape=out_shape,
        compiler_params=pltpu.CompilerParams(
            dimension_semantics=("arbitrary", "arbitrary"), vmem_limit_bytes=VMEM_LIMIT_BYTES),
        name="in_proj",
    )(x, mod, g, *ws)


def _online_update(s, v, acc_ref, m_ref, l_ref, idx):
    m_prev = m_ref[idx:idx + 1, :]
    m_new = jnp.maximum(m_prev, jnp.max(s, axis=0, keepdims=True))
    alpha = jnp.exp(m_prev - m_new)
    p = jnp.exp(s - m_new)
    l_ref[idx:idx + 1, :] = alpha * l_ref[idx:idx + 1, :] + jnp.sum(p, axis=0, keepdims=True)
    m_ref[idx:idx + 1, :] = m_new
    pv = jnp.dot(v, p.astype(BF16), preferred_element_type=F32)
    acc_ref[idx] = alpha * acc_ref[idx] + pv


def _attn_kernel(lam_init, aq_ref, bq_ref, iq_ref, iw_ref, ak_ref, av_ref, bkk_ref, ikk_ref,
                 bv_ref, lamv_ref, g_ref, out_ref,
                 key_ref, dl_ref, dlim_ref, qa_ref, qb_ref, qi_ref, w_ref,
                 acc_a_ref, m_a_ref, l_a_ref, acc_b_ref, m_b_ref, l_b_ref):
    i = pl.program_id(1)
    nkv = i + 1
    slopes_a = _alibi_slopes(A_HEADS)
    slopes_b = _alibi_slopes(B_HEADS)

    row = lax.broadcasted_iota(I32, (TK, TQ), 0)
    col = lax.broadcasted_iota(I32, (TK, TQ), 1)
    dl_ref[...] = (col - row).astype(F32)
    dlim_ref[...] = (col // CHUNK + 1) * CHUNK - row

    lane = lax.broadcasted_iota(I32, (TQ, 128), 1)
    lo = lane < 64

    def halves(tile_bf16, scale):
        t = tile_bf16.astype(F32) * scale
        z = jnp.zeros_like(t)
        return jnp.where(lo, t, z).astype(BF16), jnp.where(lo, z, t).astype(BF16)

    qscale = A_DH ** -0.5
    for h in range(A_HEADS):
        q1, q2 = halves(aq_ref[0, :, h * 128:(h + 1) * 128], qscale)
        qa_ref[2 * h] = q1
        qa_ref[2 * h + 1] = q2
    for t in range(B_HEADS // 2):
        q1, q2 = halves(bq_ref[0, :, t * 128:(t + 1) * 128], B_DH ** -0.5)
        qb_ref[2 * t] = q1
        qb_ref[2 * t + 1] = q2
    for t in range(IDX_HEADS // 2):
        q1, q2 = halves(iq_ref[0, :, t * 128:(t + 1) * 128], 1.0)
        qi_ref[2 * t] = q1
        qi_ref[2 * t + 1] = q2
    w_ref[...] = iw_ref[0, 0] * ((IDX_HEADS * IDX_DH) ** -0.5)

    acc_a_ref[...] = jnp.zeros_like(acc_a_ref)
    acc_b_ref[...] = jnp.zeros_like(acc_b_ref)
    m_a_ref[...] = jnp.full_like(m_a_ref, NEG)
    m_b_ref[...] = jnp.full_like(m_b_ref, NEG)
    l_a_ref[...] = jnp.zeros_like(l_a_ref)
    l_b_ref[...] = jnp.zeros_like(l_b_ref)

    def block_geometry(j):
        base = ((i - j) * TK).astype(F32)
        dist = jnp.abs(dl_ref[...] + base)
        off = jnp.where(j < i, jnp.int32(1 << 20), jnp.int32(0))
        allowed = (dlim_ref[...] + off) > 0
        return dist, allowed

    def phase1(j, carry):
        k0 = pl.multiple_of(j * TK, TK)
        dist, allowed = block_geometry(j)
        kk = ikk_ref[0, pl.ds(k0, TK), :]

        def qk(n):
            h = n // 2
            kh = ak_ref[0, pl.ds(k0, TK), h * 128:(h + 1) * 128]
            return lax.dot_general(kh, qa_ref[n], _NT, preferred_element_type=F32)

        score = jnp.zeros((TK, TQ), F32)
        s_next = qk(0)
        idx_per_map = IDX_HEADS // (2 * A_HEADS)
        for n in range(2 * A_HEADS):
            s_cur = s_next
            for hh in range(n * idx_per_map, (n + 1) * idx_per_map):
                xh = lax.dot_general(kk, qi_ref[hh], _NT, preferred_element_type=F32)
                score = score + w_ref[hh:hh + 1, :] * jnp.maximum(xh, 0.0)
            if n + 1 < 2 * A_HEADS:
                s_next = qk(n + 1)
            h = n // 2
            s = jnp.where(allowed, s_cur - slopes_a[h] * dist, NEG)
            vh = av_ref[0, j, h * 128:(h + 1) * 128, :]
            _online_update(s, vh, acc_a_ref, m_a_ref, l_a_ref, n)
        bits = pltpu.bitcast(score, I32)
        key = bits ^ ((bits >> 31) & jnp.int32(0x7FFFFFFF))
        key_ref[pl.ds(k0, TK), :] = jnp.where(allowed, key, jnp.int32(INT_MIN))
        return carry

    lax.fori_loop(0, nkv, phase1, 0)

    def count_ge(cand):
        def body(c, cnt):
            kt = key_ref[pl.ds(pl.multiple_of(c * TK, TK), TK), :]
            ind = jnp.where(kt >= cand, jnp.int32(1), jnp.int32(0))
            return cnt + jnp.sum(ind.reshape(TK // 8, 8, TQ), axis=0)
        cnt8 = lax.fori_loop(0, nkv, body, jnp.zeros((8, TQ), I32))
        return jnp.sum(cnt8, axis=0, keepdims=True)

    def bit_body(t, carry):
        pref, cnt_at = carry
        bit = jnp.left_shift(jnp.int32(1), 31 - t)
        trial = pref | bit
        cnt = count_ge(trial ^ jnp.int32(INT_MIN))
        take = cnt >= TOPK_MAX
        return jnp.where(take, trial, pref), jnp.where(take, cnt, cnt_at)

    pref, cnt_at = lax.fori_loop(
        0, 32, bit_body, (jnp.zeros((1, TQ), I32), jnp.full((1, TQ), TOPK_MAX, I32)))
    tau = jnp.maximum(pref ^ jnp.int32(INT_MIN), jnp.int32(INT_MIN + 1))

    @pl.when(jnp.max(cnt_at) > TOPK_MAX)
    def _():
        def count_where(pred_fn):
            def body(c, cnt):
                k0 = pl.multiple_of(c * TK, TK)
                kt = key_ref[pl.ds(k0, TK), :]
                pos = lax.broadcasted_iota(I32, (TK, TQ), 0) + c * TK
                ind = jnp.where(pred_fn(kt, pos), jnp.int32(1), jnp.int32(0))
                return cnt + jnp.sum(ind.reshape(TK // 8, 8, TQ), axis=0)
            cnt8 = lax.fori_loop(0, nkv, body, jnp.zeros((8, TQ), I32))
            return jnp.sum(cnt8, axis=0, keepdims=True)

        need = TOPK_MAX - count_where(lambda kt, pos: kt > tau)

        def idx_body(t, bound):
            trial = bound | jnp.left_shift(jnp.int32(1), 13 - t)
            cnt = count_where(lambda kt, pos: jnp.where(kt == tau, pos, jnp.int32(1 << 30)) < trial)
            return jnp.where(cnt <= need, trial, bound)

        bound = lax.fori_loop(0, 14, idx_body, jnp.zeros((1, TQ), I32))

        def demote(c, carry):
            k0 = pl.multiple_of(c * TK, TK)
            kt = key_ref[pl.ds(k0, TK), :]
            pos = lax.broadcasted_iota(I32, (TK, TQ), 0) + c * TK
            drop = jnp.where(kt == tau, pos, jnp.int32(-1)) >= bound
            key_ref[pl.ds(k0, TK), :] = jnp.where(drop, kt - 1, kt)
            return carry

        lax.fori_loop(0, nkv, demote, 0)

    def phase3(j, carry):
        k0 = pl.multiple_of(j * TK, TK)
        dist, _ = block_geometry(j)
        sel = key_ref[pl.ds(k0, TK), :] >= tau
        kb = bkk_ref[0, pl.ds(k0, TK), :]
        vb = bv_ref[0, j]

        def qk(h):
            return lax.dot_general(kb, qb_ref[h], _NT, preferred_element_type=F32)

        s_next = qk(0)
        for h in range(B_HEADS):
            s_cur = s_next
            if h + 1 < B_HEADS:
                s_next = qk(h + 1)
            s = jnp.where(sel, s_cur - slopes_b[h] * dist, NEG)
            _online_update(s, vb, acc_b_ref, m_b_ref, l_b_ref, h)
        return carry

    lax.fori_loop(0, nkv, phase3, 0)

    lv = lamv_ref[...]
    lam = (jnp.exp(jnp.sum(lv[0:1] * lv[1:2], axis=1, keepdims=True))
           - jnp.exp(jnp.sum(lv[2:3] * lv[3:4], axis=1, keepdims=True)) + lam_init)
    g = g_ref[...]
    for h in range(A_HEADS):
        o1 = acc_a_ref[2 * h] / l_a_ref[2 * h:2 * h + 1, :]
        o2 = acc_a_ref[2 * h + 1] / l_a_ref[2 * h + 1:2 * h + 2, :]
        o = o1 - lam * o2
        y = o * lax.rsqrt(jnp.mean(o * o, axis=0, keepdims=True) + EPS)
        out_ref[0, 0, h * 128:(h + 1) * 128, :] = ((y * g) * (1.0 - lam_init)).astype(BF16)
    a_rows = A_HEADS * 2 * A_DH
    for h in range(B_HEADS):
        ob = acc_b_ref[h] / l_b_ref[h:h + 1, :]
        out_ref[0, 0, a_rows + h * B_DH:a_rows + (h + 1) * B_DH, :] = ob.astype(BF16)


def _attention(lam_init, aq, bq, iq, iwt, ak, avt, bkk, ikk, bvt, lamv, gt):
    bsz, s, a_cols = aq.shape
    nq = s // TQ
    nk = s // TK
    b_cols = bq.shape[2]
    i_cols = iq.shape[2]
    d_mix = a_cols + b_cols
    in_specs = [
        pl.BlockSpec((1, TQ, a_cols), lambda b, i: (b, i, 0)),
        pl.BlockSpec((1, TQ, b_cols), lambda b, i: (b, i, 0)),
        pl.BlockSpec((1, TQ, i_cols), lambda b, i: (b, i, 0)),
        pl.BlockSpec((1, 1, IDX_HEADS, TQ), lambda b, i: (b, i, 0, 0)),
        pl.BlockSpec((1, s, a_cols), lambda b, i: (b, 0, 0)),
        pl.BlockSpec((1, nk, a_cols, TK), lambda b, i: (b, 0, 0, 0)),
        pl.BlockSpec((1, s, 128), lambda b, i: (b, 0, 0)),
        pl.BlockSpec((1, s, 128), lambda b, i: (b, 0, 0)),
        pl.BlockSpec((1, nk, B_DH, TK), lambda b, i: (b, 0, 0, 0)),
        _const_spec(lamv.shape),
        _const_spec(gt.shape),
    ]
    scratch = [
        pltpu.VMEM((s, TQ), I32),
        pltpu.VMEM((TK, TQ), F32),
        pltpu.VMEM((TK, TQ), I32),
        pltpu.VMEM((2 * A_HEADS, TQ, 128), BF16),
        pltpu.VMEM((B_HEADS, TQ, 128), BF16),
        pltpu.VMEM((IDX_HEADS, TQ, 128), BF16),
        pltpu.VMEM((IDX_HEADS, TQ), F32),
        pltpu.VMEM((2 * A_HEADS, 2 * A_DH, TQ), F32),
        pltpu.VMEM((2 * A_HEADS, TQ), F32),
        pltpu.VMEM((2 * A_HEADS, TQ), F32),
        pltpu.VMEM((B_HEADS, B_DH, TQ), F32),
        pltpu.VMEM((B_HEADS, TQ), F32),
        pltpu.VMEM((B_HEADS, TQ), F32),
    ]
    return pl.pallas_call(
        functools.partial(_attn_kernel, lam_init),
        grid=(bsz, nq),
        in_specs=in_specs,
        out_specs=pl.BlockSpec((1, 1, d_mix, TQ), lambda b, i: (b, i, 0, 0)),
        out_shape=jax.ShapeDtypeStruct((bsz, nq, d_mix, TQ), BF16),
        scratch_shapes=scratch,
        compiler_params=pltpu.CompilerParams(
            dimension_semantics=("arbitrary", "arbitrary"), vmem_limit_bytes=VMEM_LIMIT_BYTES),
        name="attention",
    )(aq, bq, iq, iwt, ak, avt, bkk, ikk, bvt, lamv, gt)


def _ffn_kernel(final, mix_ref, x_ref, mod_ref, n2_ref, fg_ref, wo_ref, w1_ref, w2_ref, o_ref):
    wo = wo_ref[...]
    y = jnp.concatenate(
        [lax.dot_general(mix_ref[0, c], wo, _TN, preferred_element_type=F32)
         for c in range(TM_FFN // TQ)], axis=0)
    g1 = mod_ref[0, 2:3, :]
    sh2 = mod_ref[0, 3:4, :]
    sc2 = mod_ref[0, 4:5, :]
    g2 = mod_ref[0, 5:6, :]
    x1 = x_ref[0] + g1 * y
    h2 = ((_rms(x1) * n2_ref[...]) * (1.0 + sc2) + sh2).astype(BF16)
    ff = jnp.zeros_like(x1)
    for c in range(w1_ref.shape[1] // FF_CHUNK):
        u = jnp.dot(h2, w1_ref[:, c * FF_CHUNK:(c + 1) * FF_CHUNK], preferred_element_type=F32)
        u = jnp.square(jnp.maximum(u, 0.0)).astype(BF16)
        ff = ff + jnp.dot(u, w2_ref[c * FF_CHUNK:(c + 1) * FF_CHUNK, :], preferred_element_type=F32)
    x2 = x1 + g2 * ff
    if final:
        x2 = _rms(x2) * fg_ref[...]
    o_ref[0] = x2


def _out_ffn(final, mixt, x, mod, n2g, fg, wo, w1, w2):
    bsz, s, d = x.shape
    d_mix = mixt.shape[2]
    tm = TM_FFN
    cpt = tm // TQ
    return pl.pallas_call(
        functools.partial(_ffn_kernel, final),
        grid=(bsz, s // tm),
        in_specs=[
            pl.BlockSpec((1, cpt, d_mix, TQ), lambda b, t: (b, t, 0, 0)),
            pl.BlockSpec((1, tm, d), lambda b, t: (b, t, 0)),
            pl.BlockSpec((1, 6, d), lambda b, t: (b, 0, 0)),
            _const_spec(n2g.shape),
            _const_spec(fg.shape),
            _const_spec(wo.shape),
            _const_spec(w1.shape),
            _const_spec(w2.shape),
        ],
        out_specs=pl.BlockSpec((1, tm, d), lambda b, t: (b, t, 0)),
        out_shape=jax.ShapeDtypeStruct((bsz, s, d), F32),
        compiler_params=pltpu.CompilerParams(
            dimension_semantics=("arbitrary", "arbitrary"), vmem_limit_bytes=VMEM_LIMIT_BYTES),
        name="out_ffn",
    )(mixt, x, mod, n2g, fg, wo, w1, w2)


def _split_w_in(w):
    a = A_HEADS * 2 * A_DH
    sizes = (a, a, a, B_HEADS * B_DH, B_DH, B_DH, IDX_HEADS * IDX_DH, IDX_DH, IDX_HEADS)
    offs = np.cumsum((0,) + sizes)
    aq, ak, av, bq, bk, bv, iq, ik, iw = [w[:, offs[n]:offs[n + 1]] for n in range(len(sizes))]
    kk = jnp.concatenate([bk, bk, ik, ik], axis=1)
    nat = [aq, ak, bq, iq, kk]
    trn = [av.T, bv.T, iw.T]
    return [m.astype(BF16) for m in nat + trn]


def kernel(x, c, norm1_g, norm2_g, w_ada, b_ada, w_in, lam_q1, lam_k1, lam_q2, lam_k2,
           subln_g, w_out, w_ff1, w_ff2, final_g):
    bsz, s, d = x.shape
    depth = w_in.shape[0]
    assert s % TS_PROJ == 0 and s % TM_FFN == 0 and s % TQ == 0 and TQ == TK
    assert min(TOPK_MAX, s // 4) == TOPK_MAX
    fg = final_g.reshape(1, d)
    for l in range(depth):
        lam_init = 0.8 - 0.6 * math.exp(-0.3 * l)
        mod = _adaln_mod(c, w_ada[l].astype(BF16), b_ada[l].reshape(1, -1)).reshape(bsz, 6, d)
        aq, ak, bq, iq, bkk, ikk, avt, bvt, iwt = _in_proj(
            x, mod, norm1_g[l].reshape(1, d), _split_w_in(w_in[l]))
        lamv = jnp.stack([lam_q1[l], lam_k1[l], lam_q2[l], lam_k2[l]])
        gt = jnp.broadcast_to(subln_g[l][:, None], (2 * A_DH, TQ))
        mixt = _attention(lam_init, aq, bq, iq, iwt, ak, avt, bkk, ikk, bvt, lamv, gt)
        x = _out_ffn(l == depth - 1, mixt, x, mod, norm2_g[l].reshape(1, d), fg,
                     w_out[l].astype(BF16), w_ff1[l].astype(BF16), w_ff2[l].astype(BF16))
    return x
```

```python
import functools
import math

import jax
import jax.numpy as jnp
import numpy as np
from jax import lax
from jax.experimental import pallas as pl
from jax.experimental.pallas import tpu as pltpu

CHUNK = 64
CHUNK_SHIFT = 6
A_HEADS = 4
A_DH = 64
B_HEADS = 8
B_DH = 64
IDX_HEADS = 16
IDX_DH = 64
TOPK_MAX = 256
EPS = 1e-6

TQ = 256
TK = 256
TS_PROJ = 512
TM_FFN = 512
FF_CHUNK = 1024
VMEM_LIMIT_BYTES = 56 * 1024 * 1024
POS_SHIFT = 6
POS_SPLIT = 1 << POS_SHIFT

NEG = -1e30
BIG = 3e38
INT_MIN = -(2 ** 31)
F32_LOWEST = float(np.finfo(np.float32).min)
BF16 = jnp.bfloat16
F32 = jnp.float32
I32 = jnp.int32

_NT = (((1,), (1,)), ((), ()))
_TN = (((0,), (0,)), ((), ()))


def _alibi_slopes(n):
    return [2.0 ** (-8.0 * (i + 1) / n) for i in range(n)]


def _const_spec(shape):
    nd = len(shape)
    return pl.BlockSpec(shape, lambda *_: (0,) * nd, pipeline_mode=pl.Buffered(1))


def _rms(x):
    return x * lax.rsqrt(jnp.mean(x * x, axis=-1, keepdims=True) + EPS)


def _mod_kernel(c_ref, w_ref, b_ref, o_ref):
    c = c_ref[...]
    cond = c * (1.0 / (1.0 + jnp.exp(-c)))
    o_ref[...] = jnp.dot(cond.astype(BF16), w_ref[...], preferred_element_type=F32) + b_ref[...]


def _adaln_mod(c, w_ada, b_ada):
    bsz, d = c.shape
    n = w_ada.shape[1]
    return pl.pallas_call(
        _mod_kernel,
        grid=(n // d,),
        in_specs=[
            pl.BlockSpec((bsz, d), lambda j: (0, 0)),
            pl.BlockSpec((d, d), lambda j: (0, j)),
            pl.BlockSpec((1, d), lambda j: (0, j)),
        ],
        out_specs=pl.BlockSpec((bsz, d), lambda j: (0, j)),
        out_shape=jax.ShapeDtypeStruct((bsz, n), F32),
        compiler_params=pltpu.CompilerParams(dimension_semantics=("arbitrary",)),
        name="adaln_mod",
    )(c, w_ada, b_ada)


def _inproj_kernel(x_ref, mod_ref, g_ref, waq_ref, wak_ref, wbq_ref, wiq_ref, wkk_ref,
                   wav_ref, wbv_ref, wiw_ref, bqc_ref,
                   aq_ref, akp_ref, bqp_ref, iq_ref, bkp_ref, ikk_ref, av_ref, bv_ref, iw_ref):
    x = x_ref[0]
    sh1 = mod_ref[0, 0:1, :]
    sc1 = mod_ref[0, 1:2, :]
    h = (_rms(x) * g_ref[...]) * (1.0 + sc1) + sh1
    hb = h.astype(BF16)

    def nat(w_ref):
        return jnp.dot(hb, w_ref[...], preferred_element_type=F32)

    pos = pl.program_id(1) * TS_PROJ + lax.broadcasted_iota(I32, (TS_PROJ, 128), 0)
    lane = lax.broadcasted_iota(I32, (TS_PROJ, 128), 1)
    pos_hi = (pos >> POS_SHIFT).astype(F32)
    pos_lo = (pos & (POS_SPLIT - 1)).astype(F32)

    def pos_tile(l0):
        return jnp.where(lane == l0, pos_hi, jnp.where(lane == l0 + 1, pos_lo, 0.0))

    aq_ref[0] = (nat(waq_ref) * (A_DH ** -0.5)).astype(BF16)
    ak = nat(wak_ref)
    pos0 = pos_tile(0).astype(BF16)
    for hd in range(A_HEADS):
        akp_ref[0, :, hd * 256:hd * 256 + 128] = ak[:, hd * 128:(hd + 1) * 128].astype(BF16)
        akp_ref[0, :, hd * 256 + 128:(hd + 1) * 256] = pos0
    bqp_ref[0] = (nat(wbq_ref) * (B_DH ** -0.5) + bqc_ref[...]).astype(BF16)
    iq_ref[0] = nat(wiq_ref).astype(BF16)
    kk = nat(wkk_ref)
    bkp_ref[0] = (kk[:, 0:128] + pos_tile(B_DH)).astype(BF16)
    ikk_ref[0] = kk[:, 128:256].astype(BF16)

    def tr(w_ref):
        return lax.dot_general(w_ref[...], hb, _NT, preferred_element_type=F32)

    avt = tr(wav_ref).astype(BF16)
    bvt = tr(wbv_ref).astype(BF16)
    iwt = tr(wiw_ref)
    for c in range(TS_PROJ // TK):
        av_ref[0, c] = avt[:, c * TK:(c + 1) * TK]
        bv_ref[0, c] = bvt[:, c * TK:(c + 1) * TK]
        iw_ref[0, c] = iwt[:, c * TK:(c + 1) * TK]


def _in_proj(x, mod, g, ws, bqc):
    bsz, s, d = x.shape
    ts = TS_PROJ
    nk = s // TK
    cpt = ts // TK

    def tok(cols):
        return pl.BlockSpec((1, ts, cols), lambda b, t: (b, t, 0))

    def trs(rows):
        return pl.BlockSpec((1, cpt, rows, TK), lambda b, t: (b, t, 0, 0))

    a_cols = A_HEADS * 2 * A_DH
    i_cols = IDX_HEADS * IDX_DH
    out_shape = [
        jax.ShapeDtypeStruct((bsz, s, a_cols), BF16),
        jax.ShapeDtypeStruct((bsz, s, A_HEADS * 256), BF16),
        jax.ShapeDtypeStruct((bsz, s, B_HEADS * 128), BF16),
        jax.ShapeDtypeStruct((bsz, s, i_cols), BF16),
        jax.ShapeDtypeStruct((bsz, s, 128), BF16),
        jax.ShapeDtypeStruct((bsz, s, 128), BF16),
        jax.ShapeDtypeStruct((bsz, nk, a_cols, TK), BF16),
        jax.ShapeDtypeStruct((bsz, nk, B_DH, TK), BF16),
        jax.ShapeDtypeStruct((bsz, nk, IDX_HEADS, TK), F32),
    ]
    out_specs = [tok(a_cols), tok(A_HEADS * 256), tok(B_HEADS * 128), tok(i_cols), tok(128),
                 tok(128), trs(a_cols), trs(B_DH), trs(IDX_HEADS)]
    in_specs = [
        pl.BlockSpec((1, ts, d), lambda b, t: (b, t, 0)),
        pl.BlockSpec((1, 6, d), lambda b, t: (b, 0, 0)),
        _const_spec(g.shape),
    ] + [_const_spec(w.shape) for w in ws] + [_const_spec(bqc.shape)]
    return pl.pallas_call(
        _inproj_kernel,
        grid=(bsz, s // ts),
        in_specs=in_specs,
        out_specs=out_specs,
        out_shape=out_shape,
        compiler_params=pltpu.CompilerParams(
            dimension_semantics=("arbitrary", "arbitrary"), vmem_limit_bytes=VMEM_LIMIT_BYTES),
        name="in_proj",
    )(x, mod, g, *ws, bqc)


def _online_update(s_all, v, acc_ref, m_ref, l_ref, idx0, n, prep=None):
    ps, alphas = [], []
    for g in range(n):
        s = s_all[:, g * TQ:(g + 1) * TQ]
        if prep is not None:
            s = prep(g, s)
        m_prev = m_ref[idx0 + g]
        m_new = jnp.maximum(m_prev, jnp.max(s, axis=0, keepdims=True))
        alpha = jnp.exp(m_prev - m_new)
        p = jnp.exp(s - m_new)
        l_ref[idx0 + g] = alpha * l_ref[idx0 + g] + jnp.sum(p, axis=0, keepdims=True)
        m_ref[idx0 + g] = m_new
        ps.append(p.astype(BF16))
        alphas.append(alpha)
    pv = jnp.dot(v, jnp.concatenate(ps, axis=1), preferred_element_type=F32)
    for g in range(n):
        acc_ref[idx0 + g] = alphas[g] * acc_ref[idx0 + g] + pv[:, g * TQ:(g + 1) * TQ]


def _key_to_f32(key):
    return pltpu.bitcast(key ^ ((key >> 31) & jnp.int32(0x7FFFFFFF)), F32)


def _f32_to_key(x):
    bits = pltpu.bitcast(x, I32)
    return bits ^ ((bits >> 31) & jnp.int32(0x7FFFFFFF))


def _attn_kernel(lam_init, aq_ref, bqp_ref, iq_ref, iw_ref, akp_ref, av_ref, bkp_ref, ikk_ref,
                 bv_ref, lamv_ref, g_ref, out_ref,
                 score_ref, qa_ref, qi_ref, qb_ref, w_ref,
                 acc_a_ref, m_a_ref, l_a_ref, acc_b_ref, m_b_ref, l_b_ref):
    i = pl.program_id(1)
    nkv = i + 1
    slopes_a = _alibi_slopes(A_HEADS)
    slopes_b = _alibi_slopes(B_HEADS)
    n_maps = 2 * A_HEADS

    lane = lax.broadcasted_iota(I32, (TQ, 128), 1)
    keep_lo = jnp.where(lane < 64, 1.0, 0.0).astype(BF16)
    keep_hi = jnp.where(lane < 64, 0.0, 1.0).astype(BF16)
    idx_grp = IDX_HEADS // A_HEADS
    for h in range(A_HEADS):
        qh = aq_ref[0, :, h * 128:(h + 1) * 128]
        slope_cols = jnp.where(
            lane == 0, slopes_a[h] * POS_SPLIT, jnp.where(lane == 1, slopes_a[h], 0.0)).astype(BF16)
        qa_ref[h, 0:TQ, 0:128] = qh * keep_lo
        qa_ref[h, TQ:2 * TQ, 0:128] = qh * keep_hi
        qa_ref[h, 0:TQ, 128:256] = slope_cols
        qa_ref[h, TQ:2 * TQ, 128:256] = slope_cols
    for t in range(IDX_HEADS // 2):
        qt = iq_ref[0, :, t * 128:(t + 1) * 128]
        for half, keep in enumerate((keep_lo, keep_hi)):
            g, k = divmod(2 * t + half, idx_grp)
            qi_ref[g, k * TQ:(k + 1) * TQ, :] = qt * keep
    for h in range(B_HEADS):
        qb_ref[h * TQ:(h + 1) * TQ, :] = bqp_ref[0, :, h * 128:(h + 1) * 128]
    w_ref[...] = iw_ref[0, 0] * ((IDX_HEADS * IDX_DH) ** -0.5)

    acc_a_ref[...] = jnp.zeros_like(acc_a_ref)
    acc_b_ref[...] = jnp.zeros_like(acc_b_ref)
    m_a_ref[...] = jnp.full_like(m_a_ref, NEG)
    m_b_ref[...] = jnp.full_like(m_b_ref, NEG)
    l_a_ref[...] = jnp.zeros_like(l_a_ref)
    l_b_ref[...] = jnp.zeros_like(l_b_ref)

    def diag_geometry():
        row = lax.broadcasted_iota(I32, (TK, TQ), 0)
        col = lax.broadcasted_iota(I32, (TK, TQ), 1)
        allowed = row < ((col >> CHUNK_SHIFT) + 1) * CHUNK
        corr = 2.0 * jnp.minimum((col - row).astype(F32), 0.0)
        return allowed, jnp.where(allowed, BIG, NEG), corr

    def phase1_block(j, diag):
        k0 = pl.multiple_of(j * TK, TK)
        kk = ikk_ref[0, pl.ds(k0, TK), :]
        if diag:
            allowed, cap, corr = diag_geometry()

        def qk(h):
            kh = akp_ref[0, pl.ds(k0, TK), h * 256:(h + 1) * 256]
            return lax.dot_general(kh, qa_ref[h], _NT, preferred_element_type=F32)

        score = jnp.zeros((TK, TQ), F32)
        s_next = qk(0)
        for h in range(A_HEADS):
            s_pair = s_next
            xg = lax.dot_general(kk, qi_ref[h], _NT, preferred_element_type=F32)
            for k in range(idx_grp):
                hh = h * idx_grp + k
                score = score + w_ref[hh:hh + 1, :] * jnp.maximum(xg[:, k * TQ:(k + 1) * TQ], 0.0)
            if h + 1 < A_HEADS:
                s_next = qk(h + 1)
            prep = None
            if diag:
                prep = functools.partial(
                    lambda sl, g, s: jnp.minimum(s, cap) + sl * corr, slopes_a[h])
            vh = av_ref[0, j, h * 128:(h + 1) * 128, :]
            _online_update(s_pair, vh, acc_a_ref, m_a_ref, l_a_ref, 2 * h, 2, prep)
        if diag:
            score = jnp.where(allowed, score, -jnp.inf)
        score_ref[pl.ds(k0, TK), :] = score

    def phase1_body(j, carry):
        phase1_block(j, False)
        return carry

    lax.fori_loop(0, i, phase1_body, 0)
    phase1_block(i, True)

    qpos = i * TQ + lax.broadcasted_iota(I32, (1, TQ), 1)
    n_adm = ((qpos >> CHUNK_SHIFT) + 1) * CHUNK
    searching = n_adm > TOPK_MAX

    def count_where(pred_fn):
        def body(c, cnt):
            k0 = pl.multiple_of(c * TK, TK)
            ind = jnp.where(pred_fn(score_ref[pl.ds(k0, TK), :], c), jnp.int32(1), jnp.int32(0))
            return cnt + jnp.sum(ind.reshape(TK // 8, 8, TQ), axis=0)
        cnt8 = lax.fori_loop(0, nkv, body, jnp.zeros((8, TQ), I32))
        return jnp.sum(cnt8, axis=0, keepdims=True)

    def search_pass(t, pref, cnt_at, active):
        trial = pref | jnp.left_shift(jnp.int32(1), 31 - t)
        cand = _key_to_f32(trial ^ jnp.int32(INT_MIN))
        cnt = count_where(lambda st, c: st >= cand)
        take = jnp.where(cnt >= TOPK_MAX, active, 0)
        pref = jnp.where(take > 0, trial, pref)
        cnt_at = jnp.where(take > 0, cnt, cnt_at)
        active = jnp.where(cnt == TOPK_MAX, active - take, active)
        return pref, cnt_at, active

    def search_cond(carry):
        t, _, _, active = carry
        return jnp.logical_and(t < 32, jnp.max(active) > 0)

    def search_body(carry):
        t, pref, cnt_at, active = carry
        pref, cnt_at, active = search_pass(t, pref, cnt_at, active)
        pref, cnt_at, active = search_pass(t + 1, pref, cnt_at, active)
        return t + 2, pref, cnt_at, active

    _, pref, cnt_at, _ = lax.while_loop(
        search_cond, search_body,
        (jnp.int32(0), jnp.zeros((1, TQ), I32), jnp.full((1, TQ), TOPK_MAX, I32),
         jnp.where(searching, jnp.int32(1), jnp.int32(0))))
    tau = jnp.where(searching, _key_to_f32(pref ^ jnp.int32(INT_MIN)), F32_LOWEST)

    @pl.when(jnp.max(cnt_at) > TOPK_MAX)
    def _():
        def pos_of(c):
            return lax.broadcasted_iota(I32, (TK, TQ), 0) + c * TK

        need = TOPK_MAX - count_where(lambda st, c: st > tau)
        n_bits = int(score_ref.shape[0]).bit_length()

        def idx_body(t, bound):
            trial = bound | jnp.left_shift(jnp.int32(1), n_bits - 1 - t)
            cnt = count_where(
                lambda st, c: jnp.where(st == tau, pos_of(c), jnp.int32(1 << 30)) < trial)
            return jnp.where(cnt <= need, trial, bound)

        bound = lax.fori_loop(0, n_bits, idx_body, jnp.zeros((1, TQ), I32))
        below = _key_to_f32(_f32_to_key(tau) - 1)

        def demote(c, carry):
            k0 = pl.multiple_of(c * TK, TK)
            st = score_ref[pl.ds(k0, TK), :]
            drop = jnp.where(st == tau, pos_of(c), jnp.int32(-1)) >= bound
            score_ref[pl.ds(k0, TK), :] = jnp.where(drop, below, st)
            return carry

        lax.fori_loop(0, nkv, demote, 0)

    def phase3_block(j, diag):
        k0 = pl.multiple_of(j * TK, TK)
        cap = jnp.where(score_ref[pl.ds(k0, TK), :] >= tau, BIG, NEG)
        if diag:
            _, _, corr = diag_geometry()
        kb = bkp_ref[0, pl.ds(k0, TK), :]
        vb = bv_ref[0, j]

        s_all = lax.dot_general(kb, qb_ref[...], _NT, preferred_element_type=F32)

        def prep(h, s):
            s = jnp.minimum(s, cap)
            return s + slopes_b[h] * corr if diag else s

        _online_update(s_all, vb, acc_b_ref, m_b_ref, l_b_ref, 0, B_HEADS, prep)

    def phase3_body(j, carry):
        phase3_block(j, False)
        return carry

    lax.fori_loop(0, i, phase3_body, 0)
    phase3_block(i, True)

    lv = lamv_ref[...]
    lam = (jnp.exp(jnp.sum(lv[0:1] * lv[1:2], axis=1, keepdims=True))
           - jnp.exp(jnp.sum(lv[2:3] * lv[3:4], axis=1, keepdims=True)) + lam_init)
    g = g_ref[...]
    for h in range(A_HEADS):
        o1 = acc_a_ref[2 * h] / l_a_ref[2 * h]
        o2 = acc_a_ref[2 * h + 1] / l_a_ref[2 * h + 1]
        o = o1 - lam * o2
        y = o * lax.rsqrt(jnp.mean(o * o, axis=0, keepdims=True) + EPS)
        out_ref[0, 0, h * 128:(h + 1) * 128, :] = ((y * g) * (1.0 - lam_init)).astype(BF16)
    a_rows = A_HEADS * 2 * A_DH
    for h in range(B_HEADS):
        ob = acc_b_ref[h] / l_b_ref[h]
        out_ref[0, 0, a_rows + h * B_DH:a_rows + (h + 1) * B_DH, :] = ob.astype(BF16)


def _attention(lam_init, aq, bqp, iq, iwt, akp, avt, bkp, ikk, bvt, lamv, gt):
    bsz, s, a_cols = aq.shape
    nq = s // TQ
    nk = s // TK
    i_cols = iq.shape[2]
    d_mix = a_cols + B_HEADS * B_DH
    in_specs = [
        pl.BlockSpec((1, TQ, a_cols), lambda b, i: (b, i, 0)),
        pl.BlockSpec((1, TQ, bqp.shape[2]), lambda b, i: (b, i, 0)),
        pl.BlockSpec((1, TQ, i_cols), lambda b, i: (b, i, 0)),
        pl.BlockSpec((1, 1, IDX_HEADS, TQ), lambda b, i: (b, i, 0, 0)),
        pl.BlockSpec((1, s, akp.shape[2]), lambda b, i: (b, 0, 0)),
        pl.BlockSpec((1, nk, a_cols, TK), lambda b, i: (b, 0, 0, 0)),
        pl.BlockSpec((1, s, 128), lambda b, i: (b, 0, 0)),
        pl.BlockSpec((1, s, 128), lambda b, i: (b, 0, 0)),
        pl.BlockSpec((1, nk, B_DH, TK), lambda b, i: (b, 0, 0, 0)),
        _const_spec(lamv.shape),
        _const_spec(gt.shape),
    ]
    n_maps = 2 * A_HEADS
    scratch = [
        pltpu.VMEM((s, TQ), F32),
        pltpu.VMEM((A_HEADS, 2 * TQ, 256), BF16),
        pltpu.VMEM((A_HEADS, IDX_HEADS // A_HEADS * TQ, 128), BF16),
        pltpu.VMEM((B_HEADS * TQ, 128), BF16),
        pltpu.VMEM((IDX_HEADS, TQ), F32),
        pltpu.VMEM((n_maps, 2 * A_DH, TQ), F32),
        pltpu.VMEM((n_maps, 1, TQ), F32),
        pltpu.VMEM((n_maps, 1, TQ), F32),
        pltpu.VMEM((B_HEADS, B_DH, TQ), F32),
        pltpu.VMEM((B_HEADS, 1, TQ), F32),
        pltpu.VMEM((B_HEADS, 1, TQ), F32),
    ]
    return pl.pallas_call(
        functools.partial(_attn_kernel, lam_init),
        grid=(bsz, nq),
        in_specs=in_specs,
        out_specs=pl.BlockSpec((1, 1, d_mix, TQ), lambda b, i: (b, i, 0, 0)),
        out_shape=jax.ShapeDtypeStruct((bsz, nq, d_mix, TQ), BF16),
        scratch_shapes=scratch,
        compiler_params=pltpu.CompilerParams(
            dimension_semantics=("arbitrary", "arbitrary"), vmem_limit_bytes=VMEM_LIMIT_BYTES),
        name="attention",
    )(aq, bqp, iq, iwt, akp, avt, bkp, ikk, bvt, lamv, gt)


def _ffn_kernel(final, mix_ref, x_ref, mod_ref, n2_ref, fg_ref, wo_ref, w1_ref, w2_ref, o_ref):
    wo = wo_ref[...]
    y = jnp.concatenate(
        [lax.dot_general(mix_ref[0, c], wo, _TN, preferred_element_type=F32)
         for c in range(TM_FFN // TQ)], axis=0)
    g1 = mod_ref[0, 2:3, :]
    sh2 = mod_ref[0, 3:4, :]
    sc2 = mod_ref[0, 4:5, :]
    g2 = mod_ref[0, 5:6, :]
    x1 = x_ref[0] + g1 * y
    h2 = ((_rms(x1) * n2_ref[...]) * (1.0 + sc2) + sh2).astype(BF16)
    ff = jnp.zeros_like(x1)
    for c in range(w1_ref.shape[1] // FF_CHUNK):
        u = jnp.dot(h2, w1_ref[:, c * FF_CHUNK:(c + 1) * FF_CHUNK], preferred_element_type=F32)
        u = jnp.square(jnp.maximum(u, 0.0)).astype(BF16)
        ff = ff + jnp.dot(u, w2_ref[c * FF_CHUNK:(c + 1) * FF_CHUNK, :], preferred_element_type=F32)
    x2 = x1 + g2 * ff
    if final:
        x2 = _rms(x2) * fg_ref[...]
    o_ref[0] = x2


def _out_ffn(final, mixt, x, mod, n2g, fg, wo, w1, w2):
    bsz, s, d = x.shape
    d_mix = mixt.shape[2]
    tm = TM_FFN
    cpt = tm // TQ
    return pl.pallas_call(
        functools.partial(_ffn_kernel, final),
        grid=(bsz, s // tm),
        in_specs=[
            pl.BlockSpec((1, cpt, d_mix, TQ), lambda b, t: (b, t, 0, 0)),
            pl.BlockSpec((1, tm, d), lambda b, t: (b, t, 0)),
            pl.BlockSpec((1, 6, d), lambda b, t: (b, 0, 0)),
            _const_spec(n2g.shape),
            _const_spec(fg.shape),
            _const_spec(wo.shape),
            _const_spec(w1.shape),
            _const_spec(w2.shape),
        ],
        out_specs=pl.BlockSpec((1, tm, d), lambda b, t: (b, t, 0)),
        out_shape=jax.ShapeDtypeStruct((bsz, s, d), F32),
        compiler_params=pltpu.CompilerParams(
            dimension_semantics=("arbitrary", "arbitrary"), vmem_limit_bytes=VMEM_LIMIT_BYTES),
        name="out_ffn",
    )(mixt, x, mod, n2g, fg, wo, w1, w2)


def _split_w_in(w):
    d = w.shape[0]
    a = A_HEADS * 2 * A_DH
    sizes = (a, a, a, B_HEADS * B_DH, B_DH, B_DH, IDX_HEADS * IDX_DH, IDX_DH, IDX_HEADS)
    offs = np.cumsum((0,) + sizes)
    aq, ak, av, bq, bk, bv, iq, ik, iw = [w[:, offs[n]:offs[n + 1]] for n in range(len(sizes))]
    bq_pad = jnp.pad(bq.reshape(d, B_HEADS, B_DH), ((0, 0), (0, 0), (0, 128 - B_DH)))
    bq_pad = bq_pad.reshape(d, B_HEADS * 128)
    kk = jnp.concatenate([bk, jnp.zeros((d, 128 - B_DH), w.dtype), ik, ik], axis=1)
    nat = [aq, ak, bq_pad, iq, kk]
    trn = [av.T, bv.T, iw.T]
    return [m.astype(BF16) for m in nat + trn]


def _dsa_slope_columns():
    row = np.zeros((1, B_HEADS * 128), np.float32)
    for h, sl in enumerate(_alibi_slopes(B_HEADS)):
        row[0, h * 128 + B_DH] = sl * POS_SPLIT
        row[0, h * 128 + B_DH + 1] = sl
    return jnp.asarray(row)


def kernel(x, c, norm1_g, norm2_g, w_ada, b_ada, w_in, lam_q1, lam_k1, lam_q2, lam_k2,
           subln_g, w_out, w_ff1, w_ff2, final_g):
    bsz, s, d = x.shape
    depth = w_in.shape[0]
    assert s % TS_PROJ == 0 and s % TM_FFN == 0 and s % TQ == 0 and TQ == TK
    assert min(TOPK_MAX, s // 4) == TOPK_MAX and s // POS_SPLIT <= 256 and CHUNK == 1 << CHUNK_SHIFT
    fg = final_g.reshape(1, d)
    bqc = _dsa_slope_columns()
    for l in range(depth):
        lam_init = 0.8 - 0.6 * math.exp(-0.3 * l)
        mod = _adaln_mod(c, w_ada[l].astype(BF16), b_ada[l].reshape(1, -1)).reshape(bsz, 6, d)
        aq, akp, bqp, iq, bkp, ikk, avt, bvt, iwt = _in_proj(
            x, mod, norm1_g[l].reshape(1, d), _split_w_in(w_in[l]), bqc)
        lamv = jnp.stack([lam_q1[l], lam_k1[l], lam_q2[l], lam_k2[l]])
        gt = jnp.broadcast_to(subln_g[l][:, None], (2 * A_DH, TQ))
        mixt = _attention(lam_init, aq, bqp, iq, iwt, akp, avt, bkp, ikk, bvt, lamv, gt)
        x = _out_ffn(l == depth - 1, mixt, x, mod, norm2_g[l].reshape(1, d), fg,
                     w_out[l].astype(BF16), w_ff1[l].astype(BF16), w_ff2[l].astype(BF16))
    return x
```

```python
import functools
import math

import jax
import jax.numpy as jnp
import numpy as np
from jax import lax
from jax.experimental import pallas as pl
from jax.experimental.pallas import tpu as pltpu

CHUNK = 64
CHUNK_SHIFT = 6
A_HEADS = 4
A_DH = 64
B_HEADS = 8
B_DH = 64
IDX_HEADS = 16
IDX_DH = 64
TOPK_MAX = 256
EPS = 1e-6

TQ = 256
TK = 256
TS_PROJ = 512
TM_FFN = 512
FF_CHUNK = 1024
VMEM_LIMIT_BYTES = 56 * 1024 * 1024
POS_SHIFT = 6
POS_SPLIT = 1 << POS_SHIFT

NEG = -1e30
BIG = 3e38
INT_MIN = -(2 ** 31)
F32_LOWEST = float(np.finfo(np.float32).min)
BF16 = jnp.bfloat16
F32 = jnp.float32
I32 = jnp.int32

_NT = (((1,), (1,)), ((), ()))
_TN = (((0,), (0,)), ((), ()))


def _alibi_slopes(n):
    return [2.0 ** (-8.0 * (i + 1) / n) for i in range(n)]


def _const_spec(shape):
    nd = len(shape)
    return pl.BlockSpec(shape, lambda *_: (0,) * nd, pipeline_mode=pl.Buffered(1))


def _rms(x):
    return x * lax.rsqrt(jnp.mean(x * x, axis=-1, keepdims=True) + EPS)


def _mod_kernel(c_ref, w_ref, b_ref, o_ref):
    c = c_ref[...]
    cond = c * (1.0 / (1.0 + jnp.exp(-c)))
    o_ref[...] = jnp.dot(cond.astype(BF16), w_ref[...], preferred_element_type=F32) + b_ref[...]


def _adaln_mod(c, w_ada, b_ada):
    bsz, d = c.shape
    n = w_ada.shape[1]
    return pl.pallas_call(
        _mod_kernel,
        grid=(n // d,),
        in_specs=[
            pl.BlockSpec((bsz, d), lambda j: (0, 0)),
            pl.BlockSpec((d, d), lambda j: (0, j)),
            pl.BlockSpec((1, d), lambda j: (0, j)),
        ],
        out_specs=pl.BlockSpec((bsz, d), lambda j: (0, j)),
        out_shape=jax.ShapeDtypeStruct((bsz, n), F32),
        compiler_params=pltpu.CompilerParams(dimension_semantics=("arbitrary",)),
        name="adaln_mod",
    )(c, w_ada, b_ada)


def _inproj_kernel(x_ref, mod_ref, g_ref, waq_ref, wak_ref, wbq_ref, wiq_ref, wkk_ref,
                   wav_ref, wbv_ref, wiw_ref, bqc_ref,
                   aq_ref, akp_ref, bqp_ref, iq_ref, bkp_ref, ikk_ref, av_ref, bv_ref, iw_ref):
    x = x_ref[0]
    sh1 = mod_ref[0, 0:1, :]
    sc1 = mod_ref[0, 1:2, :]
    h = (_rms(x) * g_ref[...]) * (1.0 + sc1) + sh1
    hb = h.astype(BF16)

    def nat(w_ref):
        return jnp.dot(hb, w_ref[...], preferred_element_type=F32)

    pos = pl.program_id(1) * TS_PROJ + lax.broadcasted_iota(I32, (TS_PROJ, 128), 0)
    lane = lax.broadcasted_iota(I32, (TS_PROJ, 128), 1)
    pos_hi = (pos >> POS_SHIFT).astype(F32)
    pos_lo = (pos & (POS_SPLIT - 1)).astype(F32)

    def pos_tile(l0):
        return jnp.where(lane == l0, pos_hi, jnp.where(lane == l0 + 1, pos_lo, 0.0))

    aq_ref[0] = (nat(waq_ref) * (A_DH ** -0.5)).astype(BF16)
    ak = nat(wak_ref)
    pos0 = pos_tile(0).astype(BF16)
    for hd in range(A_HEADS):
        akp_ref[0, :, hd * 256:hd * 256 + 128] = ak[:, hd * 128:(hd + 1) * 128].astype(BF16)
        akp_ref[0, :, hd * 256 + 128:(hd + 1) * 256] = pos0
    bqp_ref[0] = (nat(wbq_ref) * (B_DH ** -0.5) + bqc_ref[...]).astype(BF16)
    iq_ref[0] = nat(wiq_ref).astype(BF16)
    kk = nat(wkk_ref)
    bkp_ref[0] = (kk[:, 0:128] + pos_tile(B_DH)).astype(BF16)
    ikk_ref[0] = kk[:, 128:256].astype(BF16)

    def tr(w_ref):
        return lax.dot_general(w_ref[...], hb, _NT, preferred_element_type=F32)

    avt = tr(wav_ref).astype(BF16)
    bvt = tr(wbv_ref).astype(BF16)
    iwt = tr(wiw_ref)
    for c in range(TS_PROJ // TK):
        av_ref[0, c] = avt[:, c * TK:(c + 1) * TK]
        bv_ref[0, c] = bvt[:, c * TK:(c + 1) * TK]
        iw_ref[0, c] = iwt[:, c * TK:(c + 1) * TK]


def _in_proj(x, mod, g, ws, bqc):
    bsz, s, d = x.shape
    ts = TS_PROJ
    nk = s // TK
    cpt = ts // TK

    def tok(cols):
        return pl.BlockSpec((1, ts, cols), lambda b, t: (b, t, 0))

    def trs(rows):
        return pl.BlockSpec((1, cpt, rows, TK), lambda b, t: (b, t, 0, 0))

    a_cols = A_HEADS * 2 * A_DH
    i_cols = IDX_HEADS * IDX_DH
    out_shape = [
        jax.ShapeDtypeStruct((bsz, s, a_cols), BF16),
        jax.ShapeDtypeStruct((bsz, s, A_HEADS * 256), BF16),
        jax.ShapeDtypeStruct((bsz, s, B_HEADS * 128), BF16),
        jax.ShapeDtypeStruct((bsz, s, i_cols), BF16),
        jax.ShapeDtypeStruct((bsz, s, 128), BF16),
        jax.ShapeDtypeStruct((bsz, s, 128), BF16),
        jax.ShapeDtypeStruct((bsz, nk, a_cols, TK), BF16),
        jax.ShapeDtypeStruct((bsz, nk, B_DH, TK), BF16),
        jax.ShapeDtypeStruct((bsz, nk, IDX_HEADS, TK), F32),
    ]
    out_specs = [tok(a_cols), tok(A_HEADS * 256), tok(B_HEADS * 128), tok(i_cols), tok(128),
                 tok(128), trs(a_cols), trs(B_DH), trs(IDX_HEADS)]
    in_specs = [
        pl.BlockSpec((1, ts, d), lambda b, t: (b, t, 0)),
        pl.BlockSpec((1, 6, d), lambda b, t: (b, 0, 0)),
        _const_spec(g.shape),
    ] + [_const_spec(w.shape) for w in ws] + [_const_spec(bqc.shape)]
    return pl.pallas_call(
        _inproj_kernel,
        grid=(bsz, s // ts),
        in_specs=in_specs,
        out_specs=out_specs,
        out_shape=out_shape,
        compiler_params=pltpu.CompilerParams(
            dimension_semantics=("arbitrary", "arbitrary"), vmem_limit_bytes=VMEM_LIMIT_BYTES),
        name="in_proj",
    )(x, mod, g, *ws, bqc)


def _online_update(s_all, v, acc_ref, m_ref, l_ref, idx0, n, prep=None):
    ps, alphas = [], []
    for g in range(n):
        s = s_all[:, g * TQ:(g + 1) * TQ]
        if prep is not None:
            s = prep(g, s)
        m_prev = m_ref[idx0 + g]
        m_new = jnp.maximum(m_prev, jnp.max(s, axis=0, keepdims=True))
        alpha = jnp.exp(m_prev - m_new)
        p = jnp.exp(s - m_new)
        l_ref[idx0 + g] = alpha * l_ref[idx0 + g] + jnp.sum(p, axis=0, keepdims=True)
        m_ref[idx0 + g] = m_new
        ps.append(p.astype(BF16))
        alphas.append(alpha)
    pv = jnp.dot(v, jnp.concatenate(ps, axis=1), preferred_element_type=F32)
    for g in range(n):
        acc_ref[idx0 + g] = alphas[g] * acc_ref[idx0 + g] + pv[:, g * TQ:(g + 1) * TQ]


def _key_to_f32(key):
    return pltpu.bitcast(key ^ ((key >> 31) & jnp.int32(0x7FFFFFFF)), F32)


def _f32_to_key(x):
    bits = pltpu.bitcast(x, I32)
    return bits ^ ((bits >> 31) & jnp.int32(0x7FFFFFFF))


def _attn_kernel(lam_init, aq_ref, bqp_ref, iq_ref, iw_ref, akp_ref, av_ref, bkp_ref, ikk_ref,
                 bv_ref, lamv_ref, g_ref, out_ref,
                 score_ref, qa_ref, qi_ref, qb_ref, w_ref,
                 acc_a_ref, m_a_ref, l_a_ref, acc_b_ref, m_b_ref, l_b_ref):
    i = pl.program_id(1)
    nkv = i + 1
    slopes_a = _alibi_slopes(A_HEADS)
    slopes_b = _alibi_slopes(B_HEADS)
    n_maps = 2 * A_HEADS

    lane = lax.broadcasted_iota(I32, (TQ, 128), 1)
    keep_lo = jnp.where(lane < 64, 1.0, 0.0).astype(BF16)
    keep_hi = jnp.where(lane < 64, 0.0, 1.0).astype(BF16)
    idx_grp = IDX_HEADS // A_HEADS
    for h in range(A_HEADS):
        qh = aq_ref[0, :, h * 128:(h + 1) * 128]
        slope_cols = jnp.where(
            lane == 0, slopes_a[h] * POS_SPLIT, jnp.where(lane == 1, slopes_a[h], 0.0)).astype(BF16)
        qa_ref[h, 0:TQ, 0:128] = qh * keep_lo
        qa_ref[h, TQ:2 * TQ, 0:128] = qh * keep_hi
        qa_ref[h, 0:TQ, 128:256] = slope_cols
        qa_ref[h, TQ:2 * TQ, 128:256] = slope_cols
    for t in range(IDX_HEADS // 2):
        qt = iq_ref[0, :, t * 128:(t + 1) * 128]
        for half, keep in enumerate((keep_lo, keep_hi)):
            g, k = divmod(2 * t + half, idx_grp)
            qi_ref[g, k * TQ:(k + 1) * TQ, :] = qt * keep
    for h in range(B_HEADS):
        qb_ref[h * TQ:(h + 1) * TQ, :] = bqp_ref[0, :, h * 128:(h + 1) * 128]
    w_ref[...] = iw_ref[0, 0] * ((IDX_HEADS * IDX_DH) ** -0.5)

    acc_a_ref[...] = jnp.zeros_like(acc_a_ref)
    acc_b_ref[...] = jnp.zeros_like(acc_b_ref)
    m_a_ref[...] = jnp.full_like(m_a_ref, NEG)
    m_b_ref[...] = jnp.full_like(m_b_ref, NEG)
    l_a_ref[...] = jnp.zeros_like(l_a_ref)
    l_b_ref[...] = jnp.zeros_like(l_b_ref)

    def diag_geometry():
        row = lax.broadcasted_iota(I32, (TK, TQ), 0)
        col = lax.broadcasted_iota(I32, (TK, TQ), 1)
        allowed = row < ((col >> CHUNK_SHIFT) + 1) * CHUNK
        corr = 2.0 * jnp.minimum((col - row).astype(F32), 0.0)
        return allowed, jnp.where(allowed, BIG, NEG), corr

    def phase1_block(j, diag):
        k0 = pl.multiple_of(j * TK, TK)
        kk = ikk_ref[0, pl.ds(k0, TK), :]
        if diag:
            allowed, cap, corr = diag_geometry()

        def qk(h):
            kh = akp_ref[0, pl.ds(k0, TK), h * 256:(h + 1) * 256]
            return lax.dot_general(kh, qa_ref[h], _NT, preferred_element_type=F32)

        score = jnp.zeros((TK, TQ), F32)
        s_next = qk(0)
        for h in range(A_HEADS):
            s_pair = s_next
            xg = lax.dot_general(kk, qi_ref[h], _NT, preferred_element_type=F32)
            for k in range(idx_grp):
                hh = h * idx_grp + k
                score = score + w_ref[hh:hh + 1, :] * jnp.maximum(xg[:, k * TQ:(k + 1) * TQ], 0.0)
            if h + 1 < A_HEADS:
                s_next = qk(h + 1)
            prep = None
            if diag:
                prep = functools.partial(
                    lambda sl, g, s: jnp.minimum(s, cap) + sl * corr, slopes_a[h])
            vh = av_ref[0, j, h * 128:(h + 1) * 128, :]
            _online_update(s_pair, vh, acc_a_ref, m_a_ref, l_a_ref, 2 * h, 2, prep)
        if diag:
            score = jnp.where(allowed, score, -jnp.inf)
        score_ref[j] = score

    def phase1_body(j, carry):
        phase1_block(j, False)
        return carry

    lax.fori_loop(0, i, phase1_body, 0)
    phase1_block(i, True)

    qpos = i * TQ + lax.broadcasted_iota(I32, (1, TQ), 1)
    n_adm = ((qpos >> CHUNK_SHIFT) + 1) * CHUNK
    searching = n_adm > TOPK_MAX

    def count_where(pred_fn):
        def body(c, cnt):
            k0 = pl.multiple_of(c * TK, TK)
            ind = jnp.where(pred_fn(score_ref[c], c), jnp.int32(1), jnp.int32(0))
            return cnt + jnp.sum(ind.reshape(TK // 8, 8, TQ), axis=0)
        cnt8 = lax.fori_loop(0, nkv, body, jnp.zeros((8, TQ), I32))
        return jnp.sum(cnt8, axis=0, keepdims=True)

    n_acc = 4

    def count_ge(cand):
        def body(c, accs):
            k0 = pl.multiple_of(c * TK, TK)
            accs = list(accs)
            for r in range(TK // 8):
                a = accs[r % n_acc]
                accs[r % n_acc] = jnp.where(score_ref[c, r * 8:(r + 1) * 8, :] >= cand, a + 1, a)
            return tuple(accs)
        accs = lax.fori_loop(0, nkv, body,
                             tuple(jnp.zeros((8, TQ), I32) for _ in range(n_acc)))
        return jnp.sum(functools.reduce(lambda a, b: a + b, accs), axis=0, keepdims=True)

    def search_pass(t, carry):
        pref, cnt_at = carry
        trial = pref | jnp.left_shift(jnp.int32(1), 31 - t)
        cnt = count_ge(_key_to_f32(trial ^ jnp.int32(INT_MIN)))
        take = cnt >= TOPK_MAX
        return jnp.where(take, trial, pref), jnp.where(take, cnt, cnt_at)

    pref, cnt_at = lax.fori_loop(
        0, 32, search_pass, (jnp.zeros((1, TQ), I32), jnp.full((1, TQ), TOPK_MAX, I32)))
    tau = jnp.where(searching, _key_to_f32(pref ^ jnp.int32(INT_MIN)), F32_LOWEST)

    @pl.when(jnp.max(cnt_at) > TOPK_MAX)
    def _():
        def pos_of(c):
            return lax.broadcasted_iota(I32, (TK, TQ), 0) + c * TK

        need = TOPK_MAX - count_where(lambda st, c: st > tau)
        n_bits = int(score_ref.shape[0] * TK).bit_length()

        def idx_body(t, bound):
            trial = bound | jnp.left_shift(jnp.int32(1), n_bits - 1 - t)
            cnt = count_where(
                lambda st, c: jnp.where(st == tau, pos_of(c), jnp.int32(1 << 30)) < trial)
            return jnp.where(cnt <= need, trial, bound)

        bound = lax.fori_loop(0, n_bits, idx_body, jnp.zeros((1, TQ), I32))
        below = _key_to_f32(_f32_to_key(tau) - 1)

        def demote(c, carry):
            k0 = pl.multiple_of(c * TK, TK)
            st = score_ref[c]
            drop = jnp.where(st == tau, pos_of(c), jnp.int32(-1)) >= bound
            score_ref[c] = jnp.where(drop, below, st)
            return carry

        lax.fori_loop(0, nkv, demote, 0)

    def phase3_block(j, diag):
        k0 = pl.multiple_of(j * TK, TK)
        cap = jnp.where(score_ref[j] >= tau, BIG, NEG)
        if diag:
            _, _, corr = diag_geometry()
        kb = bkp_ref[0, pl.ds(k0, TK), :]
        vb = bv_ref[0, j]

        s_all = lax.dot_general(kb, qb_ref[...], _NT, preferred_element_type=F32)

        def prep(h, s):
            s = jnp.minimum(s, cap)
            return s + slopes_b[h] * corr if diag else s

        _online_update(s_all, vb, acc_b_ref, m_b_ref, l_b_ref, 0, B_HEADS, prep)

    def phase3_body(j, carry):
        phase3_block(j, False)
        return carry

    lax.fori_loop(0, i, phase3_body, 0)
    phase3_block(i, True)

    lv = lamv_ref[...]
    lam = (jnp.exp(jnp.sum(lv[0:1] * lv[1:2], axis=1, keepdims=True))
           - jnp.exp(jnp.sum(lv[2:3] * lv[3:4], axis=1, keepdims=True)) + lam_init)
    g = g_ref[...]
    for h in range(A_HEADS):
        o1 = acc_a_ref[2 * h] / l_a_ref[2 * h]
        o2 = acc_a_ref[2 * h + 1] / l_a_ref[2 * h + 1]
        o = o1 - lam * o2
        y = o * lax.rsqrt(jnp.mean(o * o, axis=0, keepdims=True) + EPS)
        out_ref[0, 0, h * 128:(h + 1) * 128, :] = ((y * g) * (1.0 - lam_init)).astype(BF16)
    a_rows = A_HEADS * 2 * A_DH
    for h in range(B_HEADS):
        ob = acc_b_ref[h] / l_b_ref[h]
        out_ref[0, 0, a_rows + h * B_DH:a_rows + (h + 1) * B_DH, :] = ob.astype(BF16)


def _attention(lam_init, aq, bqp, iq, iwt, akp, avt, bkp, ikk, bvt, lamv, gt):
    bsz, s, a_cols = aq.shape
    nq = s // TQ
    nk = s // TK
    i_cols = iq.shape[2]
    d_mix = a_cols + B_HEADS * B_DH
    in_specs = [
        pl.BlockSpec((1, TQ, a_cols), lambda b, i: (b, i, 0)),
        pl.BlockSpec((1, TQ, bqp.shape[2]), lambda b, i: (b, i, 0)),
        pl.BlockSpec((1, TQ, i_cols), lambda b, i: (b, i, 0)),
        pl.BlockSpec((1, 1, IDX_HEADS, TQ), lambda b, i: (b, i, 0, 0)),
        pl.BlockSpec((1, s, akp.shape[2]), lambda b, i: (b, 0, 0)),
        pl.BlockSpec((1, nk, a_cols, TK), lambda b, i: (b, 0, 0, 0)),
        pl.BlockSpec((1, s, 128), lambda b, i: (b, 0, 0)),
        pl.BlockSpec((1, s, 128), lambda b, i: (b, 0, 0)),
        pl.BlockSpec((1, nk, B_DH, TK), lambda b, i: (b, 0, 0, 0)),
        _const_spec(lamv.shape),
        _const_spec(gt.shape),
    ]
    n_maps = 2 * A_HEADS
    scratch = [
        pltpu.VMEM((nk, TK, TQ), F32),
        pltpu.VMEM((A_HEADS, 2 * TQ, 256), BF16),
        pltpu.VMEM((A_HEADS, IDX_HEADS // A_HEADS * TQ, 128), BF16),
        pltpu.VMEM((B_HEADS * TQ, 128), BF16),
        pltpu.VMEM((IDX_HEADS, TQ), F32),
        pltpu.VMEM((n_maps, 2 * A_DH, TQ), F32),
        pltpu.VMEM((n_maps, 1, TQ), F32),
        pltpu.VMEM((n_maps, 1, TQ), F32),
        pltpu.VMEM((B_HEADS, B_DH, TQ), F32),
        pltpu.VMEM((B_HEADS, 1, TQ), F32),
        pltpu.VMEM((B_HEADS, 1, TQ), F32),
    ]
    return pl.pallas_call(
        functools.partial(_attn_kernel, lam_init),
        grid=(bsz, nq),
        in_specs=in_specs,
        out_specs=pl.BlockSpec((1, 1, d_mix, TQ), lambda b, i: (b, i, 0, 0)),
        out_shape=jax.ShapeDtypeStruct((bsz, nq, d_mix, TQ), BF16),
        scratch_shapes=scratch,
        compiler_params=pltpu.CompilerParams(
            dimension_semantics=("arbitrary", "arbitrary"), vmem_limit_bytes=VMEM_LIMIT_BYTES),
        name="attention",
    )(aq, bqp, iq, iwt, akp, avt, bkp, ikk, bvt, lamv, gt)


def _ffn_kernel(final, mix_ref, x_ref, mod_ref, n2_ref, fg_ref, wo_ref, w1_ref, w2_ref, o_ref):
    wo = wo_ref[...]
    y = jnp.concatenate(
        [lax.dot_general(mix_ref[0, c], wo, _TN, preferred_element_type=F32)
         for c in range(TM_FFN // TQ)], axis=0)
    g1 = mod_ref[0, 2:3, :]
    sh2 = mod_ref[0, 3:4, :]
    sc2 = mod_ref[0, 4:5, :]
    g2 = mod_ref[0, 5:6, :]
    x1 = x_ref[0] + g1 * y
    h2 = ((_rms(x1) * n2_ref[...]) * (1.0 + sc2) + sh2).astype(BF16)
    ff = jnp.zeros_like(x1)
    for c in range(w1_ref.shape[1] // FF_CHUNK):
        u = jnp.dot(h2, w1_ref[:, c * FF_CHUNK:(c + 1) * FF_CHUNK], preferred_element_type=F32)
        u = jnp.square(jnp.maximum(u, 0.0)).astype(BF16)
        ff = ff + jnp.dot(u, w2_ref[c * FF_CHUNK:(c + 1) * FF_CHUNK, :], preferred_element_type=F32)
    x2 = x1 + g2 * ff
    if final:
        x2 = _rms(x2) * fg_ref[...]
    o_ref[0] = x2


def _out_ffn(final, mixt, x, mod, n2g, fg, wo, w1, w2):
    bsz, s, d = x.shape
    d_mix = mixt.shape[2]
    tm = TM_FFN
    cpt = tm // TQ
    return pl.pallas_call(
        functools.partial(_ffn_kernel, final),
        grid=(bsz, s // tm),
        in_specs=[
            pl.BlockSpec((1, cpt, d_mix, TQ), lambda b, t: (b, t, 0, 0)),
            pl.BlockSpec((1, tm, d), lambda b, t: (b, t, 0)),
            pl.BlockSpec((1, 6, d), lambda b, t: (b, 0, 0)),
            _const_spec(n2g.shape),
            _const_spec(fg.shape),
            _const_spec(wo.shape),
            _const_spec(w1.shape),
            _const_spec(w2.shape),
        ],
        out_specs=pl.BlockSpec((1, tm, d), lambda b, t: (b, t, 0)),
        out_shape=jax.ShapeDtypeStruct((bsz, s, d), F32),
        compiler_params=pltpu.CompilerParams(
            dimension_semantics=("arbitrary", "arbitrary"), vmem_limit_bytes=VMEM_LIMIT_BYTES),
        name="out_ffn",
    )(mixt, x, mod, n2g, fg, wo, w1, w2)


def _split_w_in(w):
    d = w.shape[0]
    a = A_HEADS * 2 * A_DH
    sizes = (a, a, a, B_HEADS * B_DH, B_DH, B_DH, IDX_HEADS * IDX_DH, IDX_DH, IDX_HEADS)
    offs = np.cumsum((0,) + sizes)
    aq, ak, av, bq, bk, bv, iq, ik, iw = [w[:, offs[n]:offs[n + 1]] for n in range(len(sizes))]
    bq_pad = jnp.pad(bq.reshape(d, B_HEADS, B_DH), ((0, 0), (0, 0), (0, 128 - B_DH)))
    bq_pad = bq_pad.reshape(d, B_HEADS * 128)
    kk = jnp.concatenate([bk, jnp.zeros((d, 128 - B_DH), w.dtype), ik, ik], axis=1)
    nat = [aq, ak, bq_pad, iq, kk]
    trn = [av.T, bv.T, iw.T]
    return [m.astype(BF16) for m in nat + trn]


def _dsa_slope_columns():
    row = np.zeros((1, B_HEADS * 128), np.float32)
    for h, sl in enumerate(_alibi_slopes(B_HEADS)):
        row[0, h * 128 + B_DH] = sl * POS_SPLIT
        row[0, h * 128 + B_DH + 1] = sl
    return jnp.asarray(row)


def kernel(x, c, norm1_g, norm2_g, w_ada, b_ada, w_in, lam_q1, lam_k1, lam_q2, lam_k2,
           subln_g, w_out, w_ff1, w_ff2, final_g):
    bsz, s, d = x.shape
    depth = w_in.shape[0]
    assert s % TS_PROJ == 0 and s % TM_FFN == 0 and s % TQ == 0 and TQ == TK
    assert min(TOPK_MAX, s // 4) == TOPK_MAX and s // POS_SPLIT <= 256 and CHUNK == 1 << CHUNK_SHIFT
    fg = final_g.reshape(1, d)
    bqc = _dsa_slope_columns()
    for l in range(depth):
        lam_init = 0.8 - 0.6 * math.exp(-0.3 * l)
        mod = _adaln_mod(c, w_ada[l].astype(BF16), b_ada[l].reshape(1, -1)).reshape(bsz, 6, d)
        aq, akp, bqp, iq, bkp, ikk, avt, bvt, iwt = _in_proj(
            x, mod, norm1_g[l].reshape(1, d), _split_w_in(w_in[l]), bqc)
        lamv = jnp.stack([lam_q1[l], lam_k1[l], lam_q2[l], lam_k2[l]])
        gt = jnp.broadcast_to(subln_g[l][:, None], (2 * A_DH, TQ))
        mixt = _attention(lam_init, aq, bqp, iq, iwt, akp, avt, bkp, ikk, bvt, lamv, gt)
        x = _out_ffn(l == depth - 1, mixt, x, mod, norm2_g[l].reshape(1, d), fg,
                     w_out[l].astype(BF16), w_ff1[l].astype(BF16), w_ff2[l].astype(BF16))
    return x
```

```python
import functools
import math

import jax
import jax.numpy as jnp
import numpy as np
from jax import lax
from jax.experimental import pallas as pl
from jax.experimental.pallas import tpu as pltpu

CHUNK = 64
CHUNK_SHIFT = 6
A_HEADS = 4
A_DH = 64
B_HEADS = 8
B_DH = 64
IDX_HEADS = 16
IDX_DH = 64
TOPK_MAX = 256
EPS = 1e-6

TQ = 256
TK = 256
TS_PROJ = 512
TM_FFN = 512
FF_CHUNK = 1024
VMEM_LIMIT_BYTES = 56 * 1024 * 1024
V_PAD = 16
POS_SHIFT = 6
POS_SPLIT = 1 << POS_SHIFT

NEG = -1e30
BIG = 3e38
INT_MIN = -(2 ** 31)
F32_LOWEST = float(np.finfo(np.float32).min)
BF16 = jnp.bfloat16
F32 = jnp.float32
I32 = jnp.int32

_NT = (((1,), (1,)), ((), ()))
_TN = (((0,), (0,)), ((), ()))


def _alibi_slopes(n):
    return [2.0 ** (-8.0 * (i + 1) / n) for i in range(n)]


def _const_spec(shape):
    nd = len(shape)
    return pl.BlockSpec(shape, lambda *_: (0,) * nd, pipeline_mode=pl.Buffered(1))


def _rms(x):
    return x * lax.rsqrt(jnp.mean(x * x, axis=-1, keepdims=True) + EPS)


def _mod_kernel(c_ref, w_ref, b_ref, o_ref):
    c = c_ref[...]
    cond = c * (1.0 / (1.0 + jnp.exp(-c)))
    o_ref[...] = jnp.dot(cond.astype(BF16), w_ref[...], preferred_element_type=F32) + b_ref[...]


def _adaln_mod(c, w_ada, b_ada):
    bsz, d = c.shape
    n = w_ada.shape[1]
    return pl.pallas_call(
        _mod_kernel,
        grid=(n // d,),
        in_specs=[
            pl.BlockSpec((bsz, d), lambda j: (0, 0)),
            pl.BlockSpec((d, d), lambda j: (0, j)),
            pl.BlockSpec((1, d), lambda j: (0, j)),
        ],
        out_specs=pl.BlockSpec((bsz, d), lambda j: (0, j)),
        out_shape=jax.ShapeDtypeStruct((bsz, n), F32),
        compiler_params=pltpu.CompilerParams(dimension_semantics=("arbitrary",)),
        name="adaln_mod",
    )(c, w_ada, b_ada)


def _inproj_kernel(x_ref, mod_ref, g_ref, waq_ref, wak_ref, wbq_ref, wiq_ref, wkk_ref,
                   wav_ref, wbv_ref, wiw_ref, bqc_ref,
                   aq_ref, akp_ref, bqp_ref, iq_ref, bkp_ref, ikk_ref, av_ref, bv_ref, iw_ref):
    x = x_ref[0]
    sh1 = mod_ref[0, 0:1, :]
    sc1 = mod_ref[0, 1:2, :]
    h = (_rms(x) * g_ref[...]) * (1.0 + sc1) + sh1
    hb = h.astype(BF16)

    def nat(w_ref):
        return jnp.dot(hb, w_ref[...], preferred_element_type=F32)

    pos = pl.program_id(1) * TS_PROJ + lax.broadcasted_iota(I32, (TS_PROJ, 128), 0)
    lane = lax.broadcasted_iota(I32, (TS_PROJ, 128), 1)
    pos_hi = (pos >> POS_SHIFT).astype(F32)
    pos_lo = (pos & (POS_SPLIT - 1)).astype(F32)

    def pos_tile(l0):
        return jnp.where(lane == l0, pos_hi, jnp.where(lane == l0 + 1, pos_lo, 0.0))

    aq_ref[0] = (nat(waq_ref) * (A_DH ** -0.5)).astype(BF16)
    ak = nat(wak_ref)
    pos0 = pos_tile(0).astype(BF16)
    for hd in range(A_HEADS):
        akp_ref[0, :, hd * 256:hd * 256 + 128] = ak[:, hd * 128:(hd + 1) * 128].astype(BF16)
        akp_ref[0, :, hd * 256 + 128:(hd + 1) * 256] = pos0
    bqp_ref[0] = (nat(wbq_ref) * (B_DH ** -0.5) + bqc_ref[...]).astype(BF16)
    iq_ref[0] = nat(wiq_ref).astype(BF16)
    kk = nat(wkk_ref)
    bkp_ref[0] = (kk[:, 0:128] + pos_tile(B_DH)).astype(BF16)
    ikk_ref[0] = kk[:, 128:256].astype(BF16)

    def tr(w_ref):
        return lax.dot_general(w_ref[...], hb, _NT, preferred_element_type=F32)

    avt = tr(wav_ref).astype(BF16)
    bvt = tr(wbv_ref).astype(BF16)
    iwt = tr(wiw_ref)
    ones_rows = jnp.where(lax.broadcasted_iota(I32, (V_PAD, TK), 0) == 0, 1.0, 0.0).astype(BF16)
    a_dv = 2 * A_DH
    for c in range(TS_PROJ // TK):
        for hd in range(A_HEADS):
            r0 = hd * (a_dv + V_PAD)
            av_ref[0, c, r0:r0 + a_dv, :] = avt[hd * a_dv:(hd + 1) * a_dv, c * TK:(c + 1) * TK]
            av_ref[0, c, r0 + a_dv:r0 + a_dv + V_PAD, :] = ones_rows
        bv_ref[0, c, 0:B_DH, :] = bvt[:, c * TK:(c + 1) * TK]
        bv_ref[0, c, B_DH:B_DH + V_PAD, :] = ones_rows
        iw_ref[0, c] = iwt[:, c * TK:(c + 1) * TK]


def _in_proj(x, mod, g, ws, bqc):
    bsz, s, d = x.shape
    ts = TS_PROJ
    nk = s // TK
    cpt = ts // TK

    def tok(cols):
        return pl.BlockSpec((1, ts, cols), lambda b, t: (b, t, 0))

    def trs(rows):
        return pl.BlockSpec((1, cpt, rows, TK), lambda b, t: (b, t, 0, 0))

    a_cols = A_HEADS * 2 * A_DH
    i_cols = IDX_HEADS * IDX_DH
    out_shape = [
        jax.ShapeDtypeStruct((bsz, s, a_cols), BF16),
        jax.ShapeDtypeStruct((bsz, s, A_HEADS * 256), BF16),
        jax.ShapeDtypeStruct((bsz, s, B_HEADS * 128), BF16),
        jax.ShapeDtypeStruct((bsz, s, i_cols), BF16),
        jax.ShapeDtypeStruct((bsz, s, 128), BF16),
        jax.ShapeDtypeStruct((bsz, s, 128), BF16),
        jax.ShapeDtypeStruct((bsz, nk, A_HEADS * (2 * A_DH + V_PAD), TK), BF16),
        jax.ShapeDtypeStruct((bsz, nk, B_DH + V_PAD, TK), BF16),
        jax.ShapeDtypeStruct((bsz, nk, IDX_HEADS, TK), F32),
    ]
    out_specs = [tok(a_cols), tok(A_HEADS * 256), tok(B_HEADS * 128), tok(i_cols), tok(128),
                 tok(128), trs(A_HEADS * (2 * A_DH + V_PAD)), trs(B_DH + V_PAD), trs(IDX_HEADS)]
    in_specs = [
        pl.BlockSpec((1, ts, d), lambda b, t: (b, t, 0)),
        pl.BlockSpec((1, 6, d), lambda b, t: (b, 0, 0)),
        _const_spec(g.shape),
    ] + [_const_spec(w.shape) for w in ws] + [_const_spec(bqc.shape)]
    return pl.pallas_call(
        _inproj_kernel,
        grid=(bsz, s // ts),
        in_specs=in_specs,
        out_specs=out_specs,
        out_shape=out_shape,
        compiler_params=pltpu.CompilerParams(
            dimension_semantics=("arbitrary", "arbitrary"), vmem_limit_bytes=VMEM_LIMIT_BYTES),
        name="in_proj",
    )(x, mod, g, *ws, bqc)


def _online_update(s_all, v, acc_ref, m_ref, idx0, n, prep=None):
    ps, alphas = [], []
    for g in range(n):
        s = s_all[:, g * TQ:(g + 1) * TQ]
        if prep is not None:
            s = prep(g, s)
        m_prev = m_ref[idx0 + g]
        m_new = jnp.maximum(m_prev, jnp.max(s, axis=0, keepdims=True))
        alphas.append(jnp.exp(m_prev - m_new))
        ps.append(jnp.exp(s - m_new).astype(BF16))
        m_ref[idx0 + g] = m_new
    pv = jnp.dot(v, jnp.concatenate(ps, axis=1), preferred_element_type=F32)
    for g in range(n):
        acc_ref[idx0 + g] = alphas[g] * acc_ref[idx0 + g] + pv[:, g * TQ:(g + 1) * TQ]


def _key_to_f32(key):
    return pltpu.bitcast(key ^ ((key >> 31) & jnp.int32(0x7FFFFFFF)), F32)


def _f32_to_key(x):
    bits = pltpu.bitcast(x, I32)
    return bits ^ ((bits >> 31) & jnp.int32(0x7FFFFFFF))


def _attn_kernel(lam_init, aq_ref, bqp_ref, iq_ref, iw_ref, akp_ref, av_ref, bkp_ref, ikk_ref,
                 bv_ref, lamv_ref, g_ref, out_ref,
                 score_ref, sa_ref, sb_ref, qa_ref, qi_ref, qb_ref, w_ref,
                 acc_a_ref, m_a_ref, acc_b_ref, m_b_ref):
    i = pl.program_id(1)
    n_off = i * (TQ // TK)
    nkv = n_off + TQ // TK
    slopes_a = _alibi_slopes(A_HEADS)
    slopes_b = _alibi_slopes(B_HEADS)
    n_maps = 2 * A_HEADS

    lane = lax.broadcasted_iota(I32, (TQ, 128), 1)
    keep_lo = jnp.where(lane < 64, 1.0, 0.0).astype(BF16)
    keep_hi = jnp.where(lane < 64, 0.0, 1.0).astype(BF16)
    idx_grp = IDX_HEADS // A_HEADS
    for h in range(A_HEADS):
        qh = aq_ref[0, :, h * 128:(h + 1) * 128]
        slope_cols = jnp.where(
            lane == 0, slopes_a[h] * POS_SPLIT, jnp.where(lane == 1, slopes_a[h], 0.0)).astype(BF16)
        qa_ref[h, 0:TQ, 0:128] = qh * keep_lo
        qa_ref[h, TQ:2 * TQ, 0:128] = qh * keep_hi
        qa_ref[h, 0:TQ, 128:256] = slope_cols
        qa_ref[h, TQ:2 * TQ, 128:256] = slope_cols
    for t in range(IDX_HEADS // 2):
        qt = iq_ref[0, :, t * 128:(t + 1) * 128]
        for half, keep in enumerate((keep_lo, keep_hi)):
            g, k = divmod(2 * t + half, idx_grp)
            qi_ref[g, k * TQ:(k + 1) * TQ, :] = qt * keep
    for h in range(B_HEADS):
        qb_ref[h * TQ:(h + 1) * TQ, :] = bqp_ref[0, :, h * 128:(h + 1) * 128]
    for c in range(TQ // TK):
        w_ref[:, c * TK:(c + 1) * TK] = iw_ref[0, c] * ((IDX_HEADS * IDX_DH) ** -0.5)

    acc_a_ref[...] = jnp.zeros_like(acc_a_ref)
    acc_b_ref[...] = jnp.zeros_like(acc_b_ref)
    m_a_ref[...] = jnp.full_like(m_a_ref, NEG)
    m_b_ref[...] = jnp.full_like(m_b_ref, NEG)

    def diag_geometry(j):
        row = j * TK + lax.broadcasted_iota(I32, (TK, TQ), 0)
        col = i * TQ + lax.broadcasted_iota(I32, (TK, TQ), 1)
        allowed = row < ((col >> CHUNK_SHIFT) + 1) * CHUNK
        corr = 2.0 * jnp.minimum((col - row).astype(F32), 0.0)
        return allowed, jnp.where(allowed, BIG, NEG), corr

    def index_block(j, diag):
        k0 = pl.multiple_of(j * TK, TK)
        kk = ikk_ref[0, pl.ds(k0, TK), :]
        score = jnp.zeros((TK, TQ), F32)
        for g in range(IDX_HEADS // idx_grp):
            xg = lax.dot_general(kk, qi_ref[g], _NT, preferred_element_type=F32)
            for k in range(idx_grp):
                hh = g * idx_grp + k
                score = score + w_ref[hh:hh + 1, :] * jnp.maximum(xg[:, k * TQ:(k + 1) * TQ], 0.0)
        if diag:
            allowed, _, _ = diag_geometry(j)
            score = jnp.where(allowed, score, -jnp.inf)
        score_ref[j] = score

    def index_body(diag, j, carry):
        index_block(j, diag)
        return carry

    lax.fori_loop(0, n_off, functools.partial(index_body, False), 0)
    lax.fori_loop(n_off, nkv, functools.partial(index_body, True), 0)

    qpos = i * TQ + lax.broadcasted_iota(I32, (1, TQ), 1)
    n_adm = ((qpos >> CHUNK_SHIFT) + 1) * CHUNK
    searching = n_adm > TOPK_MAX

    def count_where(pred_fn):
        def body(c, cnt):
            k0 = pl.multiple_of(c * TK, TK)
            ind = jnp.where(pred_fn(score_ref[c], c), jnp.int32(1), jnp.int32(0))
            return cnt + jnp.sum(ind.reshape(TK // 8, 8, TQ), axis=0)
        cnt8 = lax.fori_loop(0, nkv, body, jnp.zeros((8, TQ), I32))
        return jnp.sum(cnt8, axis=0, keepdims=True)

    n_acc = 4

    def count_ge(cand):
        def body(c, accs):
            k0 = pl.multiple_of(c * TK, TK)
            accs = list(accs)
            for r in range(TK // 8):
                a = accs[r % n_acc]
                accs[r % n_acc] = jnp.where(score_ref[c, r * 8:(r + 1) * 8, :] >= cand, a + 1, a)
            return tuple(accs)
        accs = lax.fori_loop(0, nkv, body,
                             tuple(jnp.zeros((8, TQ), I32) for _ in range(n_acc)))
        return jnp.sum(functools.reduce(lambda a, b: a + b, accs), axis=0, keepdims=True)

    def search_pass(t, carry):
        pref, cnt_at = carry
        trial = pref | jnp.left_shift(jnp.int32(1), 31 - t)
        cnt = count_ge(_key_to_f32(trial ^ jnp.int32(INT_MIN)))
        take = cnt >= TOPK_MAX
        return jnp.where(take, trial, pref), jnp.where(take, cnt, cnt_at)

    pref, cnt_at = lax.fori_loop(
        0, 32, search_pass, (jnp.zeros((1, TQ), I32), jnp.full((1, TQ), TOPK_MAX, I32)))
    tau = jnp.where(searching, _key_to_f32(pref ^ jnp.int32(INT_MIN)), F32_LOWEST)

    @pl.when(jnp.max(cnt_at) > TOPK_MAX)
    def _():
        def pos_of(c):
            return lax.broadcasted_iota(I32, (TK, TQ), 0) + c * TK

        need = TOPK_MAX - count_where(lambda st, c: st > tau)
        n_bits = int(score_ref.shape[0] * TK).bit_length()

        def idx_body(t, bound):
            trial = bound | jnp.left_shift(jnp.int32(1), n_bits - 1 - t)
            cnt = count_where(
                lambda st, c: jnp.where(st == tau, pos_of(c), jnp.int32(1 << 30)) < trial)
            return jnp.where(cnt <= need, trial, bound)

        bound = lax.fori_loop(0, n_bits, idx_body, jnp.zeros((1, TQ), I32))
        below = _key_to_f32(_f32_to_key(tau) - 1)

        def demote(c, carry):
            k0 = pl.multiple_of(c * TK, TK)
            st = score_ref[c]
            drop = jnp.where(st == tau, pos_of(c), jnp.int32(-1)) >= bound
            score_ref[c] = jnp.where(drop, below, st)
            return carry

        lax.fori_loop(0, nkv, demote, 0)

    n_stage = A_HEADS
    dsa_grp = B_HEADS // n_stage
    a_dv = 2 * A_DH
    a_rows_v = a_dv + V_PAD

    def qk_diff(h, kstart):
        kh = akp_ref[0, pl.ds(kstart, TK), h * 256:(h + 1) * 256]
        return lax.dot_general(kh, qa_ref[h], _NT, preferred_element_type=F32)

    def qk_dsa(h, kstart):
        kb = bkp_ref[0, pl.ds(kstart, TK), :]
        qh = qb_ref[h * dsa_grp * TQ:(h + 1) * dsa_grp * TQ, :]
        return lax.dot_general(kb, qh, _NT, preferred_element_type=F32)

    def attend_block(j, diag):
        k0 = pl.multiple_of(j * TK, TK)
        kn0 = pl.multiple_of(jnp.minimum(j + 1, nkv - 1) * TK, TK)
        slot = lax.rem(j, 2)
        cap_sel = jnp.where(score_ref[j] >= tau, BIG, NEG)
        if diag:
            _, cap_chunk, corr = diag_geometry(j)
        vb = bv_ref[0, j]
        sd_next = sa_ref[slot]
        sb_next = sb_ref[slot]
        for h in range(n_stage):
            sd, sb = sd_next, sb_next
            if h + 1 < n_stage:
                sd_next = qk_diff(h + 1, k0)
                sb_next = qk_dsa(h + 1, k0)
            else:
                sa_ref[1 - slot] = qk_diff(0, kn0)
                sb_ref[1 - slot] = qk_dsa(0, kn0)

            prep_d = None
            if diag:
                prep_d = functools.partial(
                    lambda sl, g, s: jnp.minimum(s, cap_chunk) + sl * corr, slopes_a[h])
            vh = av_ref[0, j, h * a_rows_v:(h + 1) * a_rows_v, :]
            _online_update(sd, vh, acc_a_ref, m_a_ref, 2 * h, 2, prep_d)

            def prep_b(h0, g, s):
                s = jnp.minimum(s, cap_sel)
                return s + slopes_b[h0 + g] * corr if diag else s

            _online_update(sb, vb, acc_b_ref, m_b_ref, dsa_grp * h, dsa_grp,
                           functools.partial(prep_b, dsa_grp * h))

    def attend_body(diag, j, carry):
        attend_block(j, diag)
        return carry

    sa_ref[0] = qk_diff(0, 0)
    sb_ref[0] = qk_dsa(0, 0)
    lax.fori_loop(0, n_off, functools.partial(attend_body, False), 0)
    lax.fori_loop(n_off, nkv, functools.partial(attend_body, True), 0)

    lv = lamv_ref[...]
    lam = (jnp.exp(jnp.sum(lv[0:1] * lv[1:2], axis=1, keepdims=True))
           - jnp.exp(jnp.sum(lv[2:3] * lv[3:4], axis=1, keepdims=True)) + lam_init)
    g = g_ref[...]
    for h in range(A_HEADS):
        o1 = acc_a_ref[2 * h, 0:a_dv, :] / acc_a_ref[2 * h, a_dv:a_dv + 1, :]
        o2 = acc_a_ref[2 * h + 1, 0:a_dv, :] / acc_a_ref[2 * h + 1, a_dv:a_dv + 1, :]
        o = o1 - lam * o2
        y = o * lax.rsqrt(jnp.mean(o * o, axis=0, keepdims=True) + EPS)
        out_ref[0, 0, h * 128:(h + 1) * 128, :] = ((y * g) * (1.0 - lam_init)).astype(BF16)
    a_rows = A_HEADS * 2 * A_DH
    for h in range(B_HEADS):
        ob = acc_b_ref[h, 0:B_DH, :] / acc_b_ref[h, B_DH:B_DH + 1, :]
        out_ref[0, 0, a_rows + h * B_DH:a_rows + (h + 1) * B_DH, :] = ob.astype(BF16)


def _attention(lam_init, aq, bqp, iq, iwt, akp, avt, bkp, ikk, bvt, lamv, gt):
    bsz, s, a_cols = aq.shape
    nq = s // TQ
    nk = s // TK
    i_cols = iq.shape[2]
    d_mix = a_cols + B_HEADS * B_DH

    def per_batch(shape):
        return pl.BlockSpec(shape, lambda b, i: (b,) + (0,) * (len(shape) - 1))

    in_specs = [
        pl.BlockSpec((1, TQ, a_cols), lambda b, i: (b, i, 0)),
        pl.BlockSpec((1, TQ, bqp.shape[2]), lambda b, i: (b, i, 0)),
        pl.BlockSpec((1, TQ, i_cols), lambda b, i: (b, i, 0)),
        pl.BlockSpec((1, TQ // TK, IDX_HEADS, TK), lambda b, i: (b, i, 0, 0)),
        per_batch((1, s, akp.shape[2])),
        per_batch((1, nk, avt.shape[2], TK)),
        per_batch((1, s, 128)),
        per_batch((1, s, 128)),
        per_batch((1, nk, bvt.shape[2], TK)),
        _const_spec(lamv.shape),
        _const_spec(gt.shape),
    ]
    n_maps = 2 * A_HEADS
    scratch = [
        pltpu.VMEM((nk, TK, TQ), F32),
        pltpu.VMEM((2, TK, 2 * TQ), F32),
        pltpu.VMEM((2, TK, B_HEADS // A_HEADS * TQ), F32),
        pltpu.VMEM((A_HEADS, 2 * TQ, 256), BF16),
        pltpu.VMEM((A_HEADS, IDX_HEADS // A_HEADS * TQ, 128), BF16),
        pltpu.VMEM((B_HEADS * TQ, 128), BF16),
        pltpu.VMEM((IDX_HEADS, TQ), F32),
        pltpu.VMEM((n_maps, 2 * A_DH + V_PAD, TQ), F32),
        pltpu.VMEM((n_maps, 1, TQ), F32),
        pltpu.VMEM((B_HEADS, B_DH + V_PAD, TQ), F32),
        pltpu.VMEM((B_HEADS, 1, TQ), F32),
    ]
    return pl.pallas_call(
        functools.partial(_attn_kernel, lam_init),
        grid=(bsz, nq),
        in_specs=in_specs,
        out_specs=pl.BlockSpec((1, 1, d_mix, TQ), lambda b, i: (b, i, 0, 0)),
        out_shape=jax.ShapeDtypeStruct((bsz, nq, d_mix, TQ), BF16),
        scratch_shapes=scratch,
        compiler_params=pltpu.CompilerParams(
            dimension_semantics=("arbitrary", "arbitrary"), vmem_limit_bytes=VMEM_LIMIT_BYTES),
        name="attention",
    )(aq, bqp, iq, iwt, akp, avt, bkp, ikk, bvt, lamv, gt)


def _ffn_kernel(final, mix_ref, x_ref, mod_ref, n2_ref, fg_ref, wo_ref, w1_ref, w2_ref, o_ref):
    wo = wo_ref[...]
    y = jnp.concatenate(
        [lax.dot_general(mix_ref[0, c], wo, _TN, preferred_element_type=F32)
         for c in range(TM_FFN // TQ)], axis=0)
    g1 = mod_ref[0, 2:3, :]
    sh2 = mod_ref[0, 3:4, :]
    sc2 = mod_ref[0, 4:5, :]
    g2 = mod_ref[0, 5:6, :]
    x1 = x_ref[0] + g1 * y
    h2 = ((_rms(x1) * n2_ref[...]) * (1.0 + sc2) + sh2).astype(BF16)
    ff = jnp.zeros_like(x1)
    for c in range(w1_ref.shape[1] // FF_CHUNK):
        u = jnp.dot(h2, w1_ref[:, c * FF_CHUNK:(c + 1) * FF_CHUNK], preferred_element_type=F32)
        u = jnp.square(jnp.maximum(u, 0.0)).astype(BF16)
        ff = ff + jnp.dot(u, w2_ref[c * FF_CHUNK:(c + 1) * FF_CHUNK, :], preferred_element_type=F32)
    x2 = x1 + g2 * ff
    if final:
        x2 = _rms(x2) * fg_ref[...]
    o_ref[0] = x2


def _out_ffn(final, mixt, x, mod, n2g, fg, wo, w1, w2):
    bsz, s, d = x.shape
    d_mix = mixt.shape[2]
    tm = TM_FFN
    cpt = tm // TQ
    return pl.pallas_call(
        functools.partial(_ffn_kernel, final),
        grid=(bsz, s // tm),
        in_specs=[
            pl.BlockSpec((1, cpt, d_mix, TQ), lambda b, t: (b, t, 0, 0)),
            pl.BlockSpec((1, tm, d), lambda b, t: (b, t, 0)),
            pl.BlockSpec((1, 6, d), lambda b, t: (b, 0, 0)),
            _const_spec(n2g.shape),
            _const_spec(fg.shape),
            _const_spec(wo.shape),
            _const_spec(w1.shape),
            _const_spec(w2.shape),
        ],
        out_specs=pl.BlockSpec((1, tm, d), lambda b, t: (b, t, 0)),
        out_shape=jax.ShapeDtypeStruct((bsz, s, d), F32),
        compiler_params=pltpu.CompilerParams(
            dimension_semantics=("arbitrary", "arbitrary"), vmem_limit_bytes=VMEM_LIMIT_BYTES),
        name="out_ffn",
    )(mixt, x, mod, n2g, fg, wo, w1, w2)


def _split_w_in(w):
    d = w.shape[0]
    a = A_HEADS * 2 * A_DH
    sizes = (a, a, a, B_HEADS * B_DH, B_DH, B_DH, IDX_HEADS * IDX_DH, IDX_DH, IDX_HEADS)
    offs = np.cumsum((0,) + sizes)
    aq, ak, av, bq, bk, bv, iq, ik, iw = [w[:, offs[n]:offs[n + 1]] for n in range(len(sizes))]
    bq_pad = jnp.pad(bq.reshape(d, B_HEADS, B_DH), ((0, 0), (0, 0), (0, 128 - B_DH)))
    bq_pad = bq_pad.reshape(d, B_HEADS * 128)
    kk = jnp.concatenate([bk, jnp.zeros((d, 128 - B_DH), w.dtype), ik, ik], axis=1)
    nat = [aq, ak, bq_pad, iq, kk]
    trn = [av.T, bv.T, iw.T]
    return [m.astype(BF16) for m in nat + trn]


def _dsa_slope_columns():
    row = np.zeros((1, B_HEADS * 128), np.float32)
    for h, sl in enumerate(_alibi_slopes(B_HEADS)):
        row[0, h * 128 + B_DH] = sl * POS_SPLIT
        row[0, h * 128 + B_DH + 1] = sl
    return jnp.asarray(row)


def kernel(x, c, norm1_g, norm2_g, w_ada, b_ada, w_in, lam_q1, lam_k1, lam_q2, lam_k2,
           subln_g, w_out, w_ff1, w_ff2, final_g):
    bsz, s, d = x.shape
    depth = w_in.shape[0]
    assert s % TS_PROJ == 0 and s % TM_FFN == 0 and s % TQ == 0 and TQ % TK == 0 and TM_FFN % TQ == 0
    assert min(TOPK_MAX, s // 4) == TOPK_MAX and s // POS_SPLIT <= 256 and CHUNK == 1 << CHUNK_SHIFT
    fg = final_g.reshape(1, d)
    bqc = _dsa_slope_columns()
    for l in range(depth):
        lam_init = 0.8 - 0.6 * math.exp(-0.3 * l)
        mod = _adaln_mod(c, w_ada[l].astype(BF16), b_ada[l].reshape(1, -1)).reshape(bsz, 6, d)
        aq, akp, bqp, iq, bkp, ikk, avt, bvt, iwt = _in_proj(
            x, mod, norm1_g[l].reshape(1, d), _split_w_in(w_in[l]), bqc)
        lamv = jnp.stack([lam_q1[l], lam_k1[l], lam_q2[l], lam_k2[l]])
        gt = jnp.broadcast_to(subln_g[l][:, None], (2 * A_DH, TQ))
        mixt = _attention(lam_init, aq, bqp, iq, iwt, akp, avt, bkp, ikk, bvt, lamv, gt)
        x = _out_ffn(l == depth - 1, mixt, x, mod, norm2_g[l].reshape(1, d), fg,
                     w_out[l].astype(BF16), w_ff1[l].astype(BF16), w_ff2[l].astype(BF16))
    return x
```

```python
import functools
import math

import jax
import jax.numpy as jnp
import numpy as np
from jax import lax
from jax.experimental import pallas as pl
from jax.experimental.pallas import tpu as pltpu

CHUNK = 64
CHUNK_SHIFT = 6
A_HEADS = 4
A_DH = 64
B_HEADS = 8
B_DH = 64
IDX_HEADS = 16
IDX_DH = 64
TOPK_MAX = 256
EPS = 1e-6

TQ = 256
TK = 256
TS_PROJ = 512
TM_FFN = 512
FF_CHUNK = 1024
VMEM_LIMIT_BYTES = 56 * 1024 * 1024
VALUE_PASSES = 24
V_PAD = 16
POS_SHIFT = 6
POS_SPLIT = 1 << POS_SHIFT

NEG = -1e30
BIG = 3e38
INT_MIN = -(2 ** 31)
F32_LOWEST = float(np.finfo(np.float32).min)
BF16 = jnp.bfloat16
F32 = jnp.float32
I32 = jnp.int32

_NT = (((1,), (1,)), ((), ()))
_TN = (((0,), (0,)), ((), ()))


def _alibi_slopes(n):
    return [2.0 ** (-8.0 * (i + 1) / n) for i in range(n)]


def _const_spec(shape):
    nd = len(shape)
    return pl.BlockSpec(shape, lambda *_: (0,) * nd, pipeline_mode=pl.Buffered(1))


def _rms(x):
    return x * lax.rsqrt(jnp.mean(x * x, axis=-1, keepdims=True) + EPS)


def _mod_kernel(c_ref, w_ref, b_ref, o_ref):
    c = c_ref[...]
    cond = c * (1.0 / (1.0 + jnp.exp(-c)))
    o_ref[...] = jnp.dot(cond.astype(BF16), w_ref[...], preferred_element_type=F32) + b_ref[...]


def _adaln_mod(c, w_ada, b_ada):
    bsz, d = c.shape
    n = w_ada.shape[1]
    return pl.pallas_call(
        _mod_kernel,
        grid=(n // d,),
        in_specs=[
            pl.BlockSpec((bsz, d), lambda j: (0, 0)),
            pl.BlockSpec((d, d), lambda j: (0, j)),
            pl.BlockSpec((1, d), lambda j: (0, j)),
        ],
        out_specs=pl.BlockSpec((bsz, d), lambda j: (0, j)),
        out_shape=jax.ShapeDtypeStruct((bsz, n), F32),
        compiler_params=pltpu.CompilerParams(dimension_semantics=("arbitrary",)),
        name="adaln_mod",
    )(c, w_ada, b_ada)


def _inproj_kernel(x_ref, mod_ref, g_ref, waq_ref, wak_ref, wbq_ref, wiq_ref, wkk_ref,
                   wav_ref, wbv_ref, wiw_ref, bqc_ref,
                   aq_ref, akp_ref, bqp_ref, iq_ref, bkp_ref, ikk_ref, av_ref, bv_ref, iw_ref):
    x = x_ref[0]
    sh1 = mod_ref[0, 0:1, :]
    sc1 = mod_ref[0, 1:2, :]
    h = (_rms(x) * g_ref[...]) * (1.0 + sc1) + sh1
    hb = h.astype(BF16)

    def nat(w_ref):
        return jnp.dot(hb, w_ref[...], preferred_element_type=F32)

    pos = pl.program_id(1) * TS_PROJ + lax.broadcasted_iota(I32, (TS_PROJ, 128), 0)
    lane = lax.broadcasted_iota(I32, (TS_PROJ, 128), 1)
    pos_hi = (pos >> POS_SHIFT).astype(F32)
    pos_lo = (pos & (POS_SPLIT - 1)).astype(F32)

    def pos_tile(l0):
        return jnp.where(lane == l0, pos_hi, jnp.where(lane == l0 + 1, pos_lo, 0.0))

    aq_ref[0] = (nat(waq_ref) * (A_DH ** -0.5)).astype(BF16)
    ak = nat(wak_ref)
    pos0 = pos_tile(0).astype(BF16)
    for hd in range(A_HEADS):
        akp_ref[0, :, hd * 256:hd * 256 + 128] = ak[:, hd * 128:(hd + 1) * 128].astype(BF16)
        akp_ref[0, :, hd * 256 + 128:(hd + 1) * 256] = pos0
    bqp_ref[0] = (nat(wbq_ref) * (B_DH ** -0.5) + bqc_ref[...]).astype(BF16)
    iq_ref[0] = nat(wiq_ref).astype(BF16)
    kk = nat(wkk_ref)
    bkp_ref[0] = (kk[:, 0:128] + pos_tile(B_DH)).astype(BF16)
    ikk_ref[0] = kk[:, 128:256].astype(BF16)

    def tr(w_ref):
        return lax.dot_general(w_ref[...], hb, _NT, preferred_element_type=F32)

    avt = tr(wav_ref).astype(BF16)
    bvt = tr(wbv_ref).astype(BF16)
    iwt = tr(wiw_ref)
    ones_rows = jnp.where(lax.broadcasted_iota(I32, (V_PAD, TK), 0) == 0, 1.0, 0.0).astype(BF16)
    a_dv = 2 * A_DH
    for c in range(TS_PROJ // TK):
        for hd in range(A_HEADS):
            r0 = hd * (a_dv + V_PAD)
            av_ref[0, c, r0:r0 + a_dv, :] = avt[hd * a_dv:(hd + 1) * a_dv, c * TK:(c + 1) * TK]
            av_ref[0, c, r0 + a_dv:r0 + a_dv + V_PAD, :] = ones_rows
        bv_ref[0, c, 0:B_DH, :] = bvt[:, c * TK:(c + 1) * TK]
        bv_ref[0, c, B_DH:B_DH + V_PAD, :] = ones_rows
        iw_ref[0, c] = iwt[:, c * TK:(c + 1) * TK]


def _in_proj(x, mod, g, ws, bqc):
    bsz, s, d = x.shape
    ts = TS_PROJ
    nk = s // TK
    cpt = ts // TK

    def tok(cols):
        return pl.BlockSpec((1, ts, cols), lambda b, t: (b, t, 0))

    def trs(rows):
        return pl.BlockSpec((1, cpt, rows, TK), lambda b, t: (b, t, 0, 0))

    a_cols = A_HEADS * 2 * A_DH
    i_cols = IDX_HEADS * IDX_DH
    out_shape = [
        jax.ShapeDtypeStruct((bsz, s, a_cols), BF16),
        jax.ShapeDtypeStruct((bsz, s, A_HEADS * 256), BF16),
        jax.ShapeDtypeStruct((bsz, s, B_HEADS * 128), BF16),
        jax.ShapeDtypeStruct((bsz, s, i_cols), BF16),
        jax.ShapeDtypeStruct((bsz, s, 128), BF16),
        jax.ShapeDtypeStruct((bsz, s, 128), BF16),
        jax.ShapeDtypeStruct((bsz, nk, A_HEADS * (2 * A_DH + V_PAD), TK), BF16),
        jax.ShapeDtypeStruct((bsz, nk, B_DH + V_PAD, TK), BF16),
        jax.ShapeDtypeStruct((bsz, nk, IDX_HEADS, TK), F32),
    ]
    out_specs = [tok(a_cols), tok(A_HEADS * 256), tok(B_HEADS * 128), tok(i_cols), tok(128),
                 tok(128), trs(A_HEADS * (2 * A_DH + V_PAD)), trs(B_DH + V_PAD), trs(IDX_HEADS)]
    in_specs = [
        pl.BlockSpec((1, ts, d), lambda b, t: (b, t, 0)),
        pl.BlockSpec((1, 6, d), lambda b, t: (b, 0, 0)),
        _const_spec(g.shape),
    ] + [_const_spec(w.shape) for w in ws] + [_const_spec(bqc.shape)]
    return pl.pallas_call(
        _inproj_kernel,
        grid=(bsz, s // ts),
        in_specs=in_specs,
        out_specs=out_specs,
        out_shape=out_shape,
        compiler_params=pltpu.CompilerParams(
            dimension_semantics=("arbitrary", "arbitrary"), vmem_limit_bytes=VMEM_LIMIT_BYTES),
        name="in_proj",
    )(x, mod, g, *ws, bqc)


def _online_update(s_all, v, acc_ref, m_ref, idx0, n, prep=None):
    ps, alphas = [], []
    for g in range(n):
        s = s_all[:, g * TQ:(g + 1) * TQ]
        if prep is not None:
            s = prep(g, s)
        m_prev = m_ref[idx0 + g]
        m_new = jnp.maximum(m_prev, jnp.max(s, axis=0, keepdims=True))
        alphas.append(jnp.exp(m_prev - m_new))
        ps.append(jnp.exp(s - m_new).astype(BF16))
        m_ref[idx0 + g] = m_new
    pv = jnp.dot(v, jnp.concatenate(ps, axis=1), preferred_element_type=F32)
    for g in range(n):
        acc_ref[idx0 + g] = alphas[g] * acc_ref[idx0 + g] + pv[:, g * TQ:(g + 1) * TQ]


def _key_to_f32(key):
    return pltpu.bitcast(key ^ ((key >> 31) & jnp.int32(0x7FFFFFFF)), F32)


def _f32_to_key(x):
    bits = pltpu.bitcast(x, I32)
    return bits ^ ((bits >> 31) & jnp.int32(0x7FFFFFFF))


def _attn_kernel(lam_init, aq_ref, bqp_ref, iq_ref, iw_ref, akp_ref, av_ref, bkp_ref, ikk_ref,
                 bv_ref, lamv_ref, g_ref, out_ref,
                 score_ref, rng_ref, sa_ref, sb_ref, qa_ref, qi_ref, qb_ref, w_ref,
                 acc_a_ref, m_a_ref, acc_b_ref, m_b_ref):
    i = pl.program_id(1)
    n_off = i * (TQ // TK)
    nkv = n_off + TQ // TK
    slopes_a = _alibi_slopes(A_HEADS)
    slopes_b = _alibi_slopes(B_HEADS)
    n_maps = 2 * A_HEADS

    lane = lax.broadcasted_iota(I32, (TQ, 128), 1)
    keep_lo = jnp.where(lane < 64, 1.0, 0.0).astype(BF16)
    keep_hi = jnp.where(lane < 64, 0.0, 1.0).astype(BF16)
    idx_grp = IDX_HEADS // A_HEADS
    for h in range(A_HEADS):
        qh = aq_ref[0, :, h * 128:(h + 1) * 128]
        slope_cols = jnp.where(
            lane == 0, slopes_a[h] * POS_SPLIT, jnp.where(lane == 1, slopes_a[h], 0.0)).astype(BF16)
        qa_ref[h, 0:TQ, 0:128] = qh * keep_lo
        qa_ref[h, TQ:2 * TQ, 0:128] = qh * keep_hi
        qa_ref[h, 0:TQ, 128:256] = slope_cols
        qa_ref[h, TQ:2 * TQ, 128:256] = slope_cols
    for t in range(IDX_HEADS // 2):
        qt = iq_ref[0, :, t * 128:(t + 1) * 128]
        for half, keep in enumerate((keep_lo, keep_hi)):
            g, k = divmod(2 * t + half, idx_grp)
            qi_ref[g, k * TQ:(k + 1) * TQ, :] = qt * keep
    for h in range(B_HEADS):
        qb_ref[h * TQ:(h + 1) * TQ, :] = bqp_ref[0, :, h * 128:(h + 1) * 128]
    for c in range(TQ // TK):
        w_ref[:, c * TK:(c + 1) * TK] = iw_ref[0, c] * ((IDX_HEADS * IDX_DH) ** -0.5)

    acc_a_ref[...] = jnp.zeros_like(acc_a_ref)
    acc_b_ref[...] = jnp.zeros_like(acc_b_ref)
    m_a_ref[...] = jnp.full_like(m_a_ref, NEG)
    m_b_ref[...] = jnp.full_like(m_b_ref, NEG)

    def diag_geometry(j):
        row = j * TK + lax.broadcasted_iota(I32, (TK, TQ), 0)
        col = i * TQ + lax.broadcasted_iota(I32, (TK, TQ), 1)
        allowed = row < ((col >> CHUNK_SHIFT) + 1) * CHUNK
        corr = 2.0 * jnp.minimum((col - row).astype(F32), 0.0)
        return allowed, jnp.where(allowed, BIG, NEG), corr

    def index_block(j, diag):
        k0 = pl.multiple_of(j * TK, TK)
        kk = ikk_ref[0, pl.ds(k0, TK), :]
        score = jnp.zeros((TK, TQ), F32)
        for g in range(IDX_HEADS // idx_grp):
            xg = lax.dot_general(kk, qi_ref[g], _NT, preferred_element_type=F32)
            for k in range(idx_grp):
                hh = g * idx_grp + k
                score = score + w_ref[hh:hh + 1, :] * jnp.maximum(xg[:, k * TQ:(k + 1) * TQ], 0.0)
        lo_src = score
        if diag:
            allowed, _, _ = diag_geometry(j)
            lo_src = jnp.where(allowed, score, jnp.inf)
            score = jnp.where(allowed, score, -jnp.inf)
        score_ref[j] = score
        rng_ref[0] = jnp.minimum(rng_ref[0], jnp.min(lo_src.reshape(TK // 8, 8, TQ), axis=0))
        rng_ref[1] = jnp.maximum(rng_ref[1], jnp.max(score.reshape(TK // 8, 8, TQ), axis=0))

    def index_body(diag, j, carry):
        index_block(j, diag)
        return carry

    rng_ref[0] = jnp.full((8, TQ), jnp.inf, F32)
    rng_ref[1] = jnp.full((8, TQ), -jnp.inf, F32)
    lax.fori_loop(0, n_off, functools.partial(index_body, False), 0)
    lax.fori_loop(n_off, nkv, functools.partial(index_body, True), 0)

    qpos = i * TQ + lax.broadcasted_iota(I32, (1, TQ), 1)
    n_adm = ((qpos >> CHUNK_SHIFT) + 1) * CHUNK
    searching = n_adm > TOPK_MAX

    def count_where(pred_fn):
        def body(c, cnt):
            k0 = pl.multiple_of(c * TK, TK)
            ind = jnp.where(pred_fn(score_ref[c], c), jnp.int32(1), jnp.int32(0))
            return cnt + jnp.sum(ind.reshape(TK // 8, 8, TQ), axis=0)
        cnt8 = lax.fori_loop(0, nkv, body, jnp.zeros((8, TQ), I32))
        return jnp.sum(cnt8, axis=0, keepdims=True)

    n_acc = 4

    def count_ge(cand):
        def body(c, accs):
            k0 = pl.multiple_of(c * TK, TK)
            accs = list(accs)
            for r in range(TK // 8):
                a = accs[r % n_acc]
                accs[r % n_acc] = jnp.where(score_ref[c, r * 8:(r + 1) * 8, :] >= cand, a + 1, a)
            return tuple(accs)
        accs = lax.fori_loop(0, nkv, body,
                             tuple(jnp.zeros((8, TQ), I32) for _ in range(n_acc)))
        return jnp.sum(functools.reduce(lambda a, b: a + b, accs), axis=0, keepdims=True)

    def value_pass(_, carry):
        lo, hi, cnt_lo = carry
        mid = 0.5 * lo + 0.5 * hi
        cnt = count_ge(mid)
        inside = jnp.logical_and(mid > lo, mid < hi)
        up = jnp.logical_and(inside, cnt >= TOPK_MAX)
        dn = jnp.logical_and(inside, cnt < TOPK_MAX)
        return jnp.where(up, mid, lo), jnp.where(dn, mid, hi), jnp.where(up, cnt, cnt_lo)

    lo0 = jnp.min(rng_ref[0], axis=0, keepdims=True)
    hi0 = _key_to_f32(_f32_to_key(jnp.max(rng_ref[1], axis=0, keepdims=True)) + 1)
    lo, hi, cnt_at = lax.fori_loop(0, VALUE_PASSES, value_pass, (lo0, hi0, n_adm))

    def unresolved(lo, hi, cnt_lo):
        gap = _f32_to_key(hi) - _f32_to_key(lo)
        open_ = jnp.logical_and(searching, jnp.logical_and(cnt_lo != TOPK_MAX, gap != 1))
        return jnp.where(open_, jnp.int32(1), jnp.int32(0))

    def key_cond(carry):
        return jnp.max(unresolved(*carry)) > 0

    def key_pass(carry):
        lo, hi, cnt_lo = carry
        klo = _f32_to_key(lo)
        kmid = klo + lax.shift_right_logical(_f32_to_key(hi) - klo, jnp.int32(1))
        mid = _key_to_f32(kmid)
        cnt = count_ge(mid)
        live = unresolved(lo, hi, cnt_lo) > 0
        up = jnp.logical_and(live, cnt >= TOPK_MAX)
        dn = jnp.logical_and(live, cnt < TOPK_MAX)
        return jnp.where(up, mid, lo), jnp.where(dn, mid, hi), jnp.where(up, cnt, cnt_lo)

    lo, hi, cnt_at = lax.while_loop(key_cond, key_pass, (lo, hi, cnt_at))
    tau = jnp.where(searching, lo, F32_LOWEST)
    cnt_at = jnp.where(searching, cnt_at, TOPK_MAX)

    @pl.when(jnp.max(cnt_at) > TOPK_MAX)
    def _():
        def pos_of(c):
            return lax.broadcasted_iota(I32, (TK, TQ), 0) + c * TK

        need = TOPK_MAX - count_where(lambda st, c: st > tau)
        n_bits = int(score_ref.shape[0] * TK).bit_length()

        def idx_body(t, bound):
            trial = bound | jnp.left_shift(jnp.int32(1), n_bits - 1 - t)
            cnt = count_where(
                lambda st, c: jnp.where(st == tau, pos_of(c), jnp.int32(1 << 30)) < trial)
            return jnp.where(cnt <= need, trial, bound)

        bound = lax.fori_loop(0, n_bits, idx_body, jnp.zeros((1, TQ), I32))
        below = _key_to_f32(_f32_to_key(tau) - 1)

        def demote(c, carry):
            k0 = pl.multiple_of(c * TK, TK)
            st = score_ref[c]
            drop = jnp.where(st == tau, pos_of(c), jnp.int32(-1)) >= bound
            score_ref[c] = jnp.where(drop, below, st)
            return carry

        lax.fori_loop(0, nkv, demote, 0)

    n_stage = A_HEADS
    dsa_grp = B_HEADS // n_stage
    a_dv = 2 * A_DH
    a_rows_v = a_dv + V_PAD

    def qk_diff(h, kstart):
        kh = akp_ref[0, pl.ds(kstart, TK), h * 256:(h + 1) * 256]
        return lax.dot_general(kh, qa_ref[h], _NT, preferred_element_type=F32)

    def qk_dsa(h, kstart):
        kb = bkp_ref[0, pl.ds(kstart, TK), :]
        qh = qb_ref[h * dsa_grp * TQ:(h + 1) * dsa_grp * TQ, :]
        return lax.dot_general(kb, qh, _NT, preferred_element_type=F32)

    def attend_block(j, diag):
        k0 = pl.multiple_of(j * TK, TK)
        kn0 = pl.multiple_of(jnp.minimum(j + 1, nkv - 1) * TK, TK)
        slot = lax.rem(j, 2)
        cap_sel = jnp.where(score_ref[j] >= tau, BIG, NEG)
        if diag:
            _, cap_chunk, corr = diag_geometry(j)
        vb = bv_ref[0, j]
        sd_next = sa_ref[slot]
        sb_next = sb_ref[slot]
        for h in range(n_stage):
            sd, sb = sd_next, sb_next
            if h + 1 < n_stage:
                sd_next = qk_diff(h + 1, k0)
                sb_next = qk_dsa(h + 1, k0)
            else:
                sa_ref[1 - slot] = qk_diff(0, kn0)
                sb_ref[1 - slot] = qk_dsa(0, kn0)

            prep_d = None
            if diag:
                prep_d = functools.partial(
                    lambda sl, g, s: jnp.minimum(s, cap_chunk) + sl * corr, slopes_a[h])
            vh = av_ref[0, j, h * a_rows_v:(h + 1) * a_rows_v, :]
            _online_update(sd, vh, acc_a_ref, m_a_ref, 2 * h, 2, prep_d)

            def prep_b(h0, g, s):
                s = jnp.minimum(s, cap_sel)
                return s + slopes_b[h0 + g] * corr if diag else s

            _online_update(sb, vb, acc_b_ref, m_b_ref, dsa_grp * h, dsa_grp,
                           functools.partial(prep_b, dsa_grp * h))

    def attend_body(diag, j, carry):
        attend_block(j, diag)
        return carry

    sa_ref[0] = qk_diff(0, 0)
    sb_ref[0] = qk_dsa(0, 0)
    lax.fori_loop(0, n_off, functools.partial(attend_body, False), 0)
    lax.fori_loop(n_off, nkv, functools.partial(attend_body, True), 0)

    lv = lamv_ref[...]
    lam = (jnp.exp(jnp.sum(lv[0:1] * lv[1:2], axis=1, keepdims=True))
           - jnp.exp(jnp.sum(lv[2:3] * lv[3:4], axis=1, keepdims=True)) + lam_init)
    g = g_ref[...]
    for h in range(A_HEADS):
        o1 = acc_a_ref[2 * h, 0:a_dv, :] / acc_a_ref[2 * h, a_dv:a_dv + 1, :]
        o2 = acc_a_ref[2 * h + 1, 0:a_dv, :] / acc_a_ref[2 * h + 1, a_dv:a_dv + 1, :]
        o = o1 - lam * o2
        y = o * lax.rsqrt(jnp.mean(o * o, axis=0, keepdims=True) + EPS)
        out_ref[0, 0, h * 128:(h + 1) * 128, :] = ((y * g) * (1.0 - lam_init)).astype(BF16)
    a_rows = A_HEADS * 2 * A_DH
    for h in range(B_HEADS):
        ob = acc_b_ref[h, 0:B_DH, :] / acc_b_ref[h, B_DH:B_DH + 1, :]
        out_ref[0, 0, a_rows + h * B_DH:a_rows + (h + 1) * B_DH, :] = ob.astype(BF16)


def _attention(lam_init, aq, bqp, iq, iwt, akp, avt, bkp, ikk, bvt, lamv, gt):
    bsz, s, a_cols = aq.shape
    nq = s // TQ
    nk = s // TK
    i_cols = iq.shape[2]
    d_mix = a_cols + B_HEADS * B_DH

    def per_batch(shape):
        return pl.BlockSpec(shape, lambda b, i: (b,) + (0,) * (len(shape) - 1))

    in_specs = [
        pl.BlockSpec((1, TQ, a_cols), lambda b, i: (b, i, 0)),
        pl.BlockSpec((1, TQ, bqp.shape[2]), lambda b, i: (b, i, 0)),
        pl.BlockSpec((1, TQ, i_cols), lambda b, i: (b, i, 0)),
        pl.BlockSpec((1, TQ // TK, IDX_HEADS, TK), lambda b, i: (b, i, 0, 0)),
        per_batch((1, s, akp.shape[2])),
        per_batch((1, nk, avt.shape[2], TK)),
        per_batch((1, s, 128)),
        per_batch((1, s, 128)),
        per_batch((1, nk, bvt.shape[2], TK)),
        _const_spec(lamv.shape),
        _const_spec(gt.shape),
    ]
    n_maps = 2 * A_HEADS
    scratch = [
        pltpu.VMEM((nk, TK, TQ), F32),
        pltpu.VMEM((2, 8, TQ), F32),
        pltpu.VMEM((2, TK, 2 * TQ), F32),
        pltpu.VMEM((2, TK, B_HEADS // A_HEADS * TQ), F32),
        pltpu.VMEM((A_HEADS, 2 * TQ, 256), BF16),
        pltpu.VMEM((A_HEADS, IDX_HEADS // A_HEADS * TQ, 128), BF16),
        pltpu.VMEM((B_HEADS * TQ, 128), BF16),
        pltpu.VMEM((IDX_HEADS, TQ), F32),
        pltpu.VMEM((n_maps, 2 * A_DH + V_PAD, TQ), F32),
        pltpu.VMEM((n_maps, 1, TQ), F32),
        pltpu.VMEM((B_HEADS, B_DH + V_PAD, TQ), F32),
        pltpu.VMEM((B_HEADS, 1, TQ), F32),
    ]
    return pl.pallas_call(
        functools.partial(_attn_kernel, lam_init),
        grid=(bsz, nq),
        in_specs=in_specs,
        out_specs=pl.BlockSpec((1, 1, d_mix, TQ), lambda b, i: (b, i, 0, 0)),
        out_shape=jax.ShapeDtypeStruct((bsz, nq, d_mix, TQ), BF16),
        scratch_shapes=scratch,
        compiler_params=pltpu.CompilerParams(
            dimension_semantics=("arbitrary", "arbitrary"), vmem_limit_bytes=VMEM_LIMIT_BYTES),
        name="attention",
    )(aq, bqp, iq, iwt, akp, avt, bkp, ikk, bvt, lamv, gt)


def _ffn_kernel(final, mix_ref, x_ref, mod_ref, n2_ref, fg_ref, wo_ref, w1_ref, w2_ref, o_ref):
    wo = wo_ref[...]
    y = jnp.concatenate(
        [lax.dot_general(mix_ref[0, c], wo, _TN, preferred_element_type=F32)
         for c in range(TM_FFN // TQ)], axis=0)
    g1 = mod_ref[0, 2:3, :]
    sh2 = mod_ref[0, 3:4, :]
    sc2 = mod_ref[0, 4:5, :]
    g2 = mod_ref[0, 5:6, :]
    x1 = x_ref[0] + g1 * y
    h2 = ((_rms(x1) * n2_ref[...]) * (1.0 + sc2) + sh2).astype(BF16)
    ff = jnp.zeros_like(x1)
    for c in range(w1_ref.shape[1] // FF_CHUNK):
        u = jnp.dot(h2, w1_ref[:, c * FF_CHUNK:(c + 1) * FF_CHUNK], preferred_element_type=F32)
        u = jnp.square(jnp.maximum(u, 0.0)).astype(BF16)
        ff = ff + jnp.dot(u, w2_ref[c * FF_CHUNK:(c + 1) * FF_CHUNK, :], preferred_element_type=F32)
    x2 = x1 + g2 * ff
    if final:
        x2 = _rms(x2) * fg_ref[...]
    o_ref[0] = x2


def _out_ffn(final, mixt, x, mod, n2g, fg, wo, w1, w2):
    bsz, s, d = x.shape
    d_mix = mixt.shape[2]
    tm = TM_FFN
    cpt = tm // TQ
    return pl.pallas_call(
        functools.partial(_ffn_kernel, final),
        grid=(bsz, s // tm),
        in_specs=[
            pl.BlockSpec((1, cpt, d_mix, TQ), lambda b, t: (b, t, 0, 0)),
            pl.BlockSpec((1, tm, d), lambda b, t: (b, t, 0)),
            pl.BlockSpec((1, 6, d), lambda b, t: (b, 0, 0)),
            _const_spec(n2g.shape),
            _const_spec(fg.shape),
            _const_spec(wo.shape),
            _const_spec(w1.shape),
            _const_spec(w2.shape),
        ],
        out_specs=pl.BlockSpec((1, tm, d), lambda b, t: (b, t, 0)),
        out_shape=jax.ShapeDtypeStruct((bsz, s, d), F32),
        compiler_params=pltpu.CompilerParams(
            dimension_semantics=("arbitrary", "arbitrary"), vmem_limit_bytes=VMEM_LIMIT_BYTES),
        name="out_ffn",
    )(mixt, x, mod, n2g, fg, wo, w1, w2)


def _split_w_in(w):
    d = w.shape[0]
    a = A_HEADS * 2 * A_DH
    sizes = (a, a, a, B_HEADS * B_DH, B_DH, B_DH, IDX_HEADS * IDX_DH, IDX_DH, IDX_HEADS)
    offs = np.cumsum((0,) + sizes)
    aq, ak, av, bq, bk, bv, iq, ik, iw = [w[:, offs[n]:offs[n + 1]] for n in range(len(sizes))]
    bq_pad = jnp.pad(bq.reshape(d, B_HEADS, B_DH), ((0, 0), (0, 0), (0, 128 - B_DH)))
    bq_pad = bq_pad.reshape(d, B_HEADS * 128)
    kk = jnp.concatenate([bk, jnp.zeros((d, 128 - B_DH), w.dtype), ik, ik], axis=1)
    nat = [aq, ak, bq_pad, iq, kk]
    trn = [av.T, bv.T, iw.T]
    return [m.astype(BF16) for m in nat + trn]


def _dsa_slope_columns():
    row = np.zeros((1, B_HEADS * 128), np.float32)
    for h, sl in enumerate(_alibi_slopes(B_HEADS)):
        row[0, h * 128 + B_DH] = sl * POS_SPLIT
        row[0, h * 128 + B_DH + 1] = sl
    return jnp.asarray(row)


def kernel(x, c, norm1_g, norm2_g, w_ada, b_ada, w_in, lam_q1, lam_k1, lam_q2, lam_k2,
           subln_g, w_out, w_ff1, w_ff2, final_g):
    bsz, s, d = x.shape
    depth = w_in.shape[0]
    assert s % TS_PROJ == 0 and s % TM_FFN == 0 and s % TQ == 0 and TQ % TK == 0 and TM_FFN % TQ == 0
    assert min(TOPK_MAX, s // 4) == TOPK_MAX and s // POS_SPLIT <= 256 and CHUNK == 1 << CHUNK_SHIFT
    fg = final_g.reshape(1, d)
    bqc = _dsa_slope_columns()
    for l in range(depth):
        lam_init = 0.8 - 0.6 * math.exp(-0.3 * l)
        mod = _adaln_mod(c, w_ada[l].astype(BF16), b_ada[l].reshape(1, -1)).reshape(bsz, 6, d)
        aq, akp, bqp, iq, bkp, ikk, avt, bvt, iwt = _in_proj(
            x, mod, norm1_g[l].reshape(1, d), _split_w_in(w_in[l]), bqc)
        lamv = jnp.stack([lam_q1[l], lam_k1[l], lam_q2[l], lam_k2[l]])
        gt = jnp.broadcast_to(subln_g[l][:, None], (2 * A_DH, TQ))
        mixt = _attention(lam_init, aq, bqp, iq, iwt, akp, avt, bkp, ikk, bvt, lamv, gt)
        x = _out_ffn(l == depth - 1, mixt, x, mod, norm2_g[l].reshape(1, d), fg,
                     w_out[l].astype(BF16), w_ff1[l].astype(BF16), w_ff2[l].astype(BF16))
    return x
```

```python
import functools
import math

import jax
import jax.numpy as jnp
import numpy as np
from jax import lax
from jax.experimental import pallas as pl
from jax.experimental.pallas import tpu as pltpu

CHUNK = 64
CHUNK_SHIFT = 6
A_HEADS = 4
A_DH = 64
B_HEADS = 8
B_DH = 64
IDX_HEADS = 16
IDX_DH = 64
TOPK_MAX = 256
EPS = 1e-6

TQ = 256
TK = 256
TS_PROJ = 512
TM_FFN = 512
FF_CHUNK = 1024
VMEM_LIMIT_BYTES = 56 * 1024 * 1024
VALUE_PASSES = 24
V_PAD = 16
POS_SHIFT = 6
POS_SPLIT = 1 << POS_SHIFT

LOG2E = math.log2(math.e)
NEG = -1e30
BIG = 3e38
INT_MIN = -(2 ** 31)
F32_LOWEST = float(np.finfo(np.float32).min)
BF16 = jnp.bfloat16
F32 = jnp.float32
I32 = jnp.int32

_NT = (((1,), (1,)), ((), ()))
_TN = (((0,), (0,)), ((), ()))


def _alibi_slopes(n):
    return [2.0 ** (-8.0 * (i + 1) / n) for i in range(n)]


def _slope_columns(slope):
    pieces, rest = [], slope * LOG2E
    for _ in range(3):
        p = float(np.asarray(rest, np.float32).astype(BF16).astype(np.float32))
        pieces.append(p)
        rest -= p
    return [p * POS_SPLIT for p in pieces] + pieces


def _lane_constants(lane, l0, values):
    out = jnp.zeros(lane.shape, F32)
    for k, v in enumerate(values):
        out = jnp.where(lane == l0 + k, v, out)
    return out


def _const_spec(shape):
    nd = len(shape)
    return pl.BlockSpec(shape, lambda *_: (0,) * nd, pipeline_mode=pl.Buffered(1))


def _rms(x):
    return x * lax.rsqrt(jnp.mean(x * x, axis=-1, keepdims=True) + EPS)


def _mod_kernel(c_ref, w_ref, b_ref, o_ref):
    c = c_ref[...]
    cond = c * (1.0 / (1.0 + jnp.exp(-c)))
    o_ref[...] = jnp.dot(cond.astype(BF16), w_ref[...], preferred_element_type=F32) + b_ref[...]


def _adaln_mod(c, w_ada, b_ada):
    bsz, d = c.shape
    n = w_ada.shape[1]
    return pl.pallas_call(
        _mod_kernel,
        grid=(n // d,),
        in_specs=[
            pl.BlockSpec((bsz, d), lambda j: (0, 0)),
            pl.BlockSpec((d, d), lambda j: (0, j)),
            pl.BlockSpec((1, d), lambda j: (0, j)),
        ],
        out_specs=pl.BlockSpec((bsz, d), lambda j: (0, j)),
        out_shape=jax.ShapeDtypeStruct((bsz, n), F32),
        compiler_params=pltpu.CompilerParams(dimension_semantics=("arbitrary",)),
        name="adaln_mod",
    )(c, w_ada, b_ada)


def _inproj_kernel(x_ref, mod_ref, g_ref, waq_ref, wak_ref, wbq_ref, wiq_ref, wkk_ref,
                   wav_ref, wbv_ref, wiw_ref, bqc_ref,
                   aq_ref, akp_ref, bqp_ref, iq_ref, bkp_ref, ikk_ref, av_ref, bv_ref, iw_ref):
    x = x_ref[0]
    sh1 = mod_ref[0, 0:1, :]
    sc1 = mod_ref[0, 1:2, :]
    h = (_rms(x) * g_ref[...]) * (1.0 + sc1) + sh1
    hb = h.astype(BF16)

    def nat(w_ref):
        return jnp.dot(hb, w_ref[...], preferred_element_type=F32)

    pos = pl.program_id(1) * TS_PROJ + lax.broadcasted_iota(I32, (TS_PROJ, 128), 0)
    lane = lax.broadcasted_iota(I32, (TS_PROJ, 128), 1)
    pos_hi = (pos >> POS_SHIFT).astype(F32)
    pos_lo = (pos & (POS_SPLIT - 1)).astype(F32)

    def pos_tile(l0):
        return jnp.where((lane >= l0) & (lane < l0 + 3), pos_hi,
                         jnp.where((lane >= l0 + 3) & (lane < l0 + 6), pos_lo, 0.0))

    aq_ref[0] = (nat(waq_ref) * (A_DH ** -0.5 * LOG2E)).astype(BF16)
    ak = nat(wak_ref)
    pos0 = pos_tile(0).astype(BF16)
    for hd in range(A_HEADS):
        akp_ref[0, :, hd * 256:hd * 256 + 128] = ak[:, hd * 128:(hd + 1) * 128].astype(BF16)
        akp_ref[0, :, hd * 256 + 128:(hd + 1) * 256] = pos0
    bqp_ref[0] = (nat(wbq_ref) * (B_DH ** -0.5 * LOG2E) + bqc_ref[...]).astype(BF16)
    iq_ref[0] = nat(wiq_ref).astype(BF16)
    kk = nat(wkk_ref)
    bkp_ref[0] = (kk[:, 0:128] + pos_tile(B_DH)).astype(BF16)
    ikk_ref[0] = kk[:, 128:256].astype(BF16)

    def tr(w_ref):
        return lax.dot_general(w_ref[...], hb, _NT, preferred_element_type=F32)

    avt = tr(wav_ref).astype(BF16)
    bvt = tr(wbv_ref).astype(BF16)
    iwt = tr(wiw_ref)
    ones_rows = jnp.where(lax.broadcasted_iota(I32, (V_PAD, TK), 0) == 0, 1.0, 0.0).astype(BF16)
    a_dv = 2 * A_DH
    for c in range(TS_PROJ // TK):
        for hd in range(A_HEADS):
            r0 = hd * (a_dv + V_PAD)
            av_ref[0, c, r0:r0 + a_dv, :] = avt[hd * a_dv:(hd + 1) * a_dv, c * TK:(c + 1) * TK]
            av_ref[0, c, r0 + a_dv:r0 + a_dv + V_PAD, :] = ones_rows
        bv_ref[0, c, 0:B_DH, :] = bvt[:, c * TK:(c + 1) * TK]
        bv_ref[0, c, B_DH:B_DH + V_PAD, :] = ones_rows
        iw_ref[0, c] = iwt[:, c * TK:(c + 1) * TK]


def _in_proj(x, mod, g, ws, bqc):
    bsz, s, d = x.shape
    ts = TS_PROJ
    nk = s // TK
    cpt = ts // TK

    def tok(cols):
        return pl.BlockSpec((1, ts, cols), lambda b, t: (b, t, 0))

    def trs(rows):
        return pl.BlockSpec((1, cpt, rows, TK), lambda b, t: (b, t, 0, 0))

    a_cols = A_HEADS * 2 * A_DH
    i_cols = IDX_HEADS * IDX_DH
    out_shape = [
        jax.ShapeDtypeStruct((bsz, s, a_cols), BF16),
        jax.ShapeDtypeStruct((bsz, s, A_HEADS * 256), BF16),
        jax.ShapeDtypeStruct((bsz, s, B_HEADS * 128), BF16),
        jax.ShapeDtypeStruct((bsz, s, i_cols), BF16),
        jax.ShapeDtypeStruct((bsz, s, 128), BF16),
        jax.ShapeDtypeStruct((bsz, s, 128), BF16),
        jax.ShapeDtypeStruct((bsz, nk, A_HEADS * (2 * A_DH + V_PAD), TK), BF16),
        jax.ShapeDtypeStruct((bsz, nk, B_DH + V_PAD, TK), BF16),
        jax.ShapeDtypeStruct((bsz, nk, IDX_HEADS, TK), F32),
    ]
    out_specs = [tok(a_cols), tok(A_HEADS * 256), tok(B_HEADS * 128), tok(i_cols), tok(128),
                 tok(128), trs(A_HEADS * (2 * A_DH + V_PAD)), trs(B_DH + V_PAD), trs(IDX_HEADS)]
    in_specs = [
        pl.BlockSpec((1, ts, d), lambda b, t: (b, t, 0)),
        pl.BlockSpec((1, 6, d), lambda b, t: (b, 0, 0)),
        _const_spec(g.shape),
    ] + [_const_spec(w.shape) for w in ws] + [_const_spec(bqc.shape)]
    return pl.pallas_call(
        _inproj_kernel,
        grid=(bsz, s // ts),
        in_specs=in_specs,
        out_specs=out_specs,
        out_shape=out_shape,
        compiler_params=pltpu.CompilerParams(
            dimension_semantics=("arbitrary", "arbitrary"), vmem_limit_bytes=VMEM_LIMIT_BYTES),
        name="in_proj",
    )(x, mod, g, *ws, bqc)


def _online_update(s_all, v, acc_ref, m_ref, idx0, n, prep=None):
    ps, alphas = [], []
    for g in range(n):
        s = s_all[:, g * TQ:(g + 1) * TQ]
        if prep is not None:
            s = prep(g, s)
        m_prev = m_ref[idx0 + g]
        m_new = jnp.maximum(m_prev, jnp.max(s, axis=0, keepdims=True))
        alphas.append(jnp.exp2(m_prev - m_new))
        ps.append(jnp.exp2(s - m_new).astype(BF16))
        m_ref[idx0 + g] = m_new
    pv = jnp.dot(v, jnp.concatenate(ps, axis=1), preferred_element_type=F32)
    for g in range(n):
        acc_ref[idx0 + g] = alphas[g] * acc_ref[idx0 + g] + pv[:, g * TQ:(g + 1) * TQ]


def _key_to_f32(key):
    return pltpu.bitcast(key ^ ((key >> 31) & jnp.int32(0x7FFFFFFF)), F32)


def _f32_to_key(x):
    bits = pltpu.bitcast(x, I32)
    return bits ^ ((bits >> 31) & jnp.int32(0x7FFFFFFF))


def _as_exact_f32(x):
    return _key_to_f32(_f32_to_key(x))


def _attn_kernel(lam_init, aq_ref, bqp_ref, iq_ref, iw_ref, akp_ref, av_ref, bkp_ref, ikk_ref,
                 bv_ref, lamv_ref, g_ref, out_ref,
                 score_ref, rng_ref, sa_ref, sb_ref, qa_ref, qi_ref, qb_ref, w_ref,
                 acc_a_ref, m_a_ref, acc_b_ref, m_b_ref):
    i = pl.program_id(1)
    n_off = i * (TQ // TK)
    nkv = n_off + TQ // TK
    slopes_a = _alibi_slopes(A_HEADS)
    slopes_b = _alibi_slopes(B_HEADS)
    n_maps = 2 * A_HEADS

    lane = lax.broadcasted_iota(I32, (TQ, 128), 1)
    keep_lo = jnp.where(lane < 64, 1.0, 0.0).astype(BF16)
    keep_hi = jnp.where(lane < 64, 0.0, 1.0).astype(BF16)
    idx_grp = IDX_HEADS // A_HEADS
    for h in range(A_HEADS):
        qh = aq_ref[0, :, h * 128:(h + 1) * 128]
        slope_cols = _lane_constants(lane, 0, _slope_columns(slopes_a[h])).astype(BF16)
        qa_ref[h, 0:TQ, 0:128] = qh * keep_lo
        qa_ref[h, TQ:2 * TQ, 0:128] = qh * keep_hi
        qa_ref[h, 0:TQ, 128:256] = slope_cols
        qa_ref[h, TQ:2 * TQ, 128:256] = slope_cols
    for t in range(IDX_HEADS // 2):
        qt = iq_ref[0, :, t * 128:(t + 1) * 128]
        for half, keep in enumerate((keep_lo, keep_hi)):
            g, k = divmod(2 * t + half, idx_grp)
            qi_ref[g, k * TQ:(k + 1) * TQ, :] = qt * keep
    for h in range(B_HEADS):
        qb_ref[h * TQ:(h + 1) * TQ, :] = bqp_ref[0, :, h * 128:(h + 1) * 128]
    for c in range(TQ // TK):
        w_ref[:, c * TK:(c + 1) * TK] = iw_ref[0, c] * ((IDX_HEADS * IDX_DH) ** -0.5)

    acc_a_ref[...] = jnp.zeros_like(acc_a_ref)
    acc_b_ref[...] = jnp.zeros_like(acc_b_ref)
    m_a_ref[...] = jnp.full_like(m_a_ref, NEG)
    m_b_ref[...] = jnp.full_like(m_b_ref, NEG)

    def diag_geometry(j):
        row = j * TK + lax.broadcasted_iota(I32, (TK, TQ), 0)
        col = i * TQ + lax.broadcasted_iota(I32, (TK, TQ), 1)
        allowed = row < ((col >> CHUNK_SHIFT) + 1) * CHUNK
        corr = (2.0 * LOG2E) * jnp.minimum((col - row).astype(F32), 0.0)
        return allowed, jnp.where(allowed, BIG, NEG), corr

    def index_block(j, diag):
        k0 = pl.multiple_of(j * TK, TK)
        kk = ikk_ref[0, pl.ds(k0, TK), :]
        score = jnp.zeros((TK, TQ), F32)
        for g in range(IDX_HEADS // idx_grp):
            xg = lax.dot_general(kk, qi_ref[g], _NT, preferred_element_type=F32)
            for k in range(idx_grp):
                hh = g * idx_grp + k
                score = score + w_ref[hh:hh + 1, :] * jnp.maximum(xg[:, k * TQ:(k + 1) * TQ], 0.0)
        lo_src = score
        if diag:
            allowed, _, _ = diag_geometry(j)
            lo_src = jnp.where(allowed, score, jnp.inf)
            score = jnp.where(allowed, score, -jnp.inf)
        score_ref[j] = score
        rng_ref[0] = jnp.minimum(rng_ref[0], jnp.min(lo_src.reshape(TK // 8, 8, TQ), axis=0))
        rng_ref[1] = jnp.maximum(rng_ref[1], jnp.max(score.reshape(TK // 8, 8, TQ), axis=0))

    def index_body(diag, j, carry):
        index_block(j, diag)
        return carry

    rng_ref[0] = jnp.full((8, TQ), jnp.inf, F32)
    rng_ref[1] = jnp.full((8, TQ), -jnp.inf, F32)
    lax.fori_loop(0, n_off, functools.partial(index_body, False), 0)
    lax.fori_loop(n_off, nkv, functools.partial(index_body, True), 0)

    qpos = i * TQ + lax.broadcasted_iota(I32, (1, TQ), 1)
    n_adm = ((qpos >> CHUNK_SHIFT) + 1) * CHUNK
    searching = n_adm > TOPK_MAX

    def count_where(pred_fn):
        def body(c, cnt):
            k0 = pl.multiple_of(c * TK, TK)
            ind = jnp.where(pred_fn(score_ref[c], c), jnp.int32(1), jnp.int32(0))
            return cnt + jnp.sum(ind.reshape(TK // 8, 8, TQ), axis=0)
        cnt8 = lax.fori_loop(0, nkv, body, jnp.zeros((8, TQ), I32))
        return jnp.sum(cnt8, axis=0, keepdims=True)

    n_acc = 4

    def count_ge(cand):
        def body(c, accs):
            k0 = pl.multiple_of(c * TK, TK)
            accs = list(accs)
            for r in range(TK // 8):
                a = accs[r % n_acc]
                accs[r % n_acc] = jnp.where(score_ref[c, r * 8:(r + 1) * 8, :] >= cand, a + 1, a)
            return tuple(accs)
        accs = lax.fori_loop(0, nkv, body,
                             tuple(jnp.zeros((8, TQ), I32) for _ in range(n_acc)))
        return jnp.sum(functools.reduce(lambda a, b: a + b, accs), axis=0, keepdims=True)

    def value_pass(_, carry):
        lo, hi, cnt_lo = carry
        mid = _as_exact_f32(0.5 * lo + 0.5 * hi)
        cnt = count_ge(mid)
        inside = jnp.logical_and(mid > lo, mid < hi)
        up = jnp.logical_and(inside, cnt >= TOPK_MAX)
        dn = jnp.logical_and(inside, cnt < TOPK_MAX)
        return jnp.where(up, mid, lo), jnp.where(dn, mid, hi), jnp.where(up, cnt, cnt_lo)

    lo0 = _as_exact_f32(jnp.min(rng_ref[0], axis=0, keepdims=True))
    hi0 = _key_to_f32(_f32_to_key(jnp.max(rng_ref[1], axis=0, keepdims=True)) + 1)
    lo, hi, cnt_at = lax.fori_loop(0, VALUE_PASSES, value_pass, (lo0, hi0, n_adm))

    def unresolved(lo, hi, cnt_lo):
        gap = _f32_to_key(hi) - _f32_to_key(lo)
        open_ = jnp.logical_and(searching, jnp.logical_and(cnt_lo != TOPK_MAX, gap != 1))
        return jnp.where(open_, jnp.int32(1), jnp.int32(0))

    def key_cond(carry):
        return jnp.max(unresolved(*carry)) > 0

    def key_pass(carry):
        lo, hi, cnt_lo = carry
        klo = _f32_to_key(lo)
        kmid = klo + lax.shift_right_logical(_f32_to_key(hi) - klo, jnp.int32(1))
        mid = _key_to_f32(kmid)
        cnt = count_ge(mid)
        live = unresolved(lo, hi, cnt_lo) > 0
        up = jnp.logical_and(live, cnt >= TOPK_MAX)
        dn = jnp.logical_and(live, cnt < TOPK_MAX)
        return jnp.where(up, mid, lo), jnp.where(dn, mid, hi), jnp.where(up, cnt, cnt_lo)

    lo, hi, cnt_at = lax.while_loop(key_cond, key_pass, (lo, hi, cnt_at))
    tau = jnp.where(searching, lo, F32_LOWEST)
    cnt_at = jnp.where(searching, cnt_at, TOPK_MAX)

    @pl.when(jnp.max(cnt_at) > TOPK_MAX)
    def _():
        def pos_of(c):
            return lax.broadcasted_iota(I32, (TK, TQ), 0) + c * TK

        need = TOPK_MAX - count_where(lambda st, c: st > tau)
        n_bits = int(score_ref.shape[0] * TK).bit_length()

        def idx_body(t, bound):
            trial = bound | jnp.left_shift(jnp.int32(1), n_bits - 1 - t)
            cnt = count_where(
                lambda st, c: jnp.where(st == tau, pos_of(c), jnp.int32(1 << 30)) < trial)
            return jnp.where(cnt <= need, trial, bound)

        bound = lax.fori_loop(0, n_bits, idx_body, jnp.zeros((1, TQ), I32))
        below = _key_to_f32(_f32_to_key(tau) - 1)

        def demote(c, carry):
            k0 = pl.multiple_of(c * TK, TK)
            st = score_ref[c]
            drop = jnp.where(st == tau, pos_of(c), jnp.int32(-1)) >= bound
            score_ref[c] = jnp.where(drop, below, st)
            return carry

        lax.fori_loop(0, nkv, demote, 0)

    n_stage = A_HEADS
    dsa_grp = B_HEADS // n_stage
    a_dv = 2 * A_DH
    a_rows_v = a_dv + V_PAD

    def qk_diff(h, kstart):
        kh = akp_ref[0, pl.ds(kstart, TK), h * 256:(h + 1) * 256]
        return lax.dot_general(kh, qa_ref[h], _NT, preferred_element_type=F32)

    def qk_dsa(h, kstart):
        kb = bkp_ref[0, pl.ds(kstart, TK), :]
        qh = qb_ref[h * dsa_grp * TQ:(h + 1) * dsa_grp * TQ, :]
        return lax.dot_general(kb, qh, _NT, preferred_element_type=F32)

    def attend_block(j, diag):
        k0 = pl.multiple_of(j * TK, TK)
        kn0 = pl.multiple_of(jnp.minimum(j + 1, nkv - 1) * TK, TK)
        slot = lax.rem(j, 2)
        cap_sel = jnp.where(score_ref[j] >= tau, BIG, NEG)
        if diag:
            _, cap_chunk, corr = diag_geometry(j)
        vb = bv_ref[0, j]
        sd_next = sa_ref[slot]
        sb_next = sb_ref[slot]
        for h in range(n_stage):
            sd, sb = sd_next, sb_next
            if h + 1 < n_stage:
                sd_next = qk_diff(h + 1, k0)
                sb_next = qk_dsa(h + 1, k0)
            else:
                sa_ref[1 - slot] = qk_diff(0, kn0)
                sb_ref[1 - slot] = qk_dsa(0, kn0)

            prep_d = None
            if diag:
                prep_d = functools.partial(
                    lambda sl, g, s: jnp.minimum(s, cap_chunk) + sl * corr, slopes_a[h])
            vh = av_ref[0, j, h * a_rows_v:(h + 1) * a_rows_v, :]
            _online_update(sd, vh, acc_a_ref, m_a_ref, 2 * h, 2, prep_d)

            def prep_b(h0, g, s):
                s = jnp.minimum(s, cap_sel)
                return s + slopes_b[h0 + g] * corr if diag else s

            _online_update(sb, vb, acc_b_ref, m_b_ref, dsa_grp * h, dsa_grp,
                           functools.partial(prep_b, dsa_grp * h))

    def attend_body(diag, j, carry):
        attend_block(j, diag)
        return carry

    sa_ref[0] = qk_diff(0, 0)
    sb_ref[0] = qk_dsa(0, 0)
    lax.fori_loop(0, n_off, functools.partial(attend_body, False), 0)
    lax.fori_loop(n_off, nkv, functools.partial(attend_body, True), 0)

    lv = lamv_ref[...]
    lam = (jnp.exp(jnp.sum(lv[0:1] * lv[1:2], axis=1, keepdims=True))
           - jnp.exp(jnp.sum(lv[2:3] * lv[3:4], axis=1, keepdims=True)) + lam_init)
    g = g_ref[...]
    for h in range(A_HEADS):
        o1 = acc_a_ref[2 * h, 0:a_dv, :] / acc_a_ref[2 * h, a_dv:a_dv + 1, :]
        o2 = acc_a_ref[2 * h + 1, 0:a_dv, :] / acc_a_ref[2 * h + 1, a_dv:a_dv + 1, :]
        o = o1 - lam * o2
        y = o * lax.rsqrt(jnp.mean(o * o, axis=0, keepdims=True) + EPS)
        out_ref[0, 0, h * 128:(h + 1) * 128, :] = ((y * g) * (1.0 - lam_init)).astype(BF16)
    a_rows = A_HEADS * 2 * A_DH
    for h in range(B_HEADS):
        ob = acc_b_ref[h, 0:B_DH, :] / acc_b_ref[h, B_DH:B_DH + 1, :]
        out_ref[0, 0, a_rows + h * B_DH:a_rows + (h + 1) * B_DH, :] = ob.astype(BF16)


def _attention(lam_init, aq, bqp, iq, iwt, akp, avt, bkp, ikk, bvt, lamv, gt):
    bsz, s, a_cols = aq.shape
    nq = s // TQ
    nk = s // TK
    i_cols = iq.shape[2]
    d_mix = a_cols + B_HEADS * B_DH

    def per_batch(shape):
        return pl.BlockSpec(shape, lambda b, i: (b,) + (0,) * (len(shape) - 1))

    in_specs = [
        pl.BlockSpec((1, TQ, a_cols), lambda b, i: (b, i, 0)),
        pl.BlockSpec((1, TQ, bqp.shape[2]), lambda b, i: (b, i, 0)),
        pl.BlockSpec((1, TQ, i_cols), lambda b, i: (b, i, 0)),
        pl.BlockSpec((1, TQ // TK, IDX_HEADS, TK), lambda b, i: (b, i, 0, 0)),
        per_batch((1, s, akp.shape[2])),
        per_batch((1, nk, avt.shape[2], TK)),
        per_batch((1, s, 128)),
        per_batch((1, s, 128)),
        per_batch((1, nk, bvt.shape[2], TK)),
        _const_spec(lamv.shape),
        _const_spec(gt.shape),
    ]
    n_maps = 2 * A_HEADS
    scratch = [
        pltpu.VMEM((nk, TK, TQ), F32),
        pltpu.VMEM((2, 8, TQ), F32),
        pltpu.VMEM((2, TK, 2 * TQ), F32),
        pltpu.VMEM((2, TK, B_HEADS // A_HEADS * TQ), F32),
        pltpu.VMEM((A_HEADS, 2 * TQ, 256), BF16),
        pltpu.VMEM((A_HEADS, IDX_HEADS // A_HEADS * TQ, 128), BF16),
        pltpu.VMEM((B_HEADS * TQ, 128), BF16),
        pltpu.VMEM((IDX_HEADS, TQ), F32),
        pltpu.VMEM((n_maps, 2 * A_DH + V_PAD, TQ), F32),
        pltpu.VMEM((n_maps, 1, TQ), F32),
        pltpu.VMEM((B_HEADS, B_DH + V_PAD, TQ), F32),
        pltpu.VMEM((B_HEADS, 1, TQ), F32),
    ]
    return pl.pallas_call(
        functools.partial(_attn_kernel, lam_init),
        grid=(bsz, nq),
        in_specs=in_specs,
        out_specs=pl.BlockSpec((1, 1, d_mix, TQ), lambda b, i: (b, i, 0, 0)),
        out_shape=jax.ShapeDtypeStruct((bsz, nq, d_mix, TQ), BF16),
        scratch_shapes=scratch,
        compiler_params=pltpu.CompilerParams(
            dimension_semantics=("arbitrary", "arbitrary"), vmem_limit_bytes=VMEM_LIMIT_BYTES),
        name="attention",
    )(aq, bqp, iq, iwt, akp, avt, bkp, ikk, bvt, lamv, gt)


def _ffn_kernel(final, mix_ref, x_ref, mod_ref, n2_ref, fg_ref, wo_ref, w1_ref, w2_ref, o_ref):
    wo = wo_ref[...]
    y = jnp.concatenate(
        [lax.dot_general(mix_ref[0, c], wo, _TN, preferred_element_type=F32)
         for c in range(TM_FFN // TQ)], axis=0)
    g1 = mod_ref[0, 2:3, :]
    sh2 = mod_ref[0, 3:4, :]
    sc2 = mod_ref[0, 4:5, :]
    g2 = mod_ref[0, 5:6, :]
    x1 = x_ref[0] + g1 * y
    h2 = ((_rms(x1) * n2_ref[...]) * (1.0 + sc2) + sh2).astype(BF16)
    ff = jnp.zeros_like(x1)
    for c in range(w1_ref.shape[1] // FF_CHUNK):
        u = jnp.dot(h2, w1_ref[:, c * FF_CHUNK:(c + 1) * FF_CHUNK], preferred_element_type=F32)
        u = jnp.square(jnp.maximum(u, 0.0)).astype(BF16)
        ff = ff + jnp.dot(u, w2_ref[c * FF_CHUNK:(c + 1) * FF_CHUNK, :], preferred_element_type=F32)
    x2 = x1 + g2 * ff
    if final:
        x2 = _rms(x2) * fg_ref[...]
    o_ref[0] = x2


def _out_ffn(final, mixt, x, mod, n2g, fg, wo, w1, w2):
    bsz, s, d = x.shape
    d_mix = mixt.shape[2]
    tm = TM_FFN
    cpt = tm // TQ
    return pl.pallas_call(
        functools.partial(_ffn_kernel, final),
        grid=(bsz, s // tm),
        in_specs=[
            pl.BlockSpec((1, cpt, d_mix, TQ), lambda b, t: (b, t, 0, 0)),
            pl.BlockSpec((1, tm, d), lambda b, t: (b, t, 0)),
            pl.BlockSpec((1, 6, d), lambda b, t: (b, 0, 0)),
            _const_spec(n2g.shape),
            _const_spec(fg.shape),
            _const_spec(wo.shape),
            _const_spec(w1.shape),
            _const_spec(w2.shape),
        ],
        out_specs=pl.BlockSpec((1, tm, d), lambda b, t: (b, t, 0)),
        out_shape=jax.ShapeDtypeStruct((bsz, s, d), F32),
        compiler_params=pltpu.CompilerParams(
            dimension_semantics=("arbitrary", "arbitrary"), vmem_limit_bytes=VMEM_LIMIT_BYTES),
        name="out_ffn",
    )(mixt, x, mod, n2g, fg, wo, w1, w2)


def _split_w_in(w):
    d = w.shape[0]
    a = A_HEADS * 2 * A_DH
    sizes = (a, a, a, B_HEADS * B_DH, B_DH, B_DH, IDX_HEADS * IDX_DH, IDX_DH, IDX_HEADS)
    offs = np.cumsum((0,) + sizes)
    aq, ak, av, bq, bk, bv, iq, ik, iw = [w[:, offs[n]:offs[n + 1]] for n in range(len(sizes))]
    bq_pad = jnp.pad(bq.reshape(d, B_HEADS, B_DH), ((0, 0), (0, 0), (0, 128 - B_DH)))
    bq_pad = bq_pad.reshape(d, B_HEADS * 128)
    kk = jnp.concatenate([bk, jnp.zeros((d, 128 - B_DH), w.dtype), ik, ik], axis=1)
    nat = [aq, ak, bq_pad, iq, kk]
    trn = [av.T, bv.T, iw.T]
    return [m.astype(BF16) for m in nat + trn]


def _dsa_slope_columns():
    row = np.zeros((1, B_HEADS * 128), np.float32)
    for h, sl in enumerate(_alibi_slopes(B_HEADS)):
        for k, v in enumerate(_slope_columns(sl)):
            row[0, h * 128 + B_DH + k] = v
    return jnp.asarray(row)


def kernel(x, c, norm1_g, norm2_g, w_ada, b_ada, w_in, lam_q1, lam_k1, lam_q2, lam_k2,
           subln_g, w_out, w_ff1, w_ff2, final_g):
    bsz, s, d = x.shape
    depth = w_in.shape[0]
    assert s % TS_PROJ == 0 and s % TM_FFN == 0 and s % TQ == 0 and TQ % TK == 0 and TM_FFN % TQ == 0
    assert min(TOPK_MAX, s // 4) == TOPK_MAX and s // POS_SPLIT <= 256 and CHUNK == 1 << CHUNK_SHIFT
    fg = final_g.reshape(1, d)
    bqc = _dsa_slope_columns()
    for l in range(depth):
        lam_init = 0.8 - 0.6 * math.exp(-0.3 * l)
        mod = _adaln_mod(c, w_ada[l].astype(BF16), b_ada[l].reshape(1, -1)).reshape(bsz, 6, d)
        aq, akp, bqp, iq, bkp, ikk, avt, bvt, iwt = _in_proj(
            x, mod, norm1_g[l].reshape(1, d), _split_w_in(w_in[l]), bqc)
        lamv = jnp.stack([lam_q1[l], lam_k1[l], lam_q2[l], lam_k2[l]])
        gt = jnp.broadcast_to(subln_g[l][:, None], (2 * A_DH, TQ))
        mixt = _attention(lam_init, aq, bqp, iq, iwt, akp, avt, bkp, ikk, bvt, lamv, gt)
        x = _out_ffn(l == depth - 1, mixt, x, mod, norm2_g[l].reshape(1, d), fg,
                     w_out[l].astype(BF16), w_ff1[l].astype(BF16), w_ff2[l].astype(BF16))
    return x
```

```python
import functools
import math

import jax
import jax.numpy as jnp
import numpy as np
from jax import lax
from jax.experimental import pallas as pl
from jax.experimental.pallas import tpu as pltpu

CHUNK = 64
CHUNK_SHIFT = 6
A_HEADS = 4
A_DH = 64
B_HEADS = 8
B_DH = 64
IDX_HEADS = 16
IDX_DH = 64
TOPK_MAX = 256
EPS = 1e-6

TQ = 256
TK = 256
TS_PROJ = 512
TM_FFN = 512
FF_CHUNK = 1024
VMEM_LIMIT_BYTES = 56 * 1024 * 1024
VALUE_PASSES = 24
V_PAD = 16
POS_SHIFT = 6
POS_SPLIT = 1 << POS_SHIFT

LOG2E = math.log2(math.e)
NEG = -1e30
BIG = 3e38
INT_MIN = -(2 ** 31)
F32_LOWEST = float(np.finfo(np.float32).min)
BF16 = jnp.bfloat16
F32 = jnp.float32
I32 = jnp.int32

_NT = (((1,), (1,)), ((), ()))
_TN = (((0,), (0,)), ((), ()))


def _alibi_slopes(n):
    return [2.0 ** (-8.0 * (i + 1) / n) for i in range(n)]


def _slope_columns(slope):
    pieces, rest = [], slope * LOG2E
    for _ in range(3):
        p = float(np.asarray(rest, np.float32).astype(BF16).astype(np.float32))
        pieces.append(p)
        rest -= p
    return [p * POS_SPLIT for p in pieces] + pieces


def _lane_constants(lane, l0, values):
    out = jnp.zeros(lane.shape, F32)
    for k, v in enumerate(values):
        out = jnp.where(lane == l0 + k, v, out)
    return out


def _const_spec(shape):
    nd = len(shape)
    return pl.BlockSpec(shape, lambda *_: (0,) * nd, pipeline_mode=pl.Buffered(1))


def _rms(x):
    return x * lax.rsqrt(jnp.mean(x * x, axis=-1, keepdims=True) + EPS)


def _mod_kernel(c_ref, w_ref, b_ref, o_ref):
    c = c_ref[...]
    cond = c * (1.0 / (1.0 + jnp.exp(-c)))
    o_ref[...] = jnp.dot(cond.astype(BF16), w_ref[...], preferred_element_type=F32) + b_ref[...]


def _adaln_mod(c, w_ada, b_ada):
    bsz, d = c.shape
    n = w_ada.shape[1]
    return pl.pallas_call(
        _mod_kernel,
        grid=(n // d,),
        in_specs=[
            pl.BlockSpec((bsz, d), lambda j: (0, 0)),
            pl.BlockSpec((d, d), lambda j: (0, j)),
            pl.BlockSpec((1, d), lambda j: (0, j)),
        ],
        out_specs=pl.BlockSpec((bsz, d), lambda j: (0, j)),
        out_shape=jax.ShapeDtypeStruct((bsz, n), F32),
        compiler_params=pltpu.CompilerParams(dimension_semantics=("arbitrary",)),
        name="adaln_mod",
    )(c, w_ada, b_ada)


def _inproj_kernel(x_ref, mod_ref, g_ref, waq_ref, wak_ref, wbq_ref, wiq_ref, wkk_ref,
                   wav_ref, wbv_ref, wiw_ref, bqc_ref,
                   aq_ref, akp_ref, bqp_ref, iq_ref, bkp_ref, ikk_ref, av_ref, bv_ref, iw_ref):
    x = x_ref[0]
    sh1 = mod_ref[0, 0:1, :]
    sc1 = mod_ref[0, 1:2, :]
    h = (_rms(x) * g_ref[...]) * (1.0 + sc1) + sh1
    hb = h.astype(BF16)

    def nat(w_ref):
        return jnp.dot(hb, w_ref[...], preferred_element_type=F32)

    pos = pl.program_id(1) * TS_PROJ + lax.broadcasted_iota(I32, (TS_PROJ, 128), 0)
    lane = lax.broadcasted_iota(I32, (TS_PROJ, 128), 1)
    pos_hi = (pos >> POS_SHIFT).astype(F32)
    pos_lo = (pos & (POS_SPLIT - 1)).astype(F32)

    def pos_tile(l0):
        return jnp.where((lane >= l0) & (lane < l0 + 3), pos_hi,
                         jnp.where((lane >= l0 + 3) & (lane < l0 + 6), pos_lo, 0.0))

    aq_ref[0] = (nat(waq_ref) * (A_DH ** -0.5 * LOG2E)).astype(BF16)
    ak = nat(wak_ref)
    pos0 = pos_tile(0).astype(BF16)
    for hd in range(A_HEADS):
        akp_ref[0, :, hd * 256:hd * 256 + 128] = ak[:, hd * 128:(hd + 1) * 128].astype(BF16)
        akp_ref[0, :, hd * 256 + 128:(hd + 1) * 256] = pos0
    bq = nat(wbq_ref) * (B_DH ** -0.5 * LOG2E)
    for t in range(B_HEADS // 2):
        tile = bq[:, t * 128:(t + 1) * 128]
        for half, src in enumerate((tile, pltpu.roll(tile, 64, 1))):
            hd = 2 * t + half
            slab = jnp.where(lane < B_DH, src, bqc_ref[:, hd * 128:(hd + 1) * 128])
            bqp_ref[0, :, hd * 128:(hd + 1) * 128] = slab.astype(BF16)
    iq_ref[0] = nat(wiq_ref).astype(BF16)
    kk = nat(wkk_ref)
    bkp_ref[0] = (kk[:, 0:128] + pos_tile(B_DH)).astype(BF16)
    ikk_ref[0] = kk[:, 128:256].astype(BF16)

    def tr(w_ref):
        return lax.dot_general(w_ref[...], hb, _NT, preferred_element_type=F32)

    avt = tr(wav_ref).astype(BF16)
    bvt = tr(wbv_ref).astype(BF16)
    iwt = tr(wiw_ref)
    ones_rows = jnp.where(lax.broadcasted_iota(I32, (V_PAD, TK), 0) == 0, 1.0, 0.0).astype(BF16)
    a_dv = 2 * A_DH
    for c in range(TS_PROJ // TK):
        for hd in range(A_HEADS):
            r0 = hd * (a_dv + V_PAD)
            av_ref[0, c, r0:r0 + a_dv, :] = avt[hd * a_dv:(hd + 1) * a_dv, c * TK:(c + 1) * TK]
            av_ref[0, c, r0 + a_dv:r0 + a_dv + V_PAD, :] = ones_rows
        bv_ref[0, c, 0:B_DH, :] = bvt[:, c * TK:(c + 1) * TK]
        bv_ref[0, c, B_DH:B_DH + V_PAD, :] = ones_rows
        iw_ref[0, c] = iwt[:, c * TK:(c + 1) * TK]


def _in_proj(x, mod, g, ws, bqc):
    bsz, s, d = x.shape
    ts = TS_PROJ
    nk = s // TK
    cpt = ts // TK

    def tok(cols):
        return pl.BlockSpec((1, ts, cols), lambda b, t: (b, t, 0))

    def trs(rows):
        return pl.BlockSpec((1, cpt, rows, TK), lambda b, t: (b, t, 0, 0))

    a_cols = A_HEADS * 2 * A_DH
    i_cols = IDX_HEADS * IDX_DH
    out_shape = [
        jax.ShapeDtypeStruct((bsz, s, a_cols), BF16),
        jax.ShapeDtypeStruct((bsz, s, A_HEADS * 256), BF16),
        jax.ShapeDtypeStruct((bsz, s, B_HEADS * 128), BF16),
        jax.ShapeDtypeStruct((bsz, s, i_cols), BF16),
        jax.ShapeDtypeStruct((bsz, s, 128), BF16),
        jax.ShapeDtypeStruct((bsz, s, 128), BF16),
        jax.ShapeDtypeStruct((bsz, nk, A_HEADS * (2 * A_DH + V_PAD), TK), BF16),
        jax.ShapeDtypeStruct((bsz, nk, B_DH + V_PAD, TK), BF16),
        jax.ShapeDtypeStruct((bsz, nk, IDX_HEADS, TK), F32),
    ]
    out_specs = [tok(a_cols), tok(A_HEADS * 256), tok(B_HEADS * 128), tok(i_cols), tok(128),
                 tok(128), trs(A_HEADS * (2 * A_DH + V_PAD)), trs(B_DH + V_PAD), trs(IDX_HEADS)]
    in_specs = [
        pl.BlockSpec((1, ts, d), lambda b, t: (b, t, 0)),
        pl.BlockSpec((1, 6, d), lambda b, t: (b, 0, 0)),
        _const_spec(g.shape),
    ] + [_const_spec(w.shape) for w in ws] + [_const_spec(bqc.shape)]
    return pl.pallas_call(
        _inproj_kernel,
        grid=(bsz, s // ts),
        in_specs=in_specs,
        out_specs=out_specs,
        out_shape=out_shape,
        compiler_params=pltpu.CompilerParams(
            dimension_semantics=("arbitrary", "arbitrary"), vmem_limit_bytes=VMEM_LIMIT_BYTES),
        name="in_proj",
    )(x, mod, g, *ws, bqc)


def _online_update(s_all, v, acc_ref, m_ref, idx0, n, prep=None):
    ps, alphas = [], []
    for g in range(n):
        s = s_all[:, g * TQ:(g + 1) * TQ]
        if prep is not None:
            s = prep(g, s)
        m_prev = m_ref[idx0 + g]
        m_new = jnp.maximum(m_prev, jnp.max(s, axis=0, keepdims=True))
        alphas.append(jnp.exp2(m_prev - m_new))
        ps.append(jnp.exp2(s - m_new).astype(BF16))
        m_ref[idx0 + g] = m_new
    pv = jnp.dot(v, jnp.concatenate(ps, axis=1), preferred_element_type=F32)
    for g in range(n):
        acc_ref[idx0 + g] = alphas[g] * acc_ref[idx0 + g] + pv[:, g * TQ:(g + 1) * TQ]


def _key_to_f32(key):
    return pltpu.bitcast(key ^ ((key >> 31) & jnp.int32(0x7FFFFFFF)), F32)


def _f32_to_key(x):
    bits = pltpu.bitcast(x, I32)
    return bits ^ ((bits >> 31) & jnp.int32(0x7FFFFFFF))


def _as_exact_f32(x):
    return _key_to_f32(_f32_to_key(x))


def _attn_kernel(lam_init, aq_ref, bqp_ref, iq_ref, iw_ref, akp_ref, av_ref, bkp_ref, ikk_ref,
                 bv_ref, lamv_ref, g_ref, out_ref,
                 score_ref, rng_ref, sa_ref, sb_ref, qa_ref, qi_ref, qb_ref, w_ref,
                 acc_a_ref, m_a_ref, acc_b_ref, m_b_ref):
    i = pl.program_id(1)
    n_off = i * (TQ // TK)
    nkv = n_off + TQ // TK
    slopes_a = _alibi_slopes(A_HEADS)
    slopes_b = _alibi_slopes(B_HEADS)
    n_maps = 2 * A_HEADS

    lane = lax.broadcasted_iota(I32, (TQ, 128), 1)
    keep_lo = jnp.where(lane < 64, 1.0, 0.0).astype(BF16)
    keep_hi = jnp.where(lane < 64, 0.0, 1.0).astype(BF16)
    idx_grp = IDX_HEADS // A_HEADS
    for h in range(A_HEADS):
        qh = aq_ref[0, :, h * 128:(h + 1) * 128]
        slope_cols = _lane_constants(lane, 0, _slope_columns(slopes_a[h])).astype(BF16)
        qa_ref[h, 0:TQ, 0:128] = qh * keep_lo
        qa_ref[h, TQ:2 * TQ, 0:128] = qh * keep_hi
        qa_ref[h, 0:TQ, 128:256] = slope_cols
        qa_ref[h, TQ:2 * TQ, 128:256] = slope_cols
    for t in range(IDX_HEADS // 2):
        qt = iq_ref[0, :, t * 128:(t + 1) * 128]
        for half, keep in enumerate((keep_lo, keep_hi)):
            g, k = divmod(2 * t + half, idx_grp)
            qi_ref[g, k * TQ:(k + 1) * TQ, :] = qt * keep
    for h in range(B_HEADS):
        qb_ref[h * TQ:(h + 1) * TQ, :] = bqp_ref[0, :, h * 128:(h + 1) * 128]
    for c in range(TQ // TK):
        w_ref[:, c * TK:(c + 1) * TK] = iw_ref[0, c] * ((IDX_HEADS * IDX_DH) ** -0.5)

    acc_a_ref[...] = jnp.zeros_like(acc_a_ref)
    acc_b_ref[...] = jnp.zeros_like(acc_b_ref)
    m_a_ref[...] = jnp.full_like(m_a_ref, NEG)
    m_b_ref[...] = jnp.full_like(m_b_ref, NEG)

    def diag_geometry(j):
        row = j * TK + lax.broadcasted_iota(I32, (TK, TQ), 0)
        col = i * TQ + lax.broadcasted_iota(I32, (TK, TQ), 1)
        allowed = row < ((col >> CHUNK_SHIFT) + 1) * CHUNK
        corr = (2.0 * LOG2E) * jnp.minimum((col - row).astype(F32), 0.0)
        return allowed, jnp.where(allowed, BIG, NEG), corr

    def index_block(j, diag):
        k0 = pl.multiple_of(j * TK, TK)
        kk = ikk_ref[0, pl.ds(k0, TK), :]
        score = jnp.zeros((TK, TQ), F32)
        for g in range(IDX_HEADS // idx_grp):
            xg = lax.dot_general(kk, qi_ref[g], _NT, preferred_element_type=F32)
            for k in range(idx_grp):
                hh = g * idx_grp + k
                score = score + w_ref[hh:hh + 1, :] * jnp.maximum(xg[:, k * TQ:(k + 1) * TQ], 0.0)
        lo_src = score
        if diag:
            allowed, _, _ = diag_geometry(j)
            lo_src = jnp.where(allowed, score, jnp.inf)
            score = jnp.where(allowed, score, -jnp.inf)
        score_ref[j] = score
        rng_ref[0] = jnp.minimum(rng_ref[0], jnp.min(lo_src.reshape(TK // 8, 8, TQ), axis=0))
        rng_ref[1] = jnp.maximum(rng_ref[1], jnp.max(score.reshape(TK // 8, 8, TQ), axis=0))

    def index_body(diag, j, carry):
        index_block(j, diag)
        return carry

    rng_ref[0] = jnp.full((8, TQ), jnp.inf, F32)
    rng_ref[1] = jnp.full((8, TQ), -jnp.inf, F32)
    def index_pair_body(jj, carry):
        index_block(2 * jj, False)
        index_block(2 * jj + 1, False)
        return carry

    lax.fori_loop(0, n_off // 2, index_pair_body, 0)
    lax.fori_loop(n_off // 2 * 2, n_off, functools.partial(index_body, False), 0)
    lax.fori_loop(n_off, nkv, functools.partial(index_body, True), 0)

    qpos = i * TQ + lax.broadcasted_iota(I32, (1, TQ), 1)
    n_adm = ((qpos >> CHUNK_SHIFT) + 1) * CHUNK
    searching = n_adm > TOPK_MAX

    def count_where(pred_fn):
        def body(c, cnt):
            k0 = pl.multiple_of(c * TK, TK)
            ind = jnp.where(pred_fn(score_ref[c], c), jnp.int32(1), jnp.int32(0))
            return cnt + jnp.sum(ind.reshape(TK // 8, 8, TQ), axis=0)
        cnt8 = lax.fori_loop(0, nkv, body, jnp.zeros((8, TQ), I32))
        return jnp.sum(cnt8, axis=0, keepdims=True)

    n_acc = 4

    def count_ge(cand):
        def body(c, accs):
            k0 = pl.multiple_of(c * TK, TK)
            accs = list(accs)
            for r in range(TK // 8):
                a = accs[r % n_acc]
                accs[r % n_acc] = jnp.where(score_ref[c, r * 8:(r + 1) * 8, :] >= cand, a + 1, a)
            return tuple(accs)
        accs = lax.fori_loop(0, nkv, body,
                             tuple(jnp.zeros((8, TQ), I32) for _ in range(n_acc)))
        return jnp.sum(functools.reduce(lambda a, b: a + b, accs), axis=0, keepdims=True)

    def value_pass(_, carry):
        lo, hi, cnt_lo = carry
        mid = _as_exact_f32(0.5 * lo + 0.5 * hi)
        cnt = count_ge(mid)
        inside = jnp.logical_and(mid > lo, mid < hi)
        up = jnp.logical_and(inside, cnt >= TOPK_MAX)
        dn = jnp.logical_and(inside, cnt < TOPK_MAX)
        return jnp.where(up, mid, lo), jnp.where(dn, mid, hi), jnp.where(up, cnt, cnt_lo)

    lo0 = _as_exact_f32(jnp.min(rng_ref[0], axis=0, keepdims=True))
    hi0 = _key_to_f32(_f32_to_key(jnp.max(rng_ref[1], axis=0, keepdims=True)) + 1)
    lo, hi, cnt_at = lax.fori_loop(0, VALUE_PASSES, value_pass, (lo0, hi0, n_adm))

    def unresolved(lo, hi, cnt_lo):
        gap = _f32_to_key(hi) - _f32_to_key(lo)
        open_ = jnp.logical_and(searching, jnp.logical_and(cnt_lo != TOPK_MAX, gap != 1))
        return jnp.where(open_, jnp.int32(1), jnp.int32(0))

    def key_cond(carry):
        return jnp.max(unresolved(*carry)) > 0

    def key_pass(carry):
        lo, hi, cnt_lo = carry
        klo = _f32_to_key(lo)
        kmid = klo + lax.shift_right_logical(_f32_to_key(hi) - klo, jnp.int32(1))
        mid = _key_to_f32(kmid)
        cnt = count_ge(mid)
        live = unresolved(lo, hi, cnt_lo) > 0
        up = jnp.logical_and(live, cnt >= TOPK_MAX)
        dn = jnp.logical_and(live, cnt < TOPK_MAX)
        return jnp.where(up, mid, lo), jnp.where(dn, mid, hi), jnp.where(up, cnt, cnt_lo)

    lo, hi, cnt_at = lax.while_loop(key_cond, key_pass, (lo, hi, cnt_at))
    tau = jnp.where(searching, lo, F32_LOWEST)
    cnt_at = jnp.where(searching, cnt_at, TOPK_MAX)

    @pl.when(jnp.max(cnt_at) > TOPK_MAX)
    def _():
        def pos_of(c):
            return lax.broadcasted_iota(I32, (TK, TQ), 0) + c * TK

        need = TOPK_MAX - count_where(lambda st, c: st > tau)
        n_bits = int(score_ref.shape[0] * TK).bit_length()

        def idx_body(t, bound):
            trial = bound | jnp.left_shift(jnp.int32(1), n_bits - 1 - t)
            cnt = count_where(
                lambda st, c: jnp.where(st == tau, pos_of(c), jnp.int32(1 << 30)) < trial)
            return jnp.where(cnt <= need, trial, bound)

        bound = lax.fori_loop(0, n_bits, idx_body, jnp.zeros((1, TQ), I32))
        below = _key_to_f32(_f32_to_key(tau) - 1)

        def demote(c, carry):
            k0 = pl.multiple_of(c * TK, TK)
            st = score_ref[c]
            drop = jnp.where(st == tau, pos_of(c), jnp.int32(-1)) >= bound
            score_ref[c] = jnp.where(drop, below, st)
            return carry

        lax.fori_loop(0, nkv, demote, 0)

    n_stage = A_HEADS
    dsa_grp = B_HEADS // n_stage
    a_dv = 2 * A_DH
    a_rows_v = a_dv + V_PAD

    def qk_diff(h, kstart):
        kh = akp_ref[0, pl.ds(kstart, TK), h * 256:(h + 1) * 256]
        return lax.dot_general(kh, qa_ref[h], _NT, preferred_element_type=F32)

    def qk_dsa(h, kstart):
        kb = bkp_ref[0, pl.ds(kstart, TK), :]
        qh = qb_ref[h * dsa_grp * TQ:(h + 1) * dsa_grp * TQ, :]
        return lax.dot_general(kb, qh, _NT, preferred_element_type=F32)

    def attend_block(j, diag):
        k0 = pl.multiple_of(j * TK, TK)
        kn0 = pl.multiple_of(jnp.minimum(j + 1, nkv - 1) * TK, TK)
        slot = lax.rem(j, 2)
        cap_sel = jnp.where(score_ref[j] >= tau, BIG, NEG)
        if diag:
            _, cap_chunk, corr = diag_geometry(j)
        vb = bv_ref[0, j]
        sd_next = sa_ref[slot]
        sb_next = sb_ref[slot]
        for h in range(n_stage):
            sd, sb = sd_next, sb_next
            if h + 1 < n_stage:
                sd_next = qk_diff(h + 1, k0)
                sb_next = qk_dsa(h + 1, k0)
            else:
                sa_ref[1 - slot] = qk_diff(0, kn0)
                sb_ref[1 - slot] = qk_dsa(0, kn0)

            prep_d = None
            if diag:
                prep_d = functools.partial(
                    lambda sl, g, s: jnp.minimum(s, cap_chunk) + sl * corr, slopes_a[h])
            vh = av_ref[0, j, h * a_rows_v:(h + 1) * a_rows_v, :]
            _online_update(sd, vh, acc_a_ref, m_a_ref, 2 * h, 2, prep_d)

            def prep_b(h0, g, s):
                s = jnp.minimum(s, cap_sel)
                return s + slopes_b[h0 + g] * corr if diag else s

            _online_update(sb, vb, acc_b_ref, m_b_ref, dsa_grp * h, dsa_grp,
                           functools.partial(prep_b, dsa_grp * h))

    def attend_body(diag, j, carry):
        attend_block(j, diag)
        return carry

    sa_ref[0] = qk_diff(0, 0)
    sb_ref[0] = qk_dsa(0, 0)
    lax.fori_loop(0, n_off, functools.partial(attend_body, False), 0)
    lax.fori_loop(n_off, nkv, functools.partial(attend_body, True), 0)

    lv = lamv_ref[...]
    lam = (jnp.exp(jnp.sum(lv[0:1] * lv[1:2], axis=1, keepdims=True))
           - jnp.exp(jnp.sum(lv[2:3] * lv[3:4], axis=1, keepdims=True)) + lam_init)
    g = g_ref[...]
    for h in range(A_HEADS):
        o1 = acc_a_ref[2 * h, 0:a_dv, :] / acc_a_ref[2 * h, a_dv:a_dv + 1, :]
        o2 = acc_a_ref[2 * h + 1, 0:a_dv, :] / acc_a_ref[2 * h + 1, a_dv:a_dv + 1, :]
        o = o1 - lam * o2
        y = o * lax.rsqrt(jnp.mean(o * o, axis=0, keepdims=True) + EPS)
        out_ref[0, 0, h * 128:(h + 1) * 128, :] = ((y * g) * (1.0 - lam_init)).astype(BF16)
    a_rows = A_HEADS * 2 * A_DH
    for h in range(B_HEADS):
        ob = acc_b_ref[h, 0:B_DH, :] / acc_b_ref[h, B_DH:B_DH + 1, :]
        out_ref[0, 0, a_rows + h * B_DH:a_rows + (h + 1) * B_DH, :] = ob.astype(BF16)


def _attention(lam_init, aq, bqp, iq, iwt, akp, avt, bkp, ikk, bvt, lamv, gt):
    bsz, s, a_cols = aq.shape
    nq = s // TQ
    nk = s // TK
    i_cols = iq.shape[2]
    d_mix = a_cols + B_HEADS * B_DH

    def per_batch(shape):
        return pl.BlockSpec(shape, lambda b, i: (b,) + (0,) * (len(shape) - 1))

    in_specs = [
        pl.BlockSpec((1, TQ, a_cols), lambda b, i: (b, i, 0)),
        pl.BlockSpec((1, TQ, bqp.shape[2]), lambda b, i: (b, i, 0)),
        pl.BlockSpec((1, TQ, i_cols), lambda b, i: (b, i, 0)),
        pl.BlockSpec((1, TQ // TK, IDX_HEADS, TK), lambda b, i: (b, i, 0, 0)),
        per_batch((1, s, akp.shape[2])),
        per_batch((1, nk, avt.shape[2], TK)),
        per_batch((1, s, 128)),
        per_batch((1, s, 128)),
        per_batch((1, nk, bvt.shape[2], TK)),
        _const_spec(lamv.shape),
        _const_spec(gt.shape),
    ]
    n_maps = 2 * A_HEADS
    scratch = [
        pltpu.VMEM((nk, TK, TQ), F32),
        pltpu.VMEM((2, 8, TQ), F32),
        pltpu.VMEM((2, TK, 2 * TQ), F32),
        pltpu.VMEM((2, TK, B_HEADS // A_HEADS * TQ), F32),
        pltpu.VMEM((A_HEADS, 2 * TQ, 256), BF16),
        pltpu.VMEM((A_HEADS, IDX_HEADS // A_HEADS * TQ, 128), BF16),
        pltpu.VMEM((B_HEADS * TQ, 128), BF16),
        pltpu.VMEM((IDX_HEADS, TQ), F32),
        pltpu.VMEM((n_maps, 2 * A_DH + V_PAD, TQ), F32),
        pltpu.VMEM((n_maps, 1, TQ), F32),
        pltpu.VMEM((B_HEADS, B_DH + V_PAD, TQ), F32),
        pltpu.VMEM((B_HEADS, 1, TQ), F32),
    ]
    return pl.pallas_call(
        functools.partial(_attn_kernel, lam_init),
        grid=(bsz, nq),
        in_specs=in_specs,
        out_specs=pl.BlockSpec((1, 1, d_mix, TQ), lambda b, i: (b, i, 0, 0)),
        out_shape=jax.ShapeDtypeStruct((bsz, nq, d_mix, TQ), BF16),
        scratch_shapes=scratch,
        compiler_params=pltpu.CompilerParams(
            dimension_semantics=("arbitrary", "arbitrary"), vmem_limit_bytes=VMEM_LIMIT_BYTES),
        name="attention",
    )(aq, bqp, iq, iwt, akp, avt, bkp, ikk, bvt, lamv, gt)


def _ffn_kernel(final, mix_ref, x_ref, mod_ref, n2_ref, fg_ref, wo_ref, w1_ref, w2_ref, o_ref):
    wo = wo_ref[...]
    y = jnp.concatenate(
        [lax.dot_general(mix_ref[0, c], wo, _TN, preferred_element_type=F32)
         for c in range(TM_FFN // TQ)], axis=0)
    g1 = mod_ref[0, 2:3, :]
    sh2 = mod_ref[0, 3:4, :]
    sc2 = mod_ref[0, 4:5, :]
    g2 = mod_ref[0, 5:6, :]
    x1 = x_ref[0] + g1 * y
    h2 = ((_rms(x1) * n2_ref[...]) * (1.0 + sc2) + sh2).astype(BF16)
    ff = jnp.zeros_like(x1)
    for c in range(w1_ref.shape[1] // FF_CHUNK):
        u = jnp.dot(h2, w1_ref[:, c * FF_CHUNK:(c + 1) * FF_CHUNK], preferred_element_type=F32)
        u = jnp.square(jnp.maximum(u, 0.0)).astype(BF16)
        ff = ff + jnp.dot(u, w2_ref[c * FF_CHUNK:(c + 1) * FF_CHUNK, :], preferred_element_type=F32)
    x2 = x1 + g2 * ff
    if final:
        x2 = _rms(x2) * fg_ref[...]
    o_ref[0] = x2


def _out_ffn(final, mixt, x, mod, n2g, fg, wo, w1, w2):
    bsz, s, d = x.shape
    d_mix = mixt.shape[2]
    tm = TM_FFN
    cpt = tm // TQ
    return pl.pallas_call(
        functools.partial(_ffn_kernel, final),
        grid=(bsz, s // tm),
        in_specs=[
            pl.BlockSpec((1, cpt, d_mix, TQ), lambda b, t: (b, t, 0, 0)),
            pl.BlockSpec((1, tm, d), lambda b, t: (b, t, 0)),
            pl.BlockSpec((1, 6, d), lambda b, t: (b, 0, 0)),
            _const_spec(n2g.shape),
            _const_spec(fg.shape),
            _const_spec(wo.shape),
            _const_spec(w1.shape),
            _const_spec(w2.shape),
        ],
        out_specs=pl.BlockSpec((1, tm, d), lambda b, t: (b, t, 0)),
        out_shape=jax.ShapeDtypeStruct((bsz, s, d), F32),
        compiler_params=pltpu.CompilerParams(
            dimension_semantics=("arbitrary", "arbitrary"), vmem_limit_bytes=VMEM_LIMIT_BYTES),
        name="out_ffn",
    )(mixt, x, mod, n2g, fg, wo, w1, w2)


def _split_w_in(w):
    d = w.shape[0]
    a = A_HEADS * 2 * A_DH
    sizes = (a, a, a, B_HEADS * B_DH, B_DH, B_DH, IDX_HEADS * IDX_DH, IDX_DH, IDX_HEADS)
    offs = np.cumsum((0,) + sizes)
    aq, ak, av, bq, bk, bv, iq, ik, iw = [w[:, offs[n]:offs[n + 1]] for n in range(len(sizes))]
    kk = jnp.concatenate([bk, jnp.zeros((d, 128 - B_DH), w.dtype), ik, ik], axis=1)
    nat = [aq, ak, bq, iq, kk]
    trn = [av.T, bv.T, iw.T]
    return [m.astype(BF16) for m in nat + trn]


def _dsa_slope_columns():
    row = np.zeros((1, B_HEADS * 128), np.float32)
    for h, sl in enumerate(_alibi_slopes(B_HEADS)):
        for k, v in enumerate(_slope_columns(sl)):
            row[0, h * 128 + B_DH + k] = v
    return jnp.asarray(row)


def kernel(x, c, norm1_g, norm2_g, w_ada, b_ada, w_in, lam_q1, lam_k1, lam_q2, lam_k2,
           subln_g, w_out, w_ff1, w_ff2, final_g):
    bsz, s, d = x.shape
    depth = w_in.shape[0]
    assert s % TS_PROJ == 0 and s % TM_FFN == 0 and s % TQ == 0 and TQ % TK == 0 and TM_FFN % TQ == 0
    assert min(TOPK_MAX, s // 4) == TOPK_MAX and s // POS_SPLIT <= 256 and CHUNK == 1 << CHUNK_SHIFT
    fg = final_g.reshape(1, d)
    bqc = _dsa_slope_columns()
    for l in range(depth):
        lam_init = 0.8 - 0.6 * math.exp(-0.3 * l)
        mod = _adaln_mod(c, w_ada[l].astype(BF16), b_ada[l].reshape(1, -1)).reshape(bsz, 6, d)
        aq, akp, bqp, iq, bkp, ikk, avt, bvt, iwt = _in_proj(
            x, mod, norm1_g[l].reshape(1, d), _split_w_in(w_in[l]), bqc)
        lamv = jnp.stack([lam_q1[l], lam_k1[l], lam_q2[l], lam_k2[l]])
        gt = jnp.broadcast_to(subln_g[l][:, None], (2 * A_DH, TQ))
        mixt = _attention(lam_init, aq, bqp, iq, iwt, akp, avt, bkp, ikk, bvt, lamv, gt)
        x = _out_ffn(l == depth - 1, mixt, x, mod, norm2_g[l].reshape(1, d), fg,
                     w_out[l].astype(BF16), w_ff1[l].astype(BF16), w_ff2[l].astype(BF16))
    return x
```

```python
import functools
import math

import jax
import jax.numpy as jnp
import numpy as np
from jax import lax
from jax.experimental import pallas as pl
from jax.experimental.pallas import tpu as pltpu

CHUNK = 64
CHUNK_SHIFT = 6
A_HEADS = 4
A_DH = 64
B_HEADS = 8
B_DH = 64
IDX_HEADS = 16
IDX_DH = 64
TOPK_MAX = 256
EPS = 1e-6

TQ = 256
TK = 256
TS_PROJ = 512
TM_FFN = 512
FF_CHUNK = 1024
VMEM_LIMIT_BYTES = 56 * 1024 * 1024
VALUE_PASSES = 18
V_PAD = 16
POS_SHIFT = 6
POS_SPLIT = 1 << POS_SHIFT

LOG2E = math.log2(math.e)
NEG = -1e30
BIG = 3e38
INT_MIN = -(2 ** 31)
F32_LOWEST = float(np.finfo(np.float32).min)
BF16 = jnp.bfloat16
F32 = jnp.float32
I32 = jnp.int32

_NT = (((1,), (1,)), ((), ()))
_TN = (((0,), (0,)), ((), ()))


def _alibi_slopes(n):
    return [2.0 ** (-8.0 * (i + 1) / n) for i in range(n)]


def _slope_columns(slope):
    pieces, rest = [], slope * LOG2E
    for _ in range(3):
        p = float(np.asarray(rest, np.float32).astype(BF16).astype(np.float32))
        pieces.append(p)
        rest -= p
    return [p * POS_SPLIT for p in pieces] + pieces


def _lane_constants(lane, l0, values):
    out = jnp.zeros(lane.shape, F32)
    for k, v in enumerate(values):
        out = jnp.where(lane == l0 + k, v, out)
    return out


def _const_spec(shape):
    nd = len(shape)
    return pl.BlockSpec(shape, lambda *_: (0,) * nd, pipeline_mode=pl.Buffered(1))


def _rms(x):
    return x * lax.rsqrt(jnp.mean(x * x, axis=-1, keepdims=True) + EPS)


def _mod_kernel(c_ref, w_ref, b_ref, o_ref):
    c = c_ref[...]
    cond = c * (1.0 / (1.0 + jnp.exp(-c)))
    o_ref[...] = jnp.dot(cond.astype(BF16), w_ref[...], preferred_element_type=F32) + b_ref[...]


def _adaln_mod(c, w_ada, b_ada):
    bsz, d = c.shape
    n = w_ada.shape[1]
    return pl.pallas_call(
        _mod_kernel,
        grid=(n // d,),
        in_specs=[
            pl.BlockSpec((bsz, d), lambda j: (0, 0)),
            pl.BlockSpec((d, d), lambda j: (0, j)),
            pl.BlockSpec((1, d), lambda j: (0, j)),
        ],
        out_specs=pl.BlockSpec((bsz, d), lambda j: (0, j)),
        out_shape=jax.ShapeDtypeStruct((bsz, n), F32),
        compiler_params=pltpu.CompilerParams(dimension_semantics=("arbitrary",)),
        name="adaln_mod",
    )(c, w_ada, b_ada)


def _inproj_kernel(x_ref, mod_ref, g_ref, waq_ref, wak_ref, wbq_ref, wiq_ref, wkk_ref,
                   wav_ref, wbv_ref, wiw_ref, bqc_ref,
                   aq_ref, akp_ref, bqp_ref, iq_ref, bkp_ref, ikk_ref, av_ref, bv_ref, iw_ref):
    x = x_ref[0]
    sh1 = mod_ref[0, 0:1, :]
    sc1 = mod_ref[0, 1:2, :]
    h = (_rms(x) * g_ref[...]) * (1.0 + sc1) + sh1
    hb = h.astype(BF16)

    def nat(w_ref):
        return jnp.dot(hb, w_ref[...], preferred_element_type=F32)

    pos = pl.program_id(1) * TS_PROJ + lax.broadcasted_iota(I32, (TS_PROJ, 128), 0)
    lane = lax.broadcasted_iota(I32, (TS_PROJ, 128), 1)
    pos_hi = (pos >> POS_SHIFT).astype(F32)
    pos_lo = (pos & (POS_SPLIT - 1)).astype(F32)

    def pos_tile(l0):
        return jnp.where((lane >= l0) & (lane < l0 + 3), pos_hi,
                         jnp.where((lane >= l0 + 3) & (lane < l0 + 6), pos_lo, 0.0))

    aq_ref[0] = (nat(waq_ref) * (A_DH ** -0.5 * LOG2E)).astype(BF16)
    ak = nat(wak_ref)
    pos0 = pos_tile(0).astype(BF16)
    for hd in range(A_HEADS):
        akp_ref[0, :, hd * 256:hd * 256 + 128] = ak[:, hd * 128:(hd + 1) * 128].astype(BF16)
        akp_ref[0, :, hd * 256 + 128:(hd + 1) * 256] = pos0
    bq = nat(wbq_ref) * (B_DH ** -0.5 * LOG2E)
    for t in range(B_HEADS // 2):
        tile = bq[:, t * 128:(t + 1) * 128]
        for half, src in enumerate((tile, pltpu.roll(tile, 64, 1))):
            hd = 2 * t + half
            slab = jnp.where(lane < B_DH, src, bqc_ref[:, hd * 128:(hd + 1) * 128])
            bqp_ref[0, :, hd * 128:(hd + 1) * 128] = slab.astype(BF16)
    iq_ref[0] = nat(wiq_ref).astype(BF16)
    kk = nat(wkk_ref)
    bkp_ref[0] = (kk[:, 0:128] + pos_tile(B_DH)).astype(BF16)
    ikk_ref[0] = kk[:, 128:256].astype(BF16)

    def tr(w_ref):
        return lax.dot_general(w_ref[...], hb, _NT, preferred_element_type=F32)

    avt = tr(wav_ref).astype(BF16)
    bvt = tr(wbv_ref).astype(BF16)
    iwt = tr(wiw_ref)
    ones_rows = jnp.where(lax.broadcasted_iota(I32, (V_PAD, TK), 0) == 0, 1.0, 0.0).astype(BF16)
    a_dv = 2 * A_DH
    for c in range(TS_PROJ // TK):
        for hd in range(A_HEADS):
            r0 = hd * (a_dv + V_PAD)
            av_ref[0, c, r0:r0 + a_dv, :] = avt[hd * a_dv:(hd + 1) * a_dv, c * TK:(c + 1) * TK]
            av_ref[0, c, r0 + a_dv:r0 + a_dv + V_PAD, :] = ones_rows
        bv_ref[0, c, 0:B_DH, :] = bvt[:, c * TK:(c + 1) * TK]
        bv_ref[0, c, B_DH:B_DH + V_PAD, :] = ones_rows
        iw_ref[0, c] = iwt[:, c * TK:(c + 1) * TK]


def _in_proj(x, mod, g, ws, bqc):
    bsz, s, d = x.shape
    ts = TS_PROJ
    nk = s // TK
    cpt = ts // TK

    def tok(cols):
        return pl.BlockSpec((1, ts, cols), lambda b, t: (b, t, 0))

    def trs(rows):
        return pl.BlockSpec((1, cpt, rows, TK), lambda b, t: (b, t, 0, 0))

    a_cols = A_HEADS * 2 * A_DH
    i_cols = IDX_HEADS * IDX_DH
    out_shape = [
        jax.ShapeDtypeStruct((bsz, s, a_cols), BF16),
        jax.ShapeDtypeStruct((bsz, s, A_HEADS * 256), BF16),
        jax.ShapeDtypeStruct((bsz, s, B_HEADS * 128), BF16),
        jax.ShapeDtypeStruct((bsz, s, i_cols), BF16),
        jax.ShapeDtypeStruct((bsz, s, 128), BF16),
        jax.ShapeDtypeStruct((bsz, s, 128), BF16),
        jax.ShapeDtypeStruct((bsz, nk, A_HEADS * (2 * A_DH + V_PAD), TK), BF16),
        jax.ShapeDtypeStruct((bsz, nk, B_DH + V_PAD, TK), BF16),
        jax.ShapeDtypeStruct((bsz, nk, IDX_HEADS, TK), F32),
    ]
    out_specs = [tok(a_cols), tok(A_HEADS * 256), tok(B_HEADS * 128), tok(i_cols), tok(128),
                 tok(128), trs(A_HEADS * (2 * A_DH + V_PAD)), trs(B_DH + V_PAD), trs(IDX_HEADS)]
    in_specs = [
        pl.BlockSpec((1, ts, d), lambda b, t: (b, t, 0)),
        pl.BlockSpec((1, 6, d), lambda b, t: (b, 0, 0)),
        _const_spec(g.shape),
    ] + [_const_spec(w.shape) for w in ws] + [_const_spec(bqc.shape)]
    return pl.pallas_call(
        _inproj_kernel,
        grid=(bsz, s // ts),
        in_specs=in_specs,
        out_specs=out_specs,
        out_shape=out_shape,
        compiler_params=pltpu.CompilerParams(
            dimension_semantics=("arbitrary", "arbitrary"), vmem_limit_bytes=VMEM_LIMIT_BYTES),
        name="in_proj",
    )(x, mod, g, *ws, bqc)


def _online_update(s_all, v, acc_ref, m_ref, idx0, n, prep=None):
    for g in range(n):
        s = s_all[:, g * TQ:(g + 1) * TQ]
        if prep is not None:
            s = prep(g, s)
        m_prev = m_ref[idx0 + g]
        m_new = jnp.maximum(m_prev, jnp.max(s, axis=0, keepdims=True))
        alpha = jnp.exp2(m_prev - m_new)
        p = jnp.exp2(s - m_new).astype(BF16)
        m_ref[idx0 + g] = m_new
        pv = jnp.dot(v, p, preferred_element_type=F32)
        acc_ref[idx0 + g] = alpha * acc_ref[idx0 + g] + pv


def _key_to_f32(key):
    return pltpu.bitcast(key ^ ((key >> 31) & jnp.int32(0x7FFFFFFF)), F32)


def _f32_to_key(x):
    bits = pltpu.bitcast(x, I32)
    return bits ^ ((bits >> 31) & jnp.int32(0x7FFFFFFF))


def _as_exact_f32(x):
    return _key_to_f32(_f32_to_key(x))


def _attn_kernel(lam_init, aq_ref, bqp_ref, iq_ref, iw_ref, akp_ref, av_ref, bkp_ref, ikk_ref,
                 bv_ref, lamv_ref, g_ref, out_ref,
                 score_ref, rng_ref, sa_ref, sb_ref, qa_ref, qi_ref, qb_ref, w_ref,
                 acc_a_ref, m_a_ref, acc_b_ref, m_b_ref):
    i = pl.program_id(1)
    n_off = i * (TQ // TK)
    nkv = n_off + TQ // TK
    slopes_a = _alibi_slopes(A_HEADS)
    slopes_b = _alibi_slopes(B_HEADS)
    n_maps = 2 * A_HEADS

    lane = lax.broadcasted_iota(I32, (TQ, 128), 1)
    keep_lo = jnp.where(lane < 64, 1.0, 0.0).astype(BF16)
    keep_hi = jnp.where(lane < 64, 0.0, 1.0).astype(BF16)
    idx_grp = IDX_HEADS // A_HEADS
    for h in range(A_HEADS):
        qh = aq_ref[0, :, h * 128:(h + 1) * 128]
        slope_cols = _lane_constants(lane, 0, _slope_columns(slopes_a[h])).astype(BF16)
        qa_ref[h, 0:TQ, 0:128] = qh * keep_lo
        qa_ref[h, TQ:2 * TQ, 0:128] = qh * keep_hi
        qa_ref[h, 0:TQ, 128:256] = slope_cols
        qa_ref[h, TQ:2 * TQ, 128:256] = slope_cols
    for t in range(IDX_HEADS // 2):
        qt = iq_ref[0, :, t * 128:(t + 1) * 128]
        for half, keep in enumerate((keep_lo, keep_hi)):
            g, k = divmod(2 * t + half, idx_grp)
            qi_ref[g, k * TQ:(k + 1) * TQ, :] = qt * keep
    for h in range(B_HEADS):
        qb_ref[h * TQ:(h + 1) * TQ, :] = bqp_ref[0, :, h * 128:(h + 1) * 128]
    for c in range(TQ // TK):
        w_ref[:, c * TK:(c + 1) * TK] = iw_ref[0, c] * ((IDX_HEADS * IDX_DH) ** -0.5)

    acc_a_ref[...] = jnp.zeros_like(acc_a_ref)
    acc_b_ref[...] = jnp.zeros_like(acc_b_ref)
    m_a_ref[...] = jnp.full_like(m_a_ref, NEG)
    m_b_ref[...] = jnp.full_like(m_b_ref, NEG)

    def diag_geometry(j):
        row = j * TK + lax.broadcasted_iota(I32, (TK, TQ), 0)
        col = i * TQ + lax.broadcasted_iota(I32, (TK, TQ), 1)
        allowed = row < ((col >> CHUNK_SHIFT) + 1) * CHUNK
        corr = (2.0 * LOG2E) * jnp.minimum((col - row).astype(F32), 0.0)
        return allowed, jnp.where(allowed, BIG, NEG), corr

    def index_block(j, diag):
        k0 = pl.multiple_of(j * TK, TK)
        kk = ikk_ref[0, pl.ds(k0, TK), :]
        score = jnp.zeros((TK, TQ), F32)
        for g in range(IDX_HEADS // idx_grp):
            xg = lax.dot_general(kk, qi_ref[g], _NT, preferred_element_type=F32)
            for k in range(idx_grp):
                hh = g * idx_grp + k
                score = score + w_ref[hh:hh + 1, :] * jnp.maximum(xg[:, k * TQ:(k + 1) * TQ], 0.0)
        lo_src = score
        if diag:
            allowed, _, _ = diag_geometry(j)
            lo_src = jnp.where(allowed, score, jnp.inf)
            score = jnp.where(allowed, score, -jnp.inf)
        score_ref[j] = score
        rng_ref[0] = jnp.minimum(rng_ref[0], jnp.min(lo_src.reshape(TK // 8, 8, TQ), axis=0))
        rng_ref[1] = jnp.maximum(rng_ref[1], jnp.max(score.reshape(TK // 8, 8, TQ), axis=0))

    def index_body(diag, j, carry):
        index_block(j, diag)
        return carry

    rng_ref[0] = jnp.full((8, TQ), jnp.inf, F32)
    rng_ref[1] = jnp.full((8, TQ), -jnp.inf, F32)
    def index_pair_body(jj, carry):
        index_block(2 * jj, False)
        index_block(2 * jj + 1, False)
        return carry

    lax.fori_loop(0, n_off // 2, index_pair_body, 0)
    lax.fori_loop(n_off // 2 * 2, n_off, functools.partial(index_body, False), 0)
    lax.fori_loop(n_off, nkv, functools.partial(index_body, True), 0)

    qpos = i * TQ + lax.broadcasted_iota(I32, (1, TQ), 1)
    n_adm = ((qpos >> CHUNK_SHIFT) + 1) * CHUNK
    searching = n_adm > TOPK_MAX

    def count_where(pred_fn):
        def body(c, cnt):
            k0 = pl.multiple_of(c * TK, TK)
            ind = jnp.where(pred_fn(score_ref[c], c), jnp.int32(1), jnp.int32(0))
            return cnt + jnp.sum(ind.reshape(TK // 8, 8, TQ), axis=0)
        cnt8 = lax.fori_loop(0, nkv, body, jnp.zeros((8, TQ), I32))
        return jnp.sum(cnt8, axis=0, keepdims=True)

    n_acc = 4

    def count_ge(cand):
        def body(c, accs):
            k0 = pl.multiple_of(c * TK, TK)
            accs = list(accs)
            for r in range(TK // 8):
                a = accs[r % n_acc]
                accs[r % n_acc] = jnp.where(score_ref[c, r * 8:(r + 1) * 8, :] >= cand, a + 1, a)
            return tuple(accs)
        accs = lax.fori_loop(0, nkv, body,
                             tuple(jnp.zeros((8, TQ), I32) for _ in range(n_acc)))
        return jnp.sum(functools.reduce(lambda a, b: a + b, accs), axis=0, keepdims=True)

    def value_pass(_, carry):
        lo, hi, cnt_lo = carry
        mid = _as_exact_f32(0.5 * lo + 0.5 * hi)
        cnt = count_ge(mid)
        inside = jnp.logical_and(mid > lo, mid < hi)
        up = jnp.logical_and(inside, cnt >= TOPK_MAX)
        dn = jnp.logical_and(inside, cnt < TOPK_MAX)
        return jnp.where(up, mid, lo), jnp.where(dn, mid, hi), jnp.where(up, cnt, cnt_lo)

    lo0 = _as_exact_f32(jnp.min(rng_ref[0], axis=0, keepdims=True))
    hi0 = _key_to_f32(_f32_to_key(jnp.max(rng_ref[1], axis=0, keepdims=True)) + 1)
    lo, hi, cnt_at = lax.fori_loop(0, VALUE_PASSES, value_pass, (lo0, hi0, n_adm))

    def unresolved(lo, hi, cnt_lo):
        gap = _f32_to_key(hi) - _f32_to_key(lo)
        open_ = jnp.logical_and(searching, jnp.logical_and(cnt_lo != TOPK_MAX, gap != 1))
        return jnp.where(open_, jnp.int32(1), jnp.int32(0))

    def key_cond(carry):
        return jnp.max(carry[3]) > 0

    def key_pass(carry):
        lo, hi, cnt_lo, open_ = carry
        klo = _f32_to_key(lo)
        kmid = klo + lax.shift_right_logical(_f32_to_key(hi) - klo, jnp.int32(1))
        mid = _key_to_f32(kmid)
        cnt = count_ge(mid)
        up = jnp.logical_and(open_ > 0, cnt >= TOPK_MAX)
        dn = jnp.logical_and(open_ > 0, cnt < TOPK_MAX)
        lo, hi, cnt_lo = jnp.where(up, mid, lo), jnp.where(dn, mid, hi), jnp.where(up, cnt, cnt_lo)
        return lo, hi, cnt_lo, unresolved(lo, hi, cnt_lo)

    lo, hi, cnt_at, _ = lax.while_loop(
        key_cond, key_pass, (lo, hi, cnt_at, unresolved(lo, hi, cnt_at)))
    tau = jnp.where(searching, lo, F32_LOWEST)
    cnt_at = jnp.where(searching, cnt_at, TOPK_MAX)

    @pl.when(jnp.max(cnt_at) > TOPK_MAX)
    def _():
        def pos_of(c):
            return lax.broadcasted_iota(I32, (TK, TQ), 0) + c * TK

        need = TOPK_MAX - count_where(lambda st, c: st > tau)
        n_bits = int(score_ref.shape[0] * TK).bit_length()

        def idx_body(t, bound):
            trial = bound | jnp.left_shift(jnp.int32(1), n_bits - 1 - t)
            cnt = count_where(
                lambda st, c: jnp.where(st == tau, pos_of(c), jnp.int32(1 << 30)) < trial)
            return jnp.where(cnt <= need, trial, bound)

        bound = lax.fori_loop(0, n_bits, idx_body, jnp.zeros((1, TQ), I32))
        below = _key_to_f32(_f32_to_key(tau) - 1)

        def demote(c, carry):
            k0 = pl.multiple_of(c * TK, TK)
            st = score_ref[c]
            drop = jnp.where(st == tau, pos_of(c), jnp.int32(-1)) >= bound
            score_ref[c] = jnp.where(drop, below, st)
            return carry

        lax.fori_loop(0, nkv, demote, 0)

    n_stage = A_HEADS
    dsa_grp = B_HEADS // n_stage
    a_dv = 2 * A_DH
    a_rows_v = a_dv + V_PAD

    def qk_diff(h, kstart):
        kh = akp_ref[0, pl.ds(kstart, TK), h * 256:(h + 1) * 256]
        return lax.dot_general(kh, qa_ref[h], _NT, preferred_element_type=F32)

    def qk_dsa(h, kstart):
        kb = bkp_ref[0, pl.ds(kstart, TK), :]
        qh = qb_ref[h * dsa_grp * TQ:(h + 1) * dsa_grp * TQ, :]
        return lax.dot_general(kb, qh, _NT, preferred_element_type=F32)

    def attend_block(j, diag):
        k0 = pl.multiple_of(j * TK, TK)
        kn0 = pl.multiple_of(jnp.minimum(j + 1, nkv - 1) * TK, TK)
        slot = lax.rem(j, 2)
        cap_sel = jnp.where(score_ref[j] >= tau, BIG, NEG)
        if diag:
            _, cap_chunk, corr = diag_geometry(j)
        vb = bv_ref[0, j]
        sd_next = sa_ref[slot]
        sb_next = sb_ref[slot]
        for h in range(n_stage):
            sd, sb = sd_next, sb_next
            if h + 1 < n_stage:
                sd_next = qk_diff(h + 1, k0)
                sb_next = qk_dsa(h + 1, k0)
            else:
                sa_ref[1 - slot] = qk_diff(0, kn0)
                sb_ref[1 - slot] = qk_dsa(0, kn0)

            prep_d = None
            if diag:
                prep_d = functools.partial(
                    lambda sl, g, s: jnp.minimum(s, cap_chunk) + sl * corr, slopes_a[h])
            vh = av_ref[0, j, h * a_rows_v:(h + 1) * a_rows_v, :]
            _online_update(sd, vh, acc_a_ref, m_a_ref, 2 * h, 2, prep_d)

            def prep_b(h0, g, s):
                s = jnp.minimum(s, cap_sel)
                return s + slopes_b[h0 + g] * corr if diag else s

            _online_update(sb, vb, acc_b_ref, m_b_ref, dsa_grp * h, dsa_grp,
                           functools.partial(prep_b, dsa_grp * h))

    def attend_body(diag, j, carry):
        attend_block(j, diag)
        return carry

    sa_ref[0] = qk_diff(0, 0)
    sb_ref[0] = qk_dsa(0, 0)
    lax.fori_loop(0, n_off, functools.partial(attend_body, False), 0)
    lax.fori_loop(n_off, nkv, functools.partial(attend_body, True), 0)

    lv = lamv_ref[...]
    lam = (jnp.exp(jnp.sum(lv[0:1] * lv[1:2], axis=1, keepdims=True))
           - jnp.exp(jnp.sum(lv[2:3] * lv[3:4], axis=1, keepdims=True)) + lam_init)
    g = g_ref[...]
    for h in range(A_HEADS):
        o1 = acc_a_ref[2 * h, 0:a_dv, :] / acc_a_ref[2 * h, a_dv:a_dv + 1, :]
        o2 = acc_a_ref[2 * h + 1, 0:a_dv, :] / acc_a_ref[2 * h + 1, a_dv:a_dv + 1, :]
        o = o1 - lam * o2
        y = o * lax.rsqrt(jnp.mean(o * o, axis=0, keepdims=True) + EPS)
        out_ref[0, 0, h * 128:(h + 1) * 128, :] = ((y * g) * (1.0 - lam_init)).astype(BF16)
    a_rows = A_HEADS * 2 * A_DH
    for h in range(B_HEADS):
        ob = acc_b_ref[h, 0:B_DH, :] / acc_b_ref[h, B_DH:B_DH + 1, :]
        out_ref[0, 0, a_rows + h * B_DH:a_rows + (h + 1) * B_DH, :] = ob.astype(BF16)


def _attention(lam_init, aq, bqp, iq, iwt, akp, avt, bkp, ikk, bvt, lamv, gt):
    bsz, s, a_cols = aq.shape
    nq = s // TQ
    nk = s // TK
    i_cols = iq.shape[2]
    d_mix = a_cols + B_HEADS * B_DH

    def per_batch(shape):
        return pl.BlockSpec(shape, lambda b, i: (b,) + (0,) * (len(shape) - 1))

    in_specs = [
        pl.BlockSpec((1, TQ, a_cols), lambda b, i: (b, i, 0)),
        pl.BlockSpec((1, TQ, bqp.shape[2]), lambda b, i: (b, i, 0)),
        pl.BlockSpec((1, TQ, i_cols), lambda b, i: (b, i, 0)),
        pl.BlockSpec((1, TQ // TK, IDX_HEADS, TK), lambda b, i: (b, i, 0, 0)),
        per_batch((1, s, akp.shape[2])),
        per_batch((1, nk, avt.shape[2], TK)),
        per_batch((1, s, 128)),
        per_batch((1, s, 128)),
        per_batch((1, nk, bvt.shape[2], TK)),
        _const_spec(lamv.shape),
        _const_spec(gt.shape),
    ]
    n_maps = 2 * A_HEADS
    scratch = [
        pltpu.VMEM((nk, TK, TQ), F32),
        pltpu.VMEM((2, 8, TQ), F32),
        pltpu.VMEM((2, TK, 2 * TQ), F32),
        pltpu.VMEM((2, TK, B_HEADS // A_HEADS * TQ), F32),
        pltpu.VMEM((A_HEADS, 2 * TQ, 256), BF16),
        pltpu.VMEM((A_HEADS, IDX_HEADS // A_HEADS * TQ, 128), BF16),
        pltpu.VMEM((B_HEADS * TQ, 128), BF16),
        pltpu.VMEM((IDX_HEADS, TQ), F32),
        pltpu.VMEM((n_maps, 2 * A_DH + V_PAD, TQ), F32),
        pltpu.VMEM((n_maps, 1, TQ), F32),
        pltpu.VMEM((B_HEADS, B_DH + V_PAD, TQ), F32),
        pltpu.VMEM((B_HEADS, 1, TQ), F32),
    ]
    return pl.pallas_call(
        functools.partial(_attn_kernel, lam_init),
        grid=(bsz, nq),
        in_specs=in_specs,
        out_specs=pl.BlockSpec((1, 1, d_mix, TQ), lambda b, i: (b, i, 0, 0)),
        out_shape=jax.ShapeDtypeStruct((bsz, nq, d_mix, TQ), BF16),
        scratch_shapes=scratch,
        compiler_params=pltpu.CompilerParams(
            dimension_semantics=("arbitrary", "arbitrary"), vmem_limit_bytes=VMEM_LIMIT_BYTES),
        name="attention",
    )(aq, bqp, iq, iwt, akp, avt, bkp, ikk, bvt, lamv, gt)


def _ffn_kernel(final, mix_ref, x_ref, mod_ref, n2_ref, fg_ref, wo_ref, w1_ref, w2_ref, o_ref):
    wo = wo_ref[...]
    y = jnp.concatenate(
        [lax.dot_general(mix_ref[0, c], wo, _TN, preferred_element_type=F32)
         for c in range(TM_FFN // TQ)], axis=0)
    g1 = mod_ref[0, 2:3, :]
    sh2 = mod_ref[0, 3:4, :]
    sc2 = mod_ref[0, 4:5, :]
    g2 = mod_ref[0, 5:6, :]
    x1 = x_ref[0] + g1 * y
    h2 = ((_rms(x1) * n2_ref[...]) * (1.0 + sc2) + sh2).astype(BF16)
    ff = jnp.zeros_like(x1)
    for c in range(w1_ref.shape[1] // FF_CHUNK):
        u = jnp.dot(h2, w1_ref[:, c * FF_CHUNK:(c + 1) * FF_CHUNK], preferred_element_type=F32)
        u = jnp.square(jnp.maximum(u, 0.0)).astype(BF16)
        ff = ff + jnp.dot(u, w2_ref[c * FF_CHUNK:(c + 1) * FF_CHUNK, :], preferred_element_type=F32)
    x2 = x1 + g2 * ff
    if final:
        x2 = _rms(x2) * fg_ref[...]
    o_ref[0] = x2


def _out_ffn(final, mixt, x, mod, n2g, fg, wo, w1, w2):
    bsz, s, d = x.shape
    d_mix = mixt.shape[2]
    tm = TM_FFN
    cpt = tm // TQ
    return pl.pallas_call(
        functools.partial(_ffn_kernel, final),
        grid=(bsz, s // tm),
        in_specs=[
            pl.BlockSpec((1, cpt, d_mix, TQ), lambda b, t: (b, t, 0, 0)),
            pl.BlockSpec((1, tm, d), lambda b, t: (b, t, 0)),
            pl.BlockSpec((1, 6, d), lambda b, t: (b, 0, 0)),
            _const_spec(n2g.shape),
            _const_spec(fg.shape),
            _const_spec(wo.shape),
            _const_spec(w1.shape),
            _const_spec(w2.shape),
        ],
        out_specs=pl.BlockSpec((1, tm, d), lambda b, t: (b, t, 0)),
        out_shape=jax.ShapeDtypeStruct((bsz, s, d), F32),
        compiler_params=pltpu.CompilerParams(
            dimension_semantics=("arbitrary", "arbitrary"), vmem_limit_bytes=VMEM_LIMIT_BYTES),
        name="out_ffn",
    )(mixt, x, mod, n2g, fg, wo, w1, w2)


def _split_w_in(w):
    d = w.shape[0]
    a = A_HEADS * 2 * A_DH
    sizes = (a, a, a, B_HEADS * B_DH, B_DH, B_DH, IDX_HEADS * IDX_DH, IDX_DH, IDX_HEADS)
    offs = np.cumsum((0,) + sizes)
    aq, ak, av, bq, bk, bv, iq, ik, iw = [w[:, offs[n]:offs[n + 1]] for n in range(len(sizes))]
    kk = jnp.concatenate([bk, jnp.zeros((d, 128 - B_DH), w.dtype), ik, ik], axis=1)
    nat = [aq, ak, bq, iq, kk]
    trn = [av.T, bv.T, iw.T]
    return [m.astype(BF16) for m in nat + trn]


def _dsa_slope_columns():
    row = np.zeros((1, B_HEADS * 128), np.float32)
    for h, sl in enumerate(_alibi_slopes(B_HEADS)):
        for k, v in enumerate(_slope_columns(sl)):
            row[0, h * 128 + B_DH + k] = v
    return jnp.asarray(row)


def kernel(x, c, norm1_g, norm2_g, w_ada, b_ada, w_in, lam_q1, lam_k1, lam_q2, lam_k2,
           subln_g, w_out, w_ff1, w_ff2, final_g):
    bsz, s, d = x.shape
    depth = w_in.shape[0]
    assert s % TS_PROJ == 0 and s % TM_FFN == 0 and s % TQ == 0 and TQ % TK == 0 and TM_FFN % TQ == 0
    assert min(TOPK_MAX, s // 4) == TOPK_MAX and s // POS_SPLIT <= 256 and CHUNK == 1 << CHUNK_SHIFT
    fg = final_g.reshape(1, d)
    bqc = _dsa_slope_columns()
    for l in range(depth):
        lam_init = 0.8 - 0.6 * math.exp(-0.3 * l)
        mod = _adaln_mod(c, w_ada[l].astype(BF16), b_ada[l].reshape(1, -1)).reshape(bsz, 6, d)
        aq, akp, bqp, iq, bkp, ikk, avt, bvt, iwt = _in_proj(
            x, mod, norm1_g[l].reshape(1, d), _split_w_in(w_in[l]), bqc)
        lamv = jnp.stack([lam_q1[l], lam_k1[l], lam_q2[l], lam_k2[l]])
        gt = jnp.broadcast_to(subln_g[l][:, None], (2 * A_DH, TQ))
        mixt = _attention(lam_init, aq, bqp, iq, iwt, akp, avt, bkp, ikk, bvt, lamv, gt)
        x = _out_ffn(l == depth - 1, mixt, x, mod, norm2_g[l].reshape(1, d), fg,
                     w_out[l].astype(BF16), w_ff1[l].astype(BF16), w_ff2[l].astype(BF16))
    return x
```

```python
import functools
import math

import jax
import jax.numpy as jnp
import numpy as np
from jax import lax
from jax.experimental import pallas as pl
from jax.experimental.pallas import tpu as pltpu

CHUNK = 64
CHUNK_SHIFT = 6
A_HEADS = 4
A_DH = 64
B_HEADS = 8
B_DH = 64
IDX_HEADS = 16
IDX_DH = 64
TOPK_MAX = 256
EPS = 1e-6

TQ = 256
TK = 256
TS_PROJ = 512
TM_FFN = 512
FF_CHUNK = 1024
VMEM_LIMIT_BYTES = 56 * 1024 * 1024
VALUE_PASSES = 18
V_PAD = 16
POS_SHIFT = 6
POS_SPLIT = 1 << POS_SHIFT

LOG2E = math.log2(math.e)
NEG = -1e30
BIG = 3e38
INT_MIN = -(2 ** 31)
F32_LOWEST = float(np.finfo(np.float32).min)
BF16 = jnp.bfloat16
F32 = jnp.float32
I32 = jnp.int32

_NT = (((1,), (1,)), ((), ()))
_TN = (((0,), (0,)), ((), ()))


def _alibi_slopes(n):
    return [2.0 ** (-8.0 * (i + 1) / n) for i in range(n)]


def _slope_columns(slope):
    pieces, rest = [], slope * LOG2E
    for _ in range(3):
        p = float(np.asarray(rest, np.float32).astype(BF16).astype(np.float32))
        pieces.append(p)
        rest -= p
    return [p * POS_SPLIT for p in pieces] + pieces


def _lane_constants(lane, l0, values):
    out = jnp.zeros(lane.shape, F32)
    for k, v in enumerate(values):
        out = jnp.where(lane == l0 + k, v, out)
    return out


def _const_spec(shape):
    nd = len(shape)
    return pl.BlockSpec(shape, lambda *_: (0,) * nd, pipeline_mode=pl.Buffered(1))


def _rms(x):
    return x * lax.rsqrt(jnp.mean(x * x, axis=-1, keepdims=True) + EPS)


def _mod_kernel(c_ref, w_ref, b_ref, o_ref):
    c = c_ref[...]
    cond = c * (1.0 / (1.0 + jnp.exp(-c)))
    o_ref[...] = jnp.dot(cond.astype(BF16), w_ref[...], preferred_element_type=F32) + b_ref[...]


def _adaln_mod(c, w_ada, b_ada):
    bsz, d = c.shape
    n = w_ada.shape[1]
    return pl.pallas_call(
        _mod_kernel,
        grid=(n // d,),
        in_specs=[
            pl.BlockSpec((bsz, d), lambda j: (0, 0)),
            pl.BlockSpec((d, d), lambda j: (0, j)),
            pl.BlockSpec((1, d), lambda j: (0, j)),
        ],
        out_specs=pl.BlockSpec((bsz, d), lambda j: (0, j)),
        out_shape=jax.ShapeDtypeStruct((bsz, n), F32),
        compiler_params=pltpu.CompilerParams(dimension_semantics=("arbitrary",)),
        name="adaln_mod",
    )(c, w_ada, b_ada)


def _inproj_kernel(x_ref, mod_ref, g_ref, waq_ref, wak_ref, wbq_ref, wiq_ref, wkk_ref,
                   wav_ref, wbv_ref, wiw_ref, bqc_ref,
                   aq_ref, akp_ref, bqp_ref, iq_ref, bkp_ref, ikk_ref, av_ref, bv_ref, iw_ref):
    x = x_ref[0]
    sh1 = mod_ref[0, 0:1, :]
    sc1 = mod_ref[0, 1:2, :]
    h = (_rms(x) * g_ref[...]) * (1.0 + sc1) + sh1
    hb = h.astype(BF16)

    def nat(w_ref):
        return jnp.dot(hb, w_ref[...], preferred_element_type=F32)

    pos = pl.program_id(1) * TS_PROJ + lax.broadcasted_iota(I32, (TS_PROJ, 128), 0)
    lane = lax.broadcasted_iota(I32, (TS_PROJ, 128), 1)
    pos_hi = (pos >> POS_SHIFT).astype(F32)
    pos_lo = (pos & (POS_SPLIT - 1)).astype(F32)

    def pos_tile(l0):
        return jnp.where((lane >= l0) & (lane < l0 + 3), pos_hi,
                         jnp.where((lane >= l0 + 3) & (lane < l0 + 6), pos_lo, 0.0))

    aq_ref[0] = (nat(waq_ref) * (A_DH ** -0.5 * LOG2E)).astype(BF16)
    ak = nat(wak_ref)
    pos0 = pos_tile(0).astype(BF16)
    for hd in range(A_HEADS):
        akp_ref[0, :, hd * 256:hd * 256 + 128] = ak[:, hd * 128:(hd + 1) * 128].astype(BF16)
        akp_ref[0, :, hd * 256 + 128:(hd + 1) * 256] = pos0
    bq = nat(wbq_ref) * (B_DH ** -0.5 * LOG2E)
    for t in range(B_HEADS // 2):
        tile = bq[:, t * 128:(t + 1) * 128]
        for half, src in enumerate((tile, pltpu.roll(tile, 64, 1))):
            hd = 2 * t + half
            slab = jnp.where(lane < B_DH, src, bqc_ref[:, hd * 128:(hd + 1) * 128])
            bqp_ref[0, :, hd * 128:(hd + 1) * 128] = slab.astype(BF16)
    iq_ref[0] = nat(wiq_ref).astype(BF16)
    kk = nat(wkk_ref)
    bkp_ref[0] = (kk[:, 0:128] + pos_tile(B_DH)).astype(BF16)
    ikk_ref[0] = kk[:, 128:256].astype(BF16)

    def tr(w_ref):
        return lax.dot_general(w_ref[...], hb, _NT, preferred_element_type=F32)

    avt = tr(wav_ref).astype(BF16)
    bvt = tr(wbv_ref).astype(BF16)
    iwt = tr(wiw_ref)
    ones_rows = jnp.where(lax.broadcasted_iota(I32, (V_PAD, TK), 0) == 0, 1.0, 0.0).astype(BF16)
    a_dv = 2 * A_DH
    for c in range(TS_PROJ // TK):
        for hd in range(A_HEADS):
            r0 = hd * (a_dv + V_PAD)
            av_ref[0, c, r0:r0 + a_dv, :] = avt[hd * a_dv:(hd + 1) * a_dv, c * TK:(c + 1) * TK]
            av_ref[0, c, r0 + a_dv:r0 + a_dv + V_PAD, :] = ones_rows
        bv_ref[0, c, 0:B_DH, :] = bvt[:, c * TK:(c + 1) * TK]
        bv_ref[0, c, B_DH:B_DH + V_PAD, :] = ones_rows
        iw_ref[0, c] = iwt[:, c * TK:(c + 1) * TK]


def _in_proj(x, mod, g, ws, bqc):
    bsz, s, d = x.shape
    ts = TS_PROJ
    nk = s // TK
    cpt = ts // TK

    def tok(cols):
        return pl.BlockSpec((1, ts, cols), lambda b, t: (b, t, 0))

    def trs(rows):
        return pl.BlockSpec((1, cpt, rows, TK), lambda b, t: (b, t, 0, 0))

    a_cols = A_HEADS * 2 * A_DH
    i_cols = IDX_HEADS * IDX_DH
    out_shape = [
        jax.ShapeDtypeStruct((bsz, s, a_cols), BF16),
        jax.ShapeDtypeStruct((bsz, s, A_HEADS * 256), BF16),
        jax.ShapeDtypeStruct((bsz, s, B_HEADS * 128), BF16),
        jax.ShapeDtypeStruct((bsz, s, i_cols), BF16),
        jax.ShapeDtypeStruct((bsz, s, 128), BF16),
        jax.ShapeDtypeStruct((bsz, s, 128), BF16),
        jax.ShapeDtypeStruct((bsz, nk, A_HEADS * (2 * A_DH + V_PAD), TK), BF16),
        jax.ShapeDtypeStruct((bsz, nk, B_DH + V_PAD, TK), BF16),
        jax.ShapeDtypeStruct((bsz, nk, IDX_HEADS, TK), F32),
    ]
    out_specs = [tok(a_cols), tok(A_HEADS * 256), tok(B_HEADS * 128), tok(i_cols), tok(128),
                 tok(128), trs(A_HEADS * (2 * A_DH + V_PAD)), trs(B_DH + V_PAD), trs(IDX_HEADS)]
    in_specs = [
        pl.BlockSpec((1, ts, d), lambda b, t: (b, t, 0)),
        pl.BlockSpec((1, 6, d), lambda b, t: (b, 0, 0)),
        _const_spec(g.shape),
    ] + [_const_spec(w.shape) for w in ws] + [_const_spec(bqc.shape)]
    return pl.pallas_call(
        _inproj_kernel,
        grid=(bsz, s // ts),
        in_specs=in_specs,
        out_specs=out_specs,
        out_shape=out_shape,
        compiler_params=pltpu.CompilerParams(
            dimension_semantics=("arbitrary", "arbitrary"), vmem_limit_bytes=VMEM_LIMIT_BYTES),
        name="in_proj",
    )(x, mod, g, *ws, bqc)


def _softmax_step(s, m_ref, idx):
    m_prev = m_ref[idx]
    m_new = jnp.maximum(m_prev, jnp.max(s, axis=0, keepdims=True))
    m_ref[idx] = m_new
    return jnp.exp2(s - m_new).astype(BF16), jnp.exp2(m_prev - m_new)


def _accumulate(v, p, alpha, acc_ref, idx):
    acc_ref[idx] = alpha * acc_ref[idx] + jnp.dot(v, p, preferred_element_type=F32)


def _key_to_f32(key):
    return pltpu.bitcast(key ^ ((key >> 31) & jnp.int32(0x7FFFFFFF)), F32)


def _f32_to_key(x):
    bits = pltpu.bitcast(x, I32)
    return bits ^ ((bits >> 31) & jnp.int32(0x7FFFFFFF))


def _as_exact_f32(x):
    return _key_to_f32(_f32_to_key(x))


def _attn_kernel(lam_init, aq_ref, bqp_ref, iq_ref, iw_ref, akp_ref, av_ref, bkp_ref, ikk_ref,
                 bv_ref, lamv_ref, g_ref, out_ref,
                 score_ref, rng_ref, sa_ref, sb_ref, qa_ref, qi_ref, qb_ref, w_ref,
                 acc_a_ref, m_a_ref, acc_b_ref, m_b_ref):
    i = pl.program_id(1)
    n_off = i * (TQ // TK)
    nkv = n_off + TQ // TK
    slopes_a = _alibi_slopes(A_HEADS)
    slopes_b = _alibi_slopes(B_HEADS)
    n_maps = 2 * A_HEADS

    lane = lax.broadcasted_iota(I32, (TQ, 128), 1)
    keep_lo = jnp.where(lane < 64, 1.0, 0.0).astype(BF16)
    keep_hi = jnp.where(lane < 64, 0.0, 1.0).astype(BF16)
    idx_grp = IDX_HEADS // A_HEADS
    for h in range(A_HEADS):
        qh = aq_ref[0, :, h * 128:(h + 1) * 128]
        slope_cols = _lane_constants(lane, 0, _slope_columns(slopes_a[h])).astype(BF16)
        qa_ref[h, 0] = jnp.concatenate([qh * keep_lo, slope_cols], axis=1).T
        qa_ref[h, 1] = jnp.concatenate([qh * keep_hi, slope_cols], axis=1).T
    for t in range(IDX_HEADS // 2):
        qt = iq_ref[0, :, t * 128:(t + 1) * 128]
        for half, keep in enumerate((keep_lo, keep_hi)):
            g, k = divmod(2 * t + half, idx_grp)
            qi_ref[g, k * TQ:(k + 1) * TQ, :] = qt * keep
    for h in range(B_HEADS):
        qb_ref[h] = bqp_ref[0, :, h * 128:(h + 1) * 128].T
    for c in range(TQ // TK):
        w_ref[:, c * TK:(c + 1) * TK] = iw_ref[0, c] * ((IDX_HEADS * IDX_DH) ** -0.5)

    acc_a_ref[...] = jnp.zeros_like(acc_a_ref)
    acc_b_ref[...] = jnp.zeros_like(acc_b_ref)
    m_a_ref[...] = jnp.full_like(m_a_ref, NEG)
    m_b_ref[...] = jnp.full_like(m_b_ref, NEG)

    def diag_geometry(j):
        row = j * TK + lax.broadcasted_iota(I32, (TK, TQ), 0)
        col = i * TQ + lax.broadcasted_iota(I32, (TK, TQ), 1)
        allowed = row < ((col >> CHUNK_SHIFT) + 1) * CHUNK
        corr = (2.0 * LOG2E) * jnp.minimum((col - row).astype(F32), 0.0)
        return allowed, jnp.where(allowed, BIG, NEG), corr

    def index_block(j, diag):
        k0 = pl.multiple_of(j * TK, TK)
        kk = ikk_ref[0, pl.ds(k0, TK), :]
        score = jnp.zeros((TK, TQ), F32)
        for g in range(IDX_HEADS // idx_grp):
            xg = lax.dot_general(kk, qi_ref[g], _NT, preferred_element_type=F32)
            for k in range(idx_grp):
                hh = g * idx_grp + k
                score = score + w_ref[hh:hh + 1, :] * jnp.maximum(xg[:, k * TQ:(k + 1) * TQ], 0.0)
        lo_src = score
        if diag:
            allowed, _, _ = diag_geometry(j)
            lo_src = jnp.where(allowed, score, jnp.inf)
            score = jnp.where(allowed, score, -jnp.inf)
        score_ref[j] = score
        rng_ref[0] = jnp.minimum(rng_ref[0], jnp.min(lo_src.reshape(TK // 8, 8, TQ), axis=0))
        rng_ref[1] = jnp.maximum(rng_ref[1], jnp.max(score.reshape(TK // 8, 8, TQ), axis=0))

    def index_body(diag, j, carry):
        index_block(j, diag)
        return carry

    rng_ref[0] = jnp.full((8, TQ), jnp.inf, F32)
    rng_ref[1] = jnp.full((8, TQ), -jnp.inf, F32)
    def index_pair_body(jj, carry):
        index_block(2 * jj, False)
        index_block(2 * jj + 1, False)
        return carry

    lax.fori_loop(0, n_off // 2, index_pair_body, 0)
    lax.fori_loop(n_off // 2 * 2, n_off, functools.partial(index_body, False), 0)
    lax.fori_loop(n_off, nkv, functools.partial(index_body, True), 0)

    qpos = i * TQ + lax.broadcasted_iota(I32, (1, TQ), 1)
    n_adm = ((qpos >> CHUNK_SHIFT) + 1) * CHUNK
    searching = n_adm > TOPK_MAX

    def count_where(pred_fn):
        def body(c, cnt):
            k0 = pl.multiple_of(c * TK, TK)
            ind = jnp.where(pred_fn(score_ref[c], c), jnp.int32(1), jnp.int32(0))
            return cnt + jnp.sum(ind.reshape(TK // 8, 8, TQ), axis=0)
        cnt8 = lax.fori_loop(0, nkv, body, jnp.zeros((8, TQ), I32))
        return jnp.sum(cnt8, axis=0, keepdims=True)

    n_acc = 4

    def count_ge(cand):
        def body(c, accs):
            k0 = pl.multiple_of(c * TK, TK)
            accs = list(accs)
            for r in range(TK // 8):
                a = accs[r % n_acc]
                accs[r % n_acc] = jnp.where(score_ref[c, r * 8:(r + 1) * 8, :] >= cand, a + 1, a)
            return tuple(accs)
        accs = lax.fori_loop(0, nkv, body,
                             tuple(jnp.zeros((8, TQ), I32) for _ in range(n_acc)))
        return jnp.sum(functools.reduce(lambda a, b: a + b, accs), axis=0, keepdims=True)

    def value_pass(_, carry):
        lo, hi, cnt_lo = carry
        mid = _as_exact_f32(0.5 * lo + 0.5 * hi)
        cnt = count_ge(mid)
        inside = jnp.logical_and(mid > lo, mid < hi)
        up = jnp.logical_and(inside, cnt >= TOPK_MAX)
        dn = jnp.logical_and(inside, cnt < TOPK_MAX)
        return jnp.where(up, mid, lo), jnp.where(dn, mid, hi), jnp.where(up, cnt, cnt_lo)

    lo0 = _as_exact_f32(jnp.min(rng_ref[0], axis=0, keepdims=True))
    hi0 = _key_to_f32(_f32_to_key(jnp.max(rng_ref[1], axis=0, keepdims=True)) + 1)
    lo, hi, cnt_at = lax.fori_loop(0, VALUE_PASSES, value_pass, (lo0, hi0, n_adm))

    def unresolved(lo, hi, cnt_lo):
        gap = _f32_to_key(hi) - _f32_to_key(lo)
        open_ = jnp.logical_and(searching, jnp.logical_and(cnt_lo != TOPK_MAX, gap != 1))
        return jnp.where(open_, jnp.int32(1), jnp.int32(0))

    def key_cond(carry):
        return jnp.max(carry[3]) > 0

    def key_pass(carry):
        lo, hi, cnt_lo, open_ = carry
        klo = _f32_to_key(lo)
        kmid = klo + lax.shift_right_logical(_f32_to_key(hi) - klo, jnp.int32(1))
        mid = _key_to_f32(kmid)
        cnt = count_ge(mid)
        up = jnp.logical_and(open_ > 0, cnt >= TOPK_MAX)
        dn = jnp.logical_and(open_ > 0, cnt < TOPK_MAX)
        lo, hi, cnt_lo = jnp.where(up, mid, lo), jnp.where(dn, mid, hi), jnp.where(up, cnt, cnt_lo)
        return lo, hi, cnt_lo, unresolved(lo, hi, cnt_lo)

    lo, hi, cnt_at, _ = lax.while_loop(
        key_cond, key_pass, (lo, hi, cnt_at, unresolved(lo, hi, cnt_at)))
    tau = jnp.where(searching, lo, F32_LOWEST)
    cnt_at = jnp.where(searching, cnt_at, TOPK_MAX)

    @pl.when(jnp.max(cnt_at) > TOPK_MAX)
    def _():
        def pos_of(c):
            return lax.broadcasted_iota(I32, (TK, TQ), 0) + c * TK

        need = TOPK_MAX - count_where(lambda st, c: st > tau)
        n_bits = int(score_ref.shape[0] * TK).bit_length()

        def idx_body(t, bound):
            trial = bound | jnp.left_shift(jnp.int32(1), n_bits - 1 - t)
            cnt = count_where(
                lambda st, c: jnp.where(st == tau, pos_of(c), jnp.int32(1 << 30)) < trial)
            return jnp.where(cnt <= need, trial, bound)

        bound = lax.fori_loop(0, n_bits, idx_body, jnp.zeros((1, TQ), I32))
        below = _key_to_f32(_f32_to_key(tau) - 1)

        def demote(c, carry):
            k0 = pl.multiple_of(c * TK, TK)
            st = score_ref[c]
            drop = jnp.where(st == tau, pos_of(c), jnp.int32(-1)) >= bound
            score_ref[c] = jnp.where(drop, below, st)
            return carry

        lax.fori_loop(0, nkv, demote, 0)

    n_stage = A_HEADS
    dsa_grp = B_HEADS // n_stage
    a_dv = 2 * A_DH
    a_rows_v = a_dv + V_PAD

    n_chain = 2 + dsa_grp

    def qk_chain(h, c, kstart):
        if c < 2:
            kh = akp_ref[0, pl.ds(kstart, TK), h * 256:(h + 1) * 256]
            qh = qa_ref[h, c]
        else:
            kh = bkp_ref[0, pl.ds(kstart, TK), :]
            hb = dsa_grp * h + c - 2
            qh = qb_ref[hb]
        return jnp.dot(kh, qh, preferred_element_type=F32)

    def carry_ref(c):
        return (sa_ref, c * TQ) if c < 2 else (sb_ref, (c - 2) * TQ)

    def attend_block(j, diag):
        k0 = pl.multiple_of(j * TK, TK)
        kn0 = pl.multiple_of(jnp.minimum(j + 1, nkv - 1) * TK, TK)
        slot = lax.rem(j, 2)
        cap_sel = jnp.where(score_ref[j] >= tau, BIG, NEG)
        if diag:
            _, cap_chunk, corr = diag_geometry(j)
        vb = bv_ref[0, j]
        s_next = []
        for c in range(n_chain):
            ref, off = carry_ref(c)
            s_next.append(ref[slot, :, off:off + TQ])
        for h in range(n_stage):
            vh = av_ref[0, j, h * a_rows_v:(h + 1) * a_rows_v, :]
            for c in range(n_chain):
                s = s_next[c]
                if h + 1 < n_stage:
                    s_next[c] = qk_chain(h + 1, c, k0)
                else:
                    ref, off = carry_ref(c)
                    ref[1 - slot, :, off:off + TQ] = qk_chain(0, c, kn0)
                if c < 2:
                    if diag:
                        s = jnp.minimum(s, cap_chunk) + slopes_a[h] * corr
                    p, alpha = _softmax_step(s, m_a_ref, 2 * h + c)
                    _accumulate(vh, p, alpha, acc_a_ref, 2 * h + c)
                else:
                    hb = dsa_grp * h + c - 2
                    s = jnp.minimum(s, cap_sel)
                    if diag:
                        s = s + slopes_b[hb] * corr
                    p, alpha = _softmax_step(s, m_b_ref, hb)
                    _accumulate(vb, p, alpha, acc_b_ref, hb)

    def attend_body(diag, j, carry):
        attend_block(j, diag)
        return carry

    for c in range(n_chain):
        ref, off = carry_ref(c)
        ref[0, :, off:off + TQ] = qk_chain(0, c, 0)
    lax.fori_loop(0, n_off, functools.partial(attend_body, False), 0)
    lax.fori_loop(n_off, nkv, functools.partial(attend_body, True), 0)

    lv = lamv_ref[...]
    lam = (jnp.exp(jnp.sum(lv[0:1] * lv[1:2], axis=1, keepdims=True))
           - jnp.exp(jnp.sum(lv[2:3] * lv[3:4], axis=1, keepdims=True)) + lam_init)
    g = g_ref[...]
    for h in range(A_HEADS):
        o1 = acc_a_ref[2 * h, 0:a_dv, :] / acc_a_ref[2 * h, a_dv:a_dv + 1, :]
        o2 = acc_a_ref[2 * h + 1, 0:a_dv, :] / acc_a_ref[2 * h + 1, a_dv:a_dv + 1, :]
        o = o1 - lam * o2
        y = o * lax.rsqrt(jnp.mean(o * o, axis=0, keepdims=True) + EPS)
        out_ref[0, 0, h * 128:(h + 1) * 128, :] = ((y * g) * (1.0 - lam_init)).astype(BF16)
    a_rows = A_HEADS * 2 * A_DH
    for h in range(B_HEADS):
        ob = acc_b_ref[h, 0:B_DH, :] / acc_b_ref[h, B_DH:B_DH + 1, :]
        out_ref[0, 0, a_rows + h * B_DH:a_rows + (h + 1) * B_DH, :] = ob.astype(BF16)


def _attention(lam_init, aq, bqp, iq, iwt, akp, avt, bkp, ikk, bvt, lamv, gt):
    bsz, s, a_cols = aq.shape
    nq = s // TQ
    nk = s // TK
    i_cols = iq.shape[2]
    d_mix = a_cols + B_HEADS * B_DH

    def per_batch(shape):
        return pl.BlockSpec(shape, lambda b, i: (b,) + (0,) * (len(shape) - 1))

    in_specs = [
        pl.BlockSpec((1, TQ, a_cols), lambda b, i: (b, i, 0)),
        pl.BlockSpec((1, TQ, bqp.shape[2]), lambda b, i: (b, i, 0)),
        pl.BlockSpec((1, TQ, i_cols), lambda b, i: (b, i, 0)),
        pl.BlockSpec((1, TQ // TK, IDX_HEADS, TK), lambda b, i: (b, i, 0, 0)),
        per_batch((1, s, akp.shape[2])),
        per_batch((1, nk, avt.shape[2], TK)),
        per_batch((1, s, 128)),
        per_batch((1, s, 128)),
        per_batch((1, nk, bvt.shape[2], TK)),
        _const_spec(lamv.shape),
        _const_spec(gt.shape),
    ]
    n_maps = 2 * A_HEADS
    scratch = [
        pltpu.VMEM((nk, TK, TQ), F32),
        pltpu.VMEM((2, 8, TQ), F32),
        pltpu.VMEM((2, TK, 2 * TQ), F32),
        pltpu.VMEM((2, TK, B_HEADS // A_HEADS * TQ), F32),
        pltpu.VMEM((A_HEADS, 2, 256, TQ), BF16),
        pltpu.VMEM((A_HEADS, IDX_HEADS // A_HEADS * TQ, 128), BF16),
        pltpu.VMEM((B_HEADS, 128, TQ), BF16),
        pltpu.VMEM((IDX_HEADS, TQ), F32),
        pltpu.VMEM((n_maps, 2 * A_DH + V_PAD, TQ), F32),
        pltpu.VMEM((n_maps, 1, TQ), F32),
        pltpu.VMEM((B_HEADS, B_DH + V_PAD, TQ), F32),
        pltpu.VMEM((B_HEADS, 1, TQ), F32),
    ]
    return pl.pallas_call(
        functools.partial(_attn_kernel, lam_init),
        grid=(bsz, nq),
        in_specs=in_specs,
        out_specs=pl.BlockSpec((1, 1, d_mix, TQ), lambda b, i: (b, i, 0, 0)),
        out_shape=jax.ShapeDtypeStruct((bsz, nq, d_mix, TQ), BF16),
        scratch_shapes=scratch,
        compiler_params=pltpu.CompilerParams(
            dimension_semantics=("arbitrary", "arbitrary"), vmem_limit_bytes=VMEM_LIMIT_BYTES),
        name="attention",
    )(aq, bqp, iq, iwt, akp, avt, bkp, ikk, bvt, lamv, gt)


def _ffn_kernel(final, mix_ref, x_ref, mod_ref, n2_ref, fg_ref, wo_ref, w1_ref, w2_ref, o_ref):
    wo = wo_ref[...]
    y = jnp.concatenate(
        [lax.dot_general(mix_ref[0, c], wo, _TN, preferred_element_type=F32)
         for c in range(TM_FFN // TQ)], axis=0)
    g1 = mod_ref[0, 2:3, :]
    sh2 = mod_ref[0, 3:4, :]
    sc2 = mod_ref[0, 4:5, :]
    g2 = mod_ref[0, 5:6, :]
    x1 = x_ref[0] + g1 * y
    h2 = ((_rms(x1) * n2_ref[...]) * (1.0 + sc2) + sh2).astype(BF16)
    ff = jnp.zeros_like(x1)
    for c in range(w1_ref.shape[1] // FF_CHUNK):
        u = jnp.dot(h2, w1_ref[:, c * FF_CHUNK:(c + 1) * FF_CHUNK], preferred_element_type=F32)
        u = jnp.square(jnp.maximum(u, 0.0)).astype(BF16)
        ff = ff + jnp.dot(u, w2_ref[c * FF_CHUNK:(c + 1) * FF_CHUNK, :], preferred_element_type=F32)
    x2 = x1 + g2 * ff
    if final:
        x2 = _rms(x2) * fg_ref[...]
    o_ref[0] = x2


def _out_ffn(final, mixt, x, mod, n2g, fg, wo, w1, w2):
    bsz, s, d = x.shape
    d_mix = mixt.shape[2]
    tm = TM_FFN
    cpt = tm // TQ
    return pl.pallas_call(
        functools.partial(_ffn_kernel, final),
        grid=(bsz, s // tm),
        in_specs=[
            pl.BlockSpec((1, cpt, d_mix, TQ), lambda b, t: (b, t, 0, 0)),
            pl.BlockSpec((1, tm, d), lambda b, t: (b, t, 0)),
            pl.BlockSpec((1, 6, d), lambda b, t: (b, 0, 0)),
            _const_spec(n2g.shape),
            _const_spec(fg.shape),
            _const_spec(wo.shape),
            _const_spec(w1.shape),
            _const_spec(w2.shape),
        ],
        out_specs=pl.BlockSpec((1, tm, d), lambda b, t: (b, t, 0)),
        out_shape=jax.ShapeDtypeStruct((bsz, s, d), F32),
        compiler_params=pltpu.CompilerParams(
            dimension_semantics=("arbitrary", "arbitrary"), vmem_limit_bytes=VMEM_LIMIT_BYTES),
        name="out_ffn",
    )(mixt, x, mod, n2g, fg, wo, w1, w2)


def _split_w_in(w):
    d = w.shape[0]
    a = A_HEADS * 2 * A_DH
    sizes = (a, a, a, B_HEADS * B_DH, B_DH, B_DH, IDX_HEADS * IDX_DH, IDX_DH, IDX_HEADS)
    offs = np.cumsum((0,) + sizes)
    aq, ak, av, bq, bk, bv, iq, ik, iw = [w[:, offs[n]:offs[n + 1]] for n in range(len(sizes))]
    kk = jnp.concatenate([bk, jnp.zeros((d, 128 - B_DH), w.dtype), ik, ik], axis=1)
    nat = [aq, ak, bq, iq, kk]
    trn = [av.T, bv.T, iw.T]
    return [m.astype(BF16) for m in nat + trn]


def _dsa_slope_columns():
    row = np.zeros((1, B_HEADS * 128), np.float32)
    for h, sl in enumerate(_alibi_slopes(B_HEADS)):
        for k, v in enumerate(_slope_columns(sl)):
            row[0, h * 128 + B_DH + k] = v
    return jnp.asarray(row)


def kernel(x, c, norm1_g, norm2_g, w_ada, b_ada, w_in, lam_q1, lam_k1, lam_q2, lam_k2,
           subln_g, w_out, w_ff1, w_ff2, final_g):
    bsz, s, d = x.shape
    depth = w_in.shape[0]
    assert s % TS_PROJ == 0 and s % TM_FFN == 0 and s % TQ == 0 and TQ % TK == 0 and TM_FFN % TQ == 0
    assert min(TOPK_MAX, s // 4) == TOPK_MAX and s // POS_SPLIT <= 256 and CHUNK == 1 << CHUNK_SHIFT
    fg = final_g.reshape(1, d)
    bqc = _dsa_slope_columns()
    for l in range(depth):
        lam_init = 0.8 - 0.6 * math.exp(-0.3 * l)
        mod = _adaln_mod(c, w_ada[l].astype(BF16), b_ada[l].reshape(1, -1)).reshape(bsz, 6, d)
        aq, akp, bqp, iq, bkp, ikk, avt, bvt, iwt = _in_proj(
            x, mod, norm1_g[l].reshape(1, d), _split_w_in(w_in[l]), bqc)
        lamv = jnp.stack([lam_q1[l], lam_k1[l], lam_q2[l], lam_k2[l]])
        gt = jnp.broadcast_to(subln_g[l][:, None], (2 * A_DH, TQ))
        mixt = _attention(lam_init, aq, bqp, iq, iwt, akp, avt, bkp, ikk, bvt, lamv, gt)
        x = _out_ffn(l == depth - 1, mixt, x, mod, norm2_g[l].reshape(1, d), fg,
                     w_out[l].astype(BF16), w_ff1[l].astype(BF16), w_ff2[l].astype(BF16))
    return x
```

```python
import functools
import math

import jax
import jax.numpy as jnp
import numpy as np
from jax import lax
from jax.experimental import pallas as pl
from jax.experimental.pallas import tpu as pltpu

CHUNK = 64
CHUNK_SHIFT = 6
A_HEADS = 4
A_DH = 64
B_HEADS = 8
B_DH = 64
IDX_HEADS = 16
IDX_DH = 64
TOPK_MAX = 256
EPS = 1e-6

TQ = 256
TK = 256
TS_PROJ = 512
TM_FFN = 512
FF_CHUNK = 1024
VMEM_LIMIT_BYTES = 56 * 1024 * 1024
VALUE_PASSES = 18
V_PAD = 16
POS_SHIFT = 6
POS_SPLIT = 1 << POS_SHIFT

LOG2E = math.log2(math.e)
NEG = -1e30
BIG = 3e38
INT_MIN = -(2 ** 31)
F32_LOWEST = float(np.finfo(np.float32).min)
BF16 = jnp.bfloat16
F32 = jnp.float32
I32 = jnp.int32

_NT = (((1,), (1,)), ((), ()))
_TN = (((0,), (0,)), ((), ()))


def _alibi_slopes(n):
    return [2.0 ** (-8.0 * (i + 1) / n) for i in range(n)]


def _slope_columns(slope):
    pieces, rest = [], slope * LOG2E
    for _ in range(3):
        p = float(np.asarray(rest, np.float32).astype(BF16).astype(np.float32))
        pieces.append(p)
        rest -= p
    return [p * POS_SPLIT for p in pieces] + pieces


def _const_spec(shape):
    nd = len(shape)
    return pl.BlockSpec(shape, lambda *_: (0,) * nd, pipeline_mode=pl.Buffered(1))


def _rms(x):
    return x * lax.rsqrt(jnp.mean(x * x, axis=-1, keepdims=True) + EPS)


def _mod_kernel(c_ref, w_ref, b_ref, o_ref):
    c = c_ref[...]
    cond = c * (1.0 / (1.0 + jnp.exp(-c)))
    o_ref[...] = jnp.dot(cond.astype(BF16), w_ref[...], preferred_element_type=F32) + b_ref[...]


def _adaln_mod(c, w_ada, b_ada):
    bsz, d = c.shape
    n = w_ada.shape[1]
    return pl.pallas_call(
        _mod_kernel,
        grid=(n // d,),
        in_specs=[
            pl.BlockSpec((bsz, d), lambda j: (0, 0)),
            pl.BlockSpec((d, d), lambda j: (0, j)),
            pl.BlockSpec((1, d), lambda j: (0, j)),
        ],
        out_specs=pl.BlockSpec((bsz, d), lambda j: (0, j)),
        out_shape=jax.ShapeDtypeStruct((bsz, n), F32),
        compiler_params=pltpu.CompilerParams(dimension_semantics=("arbitrary",)),
        name="adaln_mod",
    )(c, w_ada, b_ada)


def _row_constants(rows, lanes, values):
    r = lax.broadcasted_iota(I32, (rows, lanes), 0)
    out = jnp.zeros((rows, lanes), F32)
    for k, v in enumerate(values):
        out = jnp.where(r == k, v, out)
    return out


def _inproj_kernel(x_ref, mod_ref, g_ref, wak_ref, wkk_ref, wt_ref,
                   qa_ref, qb_ref, qi_ref, akp_ref, bkp_ref, ikk_ref, av_ref, bv_ref, iw_ref):
    x = x_ref[0]
    sh1 = mod_ref[0, 0:1, :]
    sc1 = mod_ref[0, 1:2, :]
    h = (_rms(x) * g_ref[...]) * (1.0 + sc1) + sh1
    hb = h.astype(BF16)

    pos = pl.program_id(1) * TS_PROJ + lax.broadcasted_iota(I32, (TS_PROJ, 128), 0)
    lane = lax.broadcasted_iota(I32, (TS_PROJ, 128), 1)
    pos_hi = (pos >> POS_SHIFT).astype(F32)
    pos_lo = (pos & (POS_SPLIT - 1)).astype(F32)

    def pos_tile(l0):
        return jnp.where((lane >= l0) & (lane < l0 + 3), pos_hi,
                         jnp.where((lane >= l0 + 3) & (lane < l0 + 6), pos_lo, 0.0))

    ak = jnp.dot(hb, wak_ref[...], preferred_element_type=F32)
    pos0 = pos_tile(0).astype(BF16)
    for hd in range(A_HEADS):
        akp_ref[0, :, hd * 256:hd * 256 + 128] = ak[:, hd * 128:(hd + 1) * 128].astype(BF16)
        akp_ref[0, :, hd * 256 + 128:(hd + 1) * 256] = pos0
    kk = jnp.dot(hb, wkk_ref[...], preferred_element_type=F32)
    bkp_ref[0] = (kk[:, 0:128] + pos_tile(B_DH)).astype(BF16)
    ikk_ref[0] = kk[:, 128:256].astype(BF16)

    t_all = lax.dot_general(wt_ref[...], hb, _NT, preferred_element_type=F32)
    a_cols = A_HEADS * 2 * A_DH
    b_cols = B_HEADS * B_DH
    i_cols = IDX_HEADS * IDX_DH
    r_bq = a_cols
    r_iq = r_bq + b_cols
    r_av = r_iq + i_cols
    r_bv = r_av + a_cols
    r_iw = r_bv + B_DH
    aqt = (t_all[0:r_bq] * (A_DH ** -0.5 * LOG2E)).astype(BF16)
    bqt = (t_all[r_bq:r_iq] * (B_DH ** -0.5 * LOG2E)).astype(BF16)
    iqt = t_all[r_iq:r_av].astype(BF16)
    avt = t_all[r_av:r_bv].astype(BF16)
    bvt = t_all[r_bv:r_iw].astype(BF16)
    iwt = t_all[r_iw:r_iw + IDX_HEADS]

    zeros = jnp.zeros((A_DH, TQ), BF16)
    idx_grp = IDX_HEADS // A_HEADS
    slopes_a = _alibi_slopes(A_HEADS)
    slopes_b = _alibi_slopes(B_HEADS)
    for c in range(TS_PROJ // TQ):
        cols = slice(c * TQ, (c + 1) * TQ)
        for hd in range(A_HEADS):
            slope_rows = _row_constants(128, TQ, _slope_columns(slopes_a[hd])).astype(BF16)
            for mp in range(2):
                r0 = hd * 2 * A_DH + mp * A_DH
                qa_ref[0, c, 2 * hd + mp, mp * A_DH:(mp + 1) * A_DH, :] = aqt[r0:r0 + A_DH, cols]
                qa_ref[0, c, 2 * hd + mp, (1 - mp) * A_DH:(2 - mp) * A_DH, :] = zeros
                qa_ref[0, c, 2 * hd + mp, 2 * A_DH:, :] = slope_rows
        for hd in range(B_HEADS):
            qb_ref[0, c, hd, 0:B_DH, :] = bqt[hd * B_DH:(hd + 1) * B_DH, cols]
            qb_ref[0, c, hd, B_DH:, :] = _row_constants(
                128 - B_DH, TQ, _slope_columns(slopes_b[hd])).astype(BF16)
        for hd in range(IDX_HEADS):
            g, k = divmod(hd, idx_grp)
            qi_ref[0, c, g, 0:IDX_DH, k * TQ:(k + 1) * TQ] = iqt[hd * IDX_DH:(hd + 1) * IDX_DH, cols]
            qi_ref[0, c, g, IDX_DH:, k * TQ:(k + 1) * TQ] = zeros

    ones_rows = jnp.where(lax.broadcasted_iota(I32, (V_PAD, TK), 0) == 0, 1.0, 0.0).astype(BF16)
    a_dv = 2 * A_DH
    for c in range(TS_PROJ // TK):
        for hd in range(A_HEADS):
            r0 = hd * (a_dv + V_PAD)
            av_ref[0, c, r0:r0 + a_dv, :] = avt[hd * a_dv:(hd + 1) * a_dv, c * TK:(c + 1) * TK]
            av_ref[0, c, r0 + a_dv:r0 + a_dv + V_PAD, :] = ones_rows
        bv_ref[0, c, 0:B_DH, :] = bvt[:, c * TK:(c + 1) * TK]
        bv_ref[0, c, B_DH:B_DH + V_PAD, :] = ones_rows
        iw_ref[0, c] = iwt[:, c * TK:(c + 1) * TK]


def _in_proj(x, mod, g, ws):
    bsz, s, d = x.shape
    ts = TS_PROJ
    nk = s // TK
    nq = s // TQ
    cpt = ts // TK
    cpq = ts // TQ

    def tok(cols):
        return pl.BlockSpec((1, ts, cols), lambda b, t: (b, t, 0))

    def trs(rows):
        return pl.BlockSpec((1, cpt, rows, TK), lambda b, t: (b, t, 0, 0))

    def qop(n, rows, lanes):
        return pl.BlockSpec((1, cpq, n, rows, lanes), lambda b, t: (b, t, 0, 0, 0))

    idx_grp = IDX_HEADS // A_HEADS
    out_shape = [
        jax.ShapeDtypeStruct((bsz, nq, 2 * A_HEADS, 256, TQ), BF16),
        jax.ShapeDtypeStruct((bsz, nq, B_HEADS, 128, TQ), BF16),
        jax.ShapeDtypeStruct((bsz, nq, A_HEADS, 128, idx_grp * TQ), BF16),
        jax.ShapeDtypeStruct((bsz, s, A_HEADS * 256), BF16),
        jax.ShapeDtypeStruct((bsz, s, 128), BF16),
        jax.ShapeDtypeStruct((bsz, s, 128), BF16),
        jax.ShapeDtypeStruct((bsz, nk, A_HEADS * (2 * A_DH + V_PAD), TK), BF16),
        jax.ShapeDtypeStruct((bsz, nk, B_DH + V_PAD, TK), BF16),
        jax.ShapeDtypeStruct((bsz, nk, IDX_HEADS, TK), F32),
    ]
    out_specs = [qop(2 * A_HEADS, 256, TQ), qop(B_HEADS, 128, TQ), qop(A_HEADS, 128, idx_grp * TQ),
                 tok(A_HEADS * 256), tok(128), tok(128),
                 trs(A_HEADS * (2 * A_DH + V_PAD)), trs(B_DH + V_PAD), trs(IDX_HEADS)]
    in_specs = [
        pl.BlockSpec((1, ts, d), lambda b, t: (b, t, 0)),
        pl.BlockSpec((1, 6, d), lambda b, t: (b, 0, 0)),
        _const_spec(g.shape),
    ] + [_const_spec(w.shape) for w in ws]
    return pl.pallas_call(
        _inproj_kernel,
        grid=(bsz, s // ts),
        in_specs=in_specs,
        out_specs=out_specs,
        out_shape=out_shape,
        compiler_params=pltpu.CompilerParams(
            dimension_semantics=("arbitrary", "arbitrary"), vmem_limit_bytes=VMEM_LIMIT_BYTES),
        name="in_proj",
    )(x, mod, g, *ws)


def _softmax_step(s, m_ref, idx):
    m_prev = m_ref[idx]
    m_new = jnp.maximum(m_prev, jnp.max(s, axis=0, keepdims=True))
    m_ref[idx] = m_new
    return jnp.exp2(s - m_new).astype(BF16), jnp.exp2(m_prev - m_new)


def _accumulate(v, p, alpha, acc_ref, idx):
    acc_ref[idx] = alpha * acc_ref[idx] + jnp.dot(v, p, preferred_element_type=F32)


def _key_to_f32(key):
    return pltpu.bitcast(key ^ ((key >> 31) & jnp.int32(0x7FFFFFFF)), F32)


def _f32_to_key(x):
    bits = pltpu.bitcast(x, I32)
    return bits ^ ((bits >> 31) & jnp.int32(0x7FFFFFFF))


def _as_exact_f32(x):
    return _key_to_f32(_f32_to_key(x))


def _attn_kernel(lam_init, qa_ref, qb_ref, qi_ref, iw_ref, akp_ref, av_ref, bkp_ref, ikk_ref,
                 bv_ref, lamv_ref, g_ref, out_ref,
                 score_ref, rng_ref, sa_ref, sb_ref, w_ref,
                 acc_a_ref, m_a_ref, acc_b_ref, m_b_ref):
    i = pl.program_id(1)
    n_off = i * (TQ // TK)
    nkv = n_off + TQ // TK
    slopes_a = _alibi_slopes(A_HEADS)
    slopes_b = _alibi_slopes(B_HEADS)

    idx_grp = IDX_HEADS // A_HEADS
    for c in range(TQ // TK):
        w_ref[:, c * TK:(c + 1) * TK] = iw_ref[0, c] * ((IDX_HEADS * IDX_DH) ** -0.5)

    acc_a_ref[...] = jnp.zeros_like(acc_a_ref)
    acc_b_ref[...] = jnp.zeros_like(acc_b_ref)
    m_a_ref[...] = jnp.full_like(m_a_ref, NEG)
    m_b_ref[...] = jnp.full_like(m_b_ref, NEG)

    def diag_geometry(j):
        row = j * TK + lax.broadcasted_iota(I32, (TK, TQ), 0)
        col = i * TQ + lax.broadcasted_iota(I32, (TK, TQ), 1)
        allowed = row < ((col >> CHUNK_SHIFT) + 1) * CHUNK
        corr = (2.0 * LOG2E) * jnp.minimum((col - row).astype(F32), 0.0)
        return allowed, jnp.where(allowed, BIG, NEG), corr

    def index_block(j, diag):
        k0 = pl.multiple_of(j * TK, TK)
        kk = ikk_ref[0, pl.ds(k0, TK), :]
        score = jnp.zeros((TK, TQ), F32)
        for g in range(IDX_HEADS // idx_grp):
            xg = jnp.dot(kk, qi_ref[0, 0, g], preferred_element_type=F32)
            for k in range(idx_grp):
                hh = g * idx_grp + k
                score = score + w_ref[hh:hh + 1, :] * jnp.maximum(xg[:, k * TQ:(k + 1) * TQ], 0.0)
        lo_src = score
        if diag:
            allowed, _, _ = diag_geometry(j)
            lo_src = jnp.where(allowed, score, jnp.inf)
            score = jnp.where(allowed, score, -jnp.inf)
        score_ref[j] = score
        rng_ref[0] = jnp.minimum(rng_ref[0], jnp.min(lo_src.reshape(TK // 8, 8, TQ), axis=0))
        rng_ref[1] = jnp.maximum(rng_ref[1], jnp.max(score.reshape(TK // 8, 8, TQ), axis=0))

    def index_body(diag, j, carry):
        index_block(j, diag)
        return carry

    def index_pair_body(jj, carry):
        index_block(2 * jj, False)
        index_block(2 * jj + 1, False)
        return carry

    rng_ref[0] = jnp.full((8, TQ), jnp.inf, F32)
    rng_ref[1] = jnp.full((8, TQ), -jnp.inf, F32)
    lax.fori_loop(0, n_off // 2, index_pair_body, 0)
    lax.fori_loop(n_off // 2 * 2, n_off, functools.partial(index_body, False), 0)
    lax.fori_loop(n_off, nkv, functools.partial(index_body, True), 0)

    qpos = i * TQ + lax.broadcasted_iota(I32, (1, TQ), 1)
    n_adm = ((qpos >> CHUNK_SHIFT) + 1) * CHUNK
    searching = n_adm > TOPK_MAX

    def count_where(pred_fn):
        def body(c, cnt):
            ind = jnp.where(pred_fn(score_ref[c], c), jnp.int32(1), jnp.int32(0))
            return cnt + jnp.sum(ind.reshape(TK // 8, 8, TQ), axis=0)
        cnt8 = lax.fori_loop(0, nkv, body, jnp.zeros((8, TQ), I32))
        return jnp.sum(cnt8, axis=0, keepdims=True)

    n_acc = 4

    def count_ge(cand):
        def body(c, accs):
            accs = list(accs)
            for r in range(TK // 8):
                a = accs[r % n_acc]
                accs[r % n_acc] = jnp.where(score_ref[c, r * 8:(r + 1) * 8, :] >= cand, a + 1, a)
            return tuple(accs)
        accs = lax.fori_loop(0, nkv, body,
                             tuple(jnp.zeros((8, TQ), I32) for _ in range(n_acc)))
        return jnp.sum(functools.reduce(lambda a, b: a + b, accs), axis=0, keepdims=True)

    def value_pass(_, carry):
        lo, hi, cnt_lo = carry
        mid = _as_exact_f32(0.5 * lo + 0.5 * hi)
        cnt = count_ge(mid)
        inside = jnp.logical_and(mid > lo, mid < hi)
        up = jnp.logical_and(inside, cnt >= TOPK_MAX)
        dn = jnp.logical_and(inside, cnt < TOPK_MAX)
        return jnp.where(up, mid, lo), jnp.where(dn, mid, hi), jnp.where(up, cnt, cnt_lo)

    lo0 = _as_exact_f32(jnp.min(rng_ref[0], axis=0, keepdims=True))
    hi0 = _key_to_f32(_f32_to_key(jnp.max(rng_ref[1], axis=0, keepdims=True)) + 1)
    lo, hi, cnt_at = lax.fori_loop(0, VALUE_PASSES, value_pass, (lo0, hi0, n_adm))

    def unresolved(lo, hi, cnt_lo):
        gap = _f32_to_key(hi) - _f32_to_key(lo)
        open_ = jnp.logical_and(searching, jnp.logical_and(cnt_lo != TOPK_MAX, gap != 1))
        return jnp.where(open_, jnp.int32(1), jnp.int32(0))

    def key_cond(carry):
        return jnp.max(carry[3]) > 0

    def key_pass(carry):
        lo, hi, cnt_lo, open_ = carry
        klo = _f32_to_key(lo)
        kmid = klo + lax.shift_right_logical(_f32_to_key(hi) - klo, jnp.int32(1))
        mid = _key_to_f32(kmid)
        cnt = count_ge(mid)
        up = jnp.logical_and(open_ > 0, cnt >= TOPK_MAX)
        dn = jnp.logical_and(open_ > 0, cnt < TOPK_MAX)
        lo, hi, cnt_lo = jnp.where(up, mid, lo), jnp.where(dn, mid, hi), jnp.where(up, cnt, cnt_lo)
        return lo, hi, cnt_lo, unresolved(lo, hi, cnt_lo)

    lo, hi, cnt_at, _ = lax.while_loop(
        key_cond, key_pass, (lo, hi, cnt_at, unresolved(lo, hi, cnt_at)))
    tau = jnp.where(searching, lo, F32_LOWEST)
    cnt_at = jnp.where(searching, cnt_at, TOPK_MAX)

    @pl.when(jnp.max(cnt_at) > TOPK_MAX)
    def _():
        def pos_of(c):
            return lax.broadcasted_iota(I32, (TK, TQ), 0) + c * TK

        need = TOPK_MAX - count_where(lambda st, c: st > tau)
        n_bits = int(score_ref.shape[0] * TK).bit_length()

        def idx_body(t, bound):
            trial = bound | jnp.left_shift(jnp.int32(1), n_bits - 1 - t)
            cnt = count_where(
                lambda st, c: jnp.where(st == tau, pos_of(c), jnp.int32(1 << 30)) < trial)
            return jnp.where(cnt <= need, trial, bound)

        bound = lax.fori_loop(0, n_bits, idx_body, jnp.zeros((1, TQ), I32))
        below = _key_to_f32(_f32_to_key(tau) - 1)

        def demote(c, carry):
            st = score_ref[c]
            drop = jnp.where(st == tau, pos_of(c), jnp.int32(-1)) >= bound
            score_ref[c] = jnp.where(drop, below, st)
            return carry

        lax.fori_loop(0, nkv, demote, 0)

    n_stage = A_HEADS
    dsa_grp = B_HEADS // n_stage
    a_dv = 2 * A_DH
    a_rows_v = a_dv + V_PAD
    n_chain = 2 + dsa_grp

    def qk_chain(h, c, kstart):
        if c < 2:
            kh = akp_ref[0, pl.ds(kstart, TK), h * 256:(h + 1) * 256]
            qh = qa_ref[0, 0, 2 * h + c]
        else:
            kh = bkp_ref[0, pl.ds(kstart, TK), :]
            qh = qb_ref[0, 0, dsa_grp * h + c - 2]
        return jnp.dot(kh, qh, preferred_element_type=F32)

    def carry_ref(c):
        return (sa_ref, c * TQ) if c < 2 else (sb_ref, (c - 2) * TQ)

    def attend_block(j, diag):
        k0 = pl.multiple_of(j * TK, TK)
        kn0 = pl.multiple_of(jnp.minimum(j + 1, nkv - 1) * TK, TK)
        slot = lax.rem(j, 2)
        cap_sel = jnp.where(score_ref[j] >= tau, BIG, NEG)
        if diag:
            _, cap_chunk, corr = diag_geometry(j)
        vb = bv_ref[0, j]
        s_next = []
        for c in range(n_chain):
            ref, off = carry_ref(c)
            s_next.append(ref[slot, :, off:off + TQ])
        for h in range(n_stage):
            vh = av_ref[0, j, h * a_rows_v:(h + 1) * a_rows_v, :]
            for c in range(n_chain):
                s = s_next[c]
                if h + 1 < n_stage:
                    s_next[c] = qk_chain(h + 1, c, k0)
                else:
                    ref, off = carry_ref(c)
                    ref[1 - slot, :, off:off + TQ] = qk_chain(0, c, kn0)
                if c < 2:
                    if diag:
                        s = jnp.minimum(s, cap_chunk) + slopes_a[h] * corr
                    p, alpha = _softmax_step(s, m_a_ref, 2 * h + c)
                    _accumulate(vh, p, alpha, acc_a_ref, 2 * h + c)
                else:
                    hb = dsa_grp * h + c - 2
                    s = jnp.minimum(s, cap_sel)
                    if diag:
                        s = s + slopes_b[hb] * corr
                    p, alpha = _softmax_step(s, m_b_ref, hb)
                    _accumulate(vb, p, alpha, acc_b_ref, hb)

    def attend_body(diag, j, carry):
        attend_block(j, diag)
        return carry

    for c in range(n_chain):
        ref, off = carry_ref(c)
        ref[0, :, off:off + TQ] = qk_chain(0, c, 0)
    lax.fori_loop(0, n_off, functools.partial(attend_body, False), 0)
    lax.fori_loop(n_off, nkv, functools.partial(attend_body, True), 0)

    lv = lamv_ref[...]
    lam = (jnp.exp(jnp.sum(lv[0:1] * lv[1:2], axis=1, keepdims=True))
           - jnp.exp(jnp.sum(lv[2:3] * lv[3:4], axis=1, keepdims=True)) + lam_init)
    g = g_ref[...]
    for h in range(A_HEADS):
        o1 = acc_a_ref[2 * h, 0:a_dv, :] / acc_a_ref[2 * h, a_dv:a_dv + 1, :]
        o2 = acc_a_ref[2 * h + 1, 0:a_dv, :] / acc_a_ref[2 * h + 1, a_dv:a_dv + 1, :]
        o = o1 - lam * o2
        y = o * lax.rsqrt(jnp.mean(o * o, axis=0, keepdims=True) + EPS)
        out_ref[0, 0, h * 128:(h + 1) * 128, :] = ((y * g) * (1.0 - lam_init)).astype(BF16)
    a_rows = A_HEADS * 2 * A_DH
    for h in range(B_HEADS):
        ob = acc_b_ref[h, 0:B_DH, :] / acc_b_ref[h, B_DH:B_DH + 1, :]
        out_ref[0, 0, a_rows + h * B_DH:a_rows + (h + 1) * B_DH, :] = ob.astype(BF16)


def _attention(lam_init, qat, qbt, qit, iwt, akp, avt, bkp, ikk, bvt, lamv, gt):
    bsz, s, _ = akp.shape
    nq = s // TQ
    nk = s // TK
    d_mix = A_HEADS * 2 * A_DH + B_HEADS * B_DH

    def per_query_block(arr):
        shape = (1, 1) + arr.shape[2:]
        return pl.BlockSpec(shape, lambda b, i: (b, i) + (0,) * (len(shape) - 2))

    def per_batch(shape):
        return pl.BlockSpec(shape, lambda b, i: (b,) + (0,) * (len(shape) - 1))

    in_specs = [
        per_query_block(qat),
        per_query_block(qbt),
        per_query_block(qit),
        pl.BlockSpec((1, TQ // TK, IDX_HEADS, TK), lambda b, i: (b, i, 0, 0)),
        per_batch((1, s, akp.shape[2])),
        per_batch((1, nk, avt.shape[2], TK)),
        per_batch((1, s, 128)),
        per_batch((1, s, 128)),
        per_batch((1, nk, bvt.shape[2], TK)),
        _const_spec(lamv.shape),
        _const_spec(gt.shape),
    ]
    n_maps = 2 * A_HEADS
    scratch = [
        pltpu.VMEM((nk, TK, TQ), F32),
        pltpu.VMEM((2, 8, TQ), F32),
        pltpu.VMEM((2, TK, 2 * TQ), F32),
        pltpu.VMEM((2, TK, B_HEADS // A_HEADS * TQ), F32),
        pltpu.VMEM((IDX_HEADS, TQ), F32),
        pltpu.VMEM((n_maps, 2 * A_DH + V_PAD, TQ), F32),
        pltpu.VMEM((n_maps, 1, TQ), F32),
        pltpu.VMEM((B_HEADS, B_DH + V_PAD, TQ), F32),
        pltpu.VMEM((B_HEADS, 1, TQ), F32),
    ]
    return pl.pallas_call(
        functools.partial(_attn_kernel, lam_init),
        grid=(bsz, nq),
        in_specs=in_specs,
        out_specs=pl.BlockSpec((1, 1, d_mix, TQ), lambda b, i: (b, i, 0, 0)),
        out_shape=jax.ShapeDtypeStruct((bsz, nq, d_mix, TQ), BF16),
        scratch_shapes=scratch,
        compiler_params=pltpu.CompilerParams(
            dimension_semantics=("arbitrary", "arbitrary"), vmem_limit_bytes=VMEM_LIMIT_BYTES),
        name="attention",
    )(qat, qbt, qit, iwt, akp, avt, bkp, ikk, bvt, lamv, gt)


def _ffn_kernel(final, mix_ref, x_ref, mod_ref, n2_ref, fg_ref, wo_ref, w1_ref, w2_ref, o_ref):
    wo = wo_ref[...]
    y = jnp.concatenate(
        [lax.dot_general(mix_ref[0, c], wo, _TN, preferred_element_type=F32)
         for c in range(TM_FFN // TQ)], axis=0)
    g1 = mod_ref[0, 2:3, :]
    sh2 = mod_ref[0, 3:4, :]
    sc2 = mod_ref[0, 4:5, :]
    g2 = mod_ref[0, 5:6, :]
    x1 = x_ref[0] + g1 * y
    h2 = ((_rms(x1) * n2_ref[...]) * (1.0 + sc2) + sh2).astype(BF16)
    ff = jnp.zeros_like(x1)
    for c in range(w1_ref.shape[1] // FF_CHUNK):
        u = jnp.dot(h2, w1_ref[:, c * FF_CHUNK:(c + 1) * FF_CHUNK], preferred_element_type=F32)
        u = jnp.square(jnp.maximum(u, 0.0)).astype(BF16)
        ff = ff + jnp.dot(u, w2_ref[c * FF_CHUNK:(c + 1) * FF_CHUNK, :], preferred_element_type=F32)
    x2 = x1 + g2 * ff
    if final:
        x2 = _rms(x2) * fg_ref[...]
    o_ref[0] = x2


def _out_ffn(final, mixt, x, mod, n2g, fg, wo, w1, w2):
    bsz, s, d = x.shape
    d_mix = mixt.shape[2]
    tm = TM_FFN
    cpt = tm // TQ
    return pl.pallas_call(
        functools.partial(_ffn_kernel, final),
        grid=(bsz, s // tm),
        in_specs=[
            pl.BlockSpec((1, cpt, d_mix, TQ), lambda b, t: (b, t, 0, 0)),
            pl.BlockSpec((1, tm, d), lambda b, t: (b, t, 0)),
            pl.BlockSpec((1, 6, d), lambda b, t: (b, 0, 0)),
            _const_spec(n2g.shape),
            _const_spec(fg.shape),
            _const_spec(wo.shape),
            _const_spec(w1.shape),
            _const_spec(w2.shape),
        ],
        out_specs=pl.BlockSpec((1, tm, d), lambda b, t: (b, t, 0)),
        out_shape=jax.ShapeDtypeStruct((bsz, s, d), F32),
        compiler_params=pltpu.CompilerParams(
            dimension_semantics=("arbitrary", "arbitrary"), vmem_limit_bytes=VMEM_LIMIT_BYTES),
        name="out_ffn",
    )(mixt, x, mod, n2g, fg, wo, w1, w2)


def _split_w_in(w):
    d = w.shape[0]
    a = A_HEADS * 2 * A_DH
    sizes = (a, a, a, B_HEADS * B_DH, B_DH, B_DH, IDX_HEADS * IDX_DH, IDX_DH, IDX_HEADS)
    offs = np.cumsum((0,) + sizes)
    aq, ak, av, bq, bk, bv, iq, ik, iw = [w[:, offs[n]:offs[n + 1]] for n in range(len(sizes))]
    kk = jnp.concatenate([bk, jnp.zeros((d, 128 - B_DH), w.dtype), ik, ik], axis=1)
    wt = jnp.concatenate([aq, bq, iq, av, bv, iw], axis=1).T
    return [m.astype(BF16) for m in (ak, kk, wt)]


def kernel(x, c, norm1_g, norm2_g, w_ada, b_ada, w_in, lam_q1, lam_k1, lam_q2, lam_k2,
           subln_g, w_out, w_ff1, w_ff2, final_g):
    bsz, s, d = x.shape
    depth = w_in.shape[0]
    assert s % TS_PROJ == 0 and s % TM_FFN == 0 and s % TQ == 0 and TQ % TK == 0 and TM_FFN % TQ == 0
    assert min(TOPK_MAX, s // 4) == TOPK_MAX and s // POS_SPLIT <= 256 and CHUNK == 1 << CHUNK_SHIFT
    fg = final_g.reshape(1, d)
    for l in range(depth):
        lam_init = 0.8 - 0.6 * math.exp(-0.3 * l)
        mod = _adaln_mod(c, w_ada[l].astype(BF16), b_ada[l].reshape(1, -1)).reshape(bsz, 6, d)
        qat, qbt, qit, akp, bkp, ikk, avt, bvt, iwt = _in_proj(
            x, mod, norm1_g[l].reshape(1, d), _split_w_in(w_in[l]))
        lamv = jnp.stack([lam_q1[l], lam_k1[l], lam_q2[l], lam_k2[l]])
        gt = jnp.broadcast_to(subln_g[l][:, None], (2 * A_DH, TQ))
        mixt = _attention(lam_init, qat, qbt, qit, iwt, akp, avt, bkp, ikk, bvt, lamv, gt)
        x = _out_ffn(l == depth - 1, mixt, x, mod, norm2_g[l].reshape(1, d), fg,
                     w_out[l].astype(BF16), w_ff1[l].astype(BF16), w_ff2[l].astype(BF16))
    return x
```

```python
import functools
import math

import jax
import jax.numpy as jnp
import numpy as np
from jax import lax
from jax.experimental import pallas as pl
from jax.experimental.pallas import tpu as pltpu

CHUNK = 64
CHUNK_SHIFT = 6
A_HEADS = 4
A_DH = 64
B_HEADS = 8
B_DH = 64
IDX_HEADS = 16
IDX_DH = 64
TOPK_MAX = 256
EPS = 1e-6

TQ = 256
TK = 256
TS_PROJ = 512
TM_FFN = 512
FF_CHUNK = 1024
VMEM_LIMIT_BYTES = 56 * 1024 * 1024
IDX_UNROLL = 2
VALUE_PASSES = 18
V_PAD = 16
POS_SHIFT = 6
POS_SPLIT = 1 << POS_SHIFT

LOG2E = math.log2(math.e)
NEG = -1e30
BIG = 3e38
F32_LOWEST = float(np.finfo(np.float32).min)
BF16 = jnp.bfloat16
F32 = jnp.float32
I32 = jnp.int32

_NT = (((1,), (1,)), ((), ()))
_TN = (((0,), (0,)), ((), ()))


def _alibi_slopes(n):
    return [2.0 ** (-8.0 * (i + 1) / n) for i in range(n)]


def _slope_columns(slope):
    pieces, rest = [], slope * LOG2E
    for _ in range(3):
        p = float(np.asarray(rest, np.float32).astype(BF16).astype(np.float32))
        pieces.append(p)
        rest -= p
    return [p * POS_SPLIT for p in pieces] + pieces


def _const_spec(shape):
    nd = len(shape)
    return pl.BlockSpec(shape, lambda *_: (0,) * nd, pipeline_mode=pl.Buffered(1))


def _rms(x):
    return x * lax.rsqrt(jnp.mean(x * x, axis=-1, keepdims=True) + EPS)


def _mod_kernel(c_ref, w_ref, b_ref, o_ref):
    c = c_ref[...]
    cond = c * (1.0 / (1.0 + jnp.exp(-c)))
    o_ref[...] = jnp.dot(cond.astype(BF16), w_ref[...], preferred_element_type=F32) + b_ref[...]


def _adaln_mod(c, w_ada, b_ada):
    bsz, d = c.shape
    n = w_ada.shape[1]
    return pl.pallas_call(
        _mod_kernel,
        grid=(n // d,),
        in_specs=[
            pl.BlockSpec((bsz, d), lambda j: (0, 0)),
            pl.BlockSpec((d, d), lambda j: (0, j)),
            pl.BlockSpec((1, d), lambda j: (0, j)),
        ],
        out_specs=pl.BlockSpec((bsz, d), lambda j: (0, j)),
        out_shape=jax.ShapeDtypeStruct((bsz, n), F32),
        compiler_params=pltpu.CompilerParams(dimension_semantics=("arbitrary",)),
        name="adaln_mod",
    )(c, w_ada, b_ada)


def _row_constants(rows, lanes, values):
    r = lax.broadcasted_iota(I32, (rows, lanes), 0)
    out = jnp.zeros((rows, lanes), F32)
    for k, v in enumerate(values):
        out = jnp.where(r == k, v, out)
    return out


def _inproj_kernel(x_ref, mod_ref, g_ref, wak_ref, wkk_ref, wt_ref,
                   qa_ref, qb_ref, qi_ref, akp_ref, bkp_ref, ikk_ref, av_ref, bv_ref, iw_ref):
    x = x_ref[0]
    sh1 = mod_ref[0, 0:1, :]
    sc1 = mod_ref[0, 1:2, :]
    h = (_rms(x) * g_ref[...]) * (1.0 + sc1) + sh1
    hb = h.astype(BF16)

    pos = pl.program_id(1) * TS_PROJ + lax.broadcasted_iota(I32, (TS_PROJ, 128), 0)
    lane = lax.broadcasted_iota(I32, (TS_PROJ, 128), 1)
    pos_hi = (pos >> POS_SHIFT).astype(F32)
    pos_lo = (pos & (POS_SPLIT - 1)).astype(F32)

    def pos_tile(l0):
        return jnp.where((lane >= l0) & (lane < l0 + 3), pos_hi,
                         jnp.where((lane >= l0 + 3) & (lane < l0 + 6), pos_lo, 0.0))

    ak = jnp.dot(hb, wak_ref[...], preferred_element_type=F32)
    pos0 = pos_tile(0).astype(BF16)
    for hd in range(A_HEADS):
        akp_ref[0, :, hd * 256:hd * 256 + 128] = ak[:, hd * 128:(hd + 1) * 128].astype(BF16)
        akp_ref[0, :, hd * 256 + 128:(hd + 1) * 256] = pos0
    kk = jnp.dot(hb, wkk_ref[...], preferred_element_type=F32)
    bkp_ref[0] = (kk[:, 0:128] + pos_tile(B_DH)).astype(BF16)
    ikk_ref[0] = kk[:, 128:256].astype(BF16)

    t_all = lax.dot_general(wt_ref[...], hb, _NT, preferred_element_type=F32)
    a_cols = A_HEADS * 2 * A_DH
    b_cols = B_HEADS * B_DH
    i_cols = IDX_HEADS * IDX_DH
    r_bq = a_cols
    r_iq = r_bq + b_cols
    r_av = r_iq + i_cols
    r_bv = r_av + a_cols
    r_iw = r_bv + B_DH
    aqt = (t_all[0:r_bq] * (A_DH ** -0.5 * LOG2E)).astype(BF16)
    bqt = (t_all[r_bq:r_iq] * (B_DH ** -0.5 * LOG2E)).astype(BF16)
    iqt = t_all[r_iq:r_av].astype(BF16)
    avt = t_all[r_av:r_bv].astype(BF16)
    bvt = t_all[r_bv:r_iw].astype(BF16)
    iwt = t_all[r_iw:r_iw + IDX_HEADS]

    zeros = jnp.zeros((A_DH, TQ), BF16)
    idx_grp = IDX_HEADS // A_HEADS
    slopes_a = _alibi_slopes(A_HEADS)
    slopes_b = _alibi_slopes(B_HEADS)
    for c in range(TS_PROJ // TQ):
        cols = slice(c * TQ, (c + 1) * TQ)
        for hd in range(A_HEADS):
            slope_rows = _row_constants(128, TQ, _slope_columns(slopes_a[hd])).astype(BF16)
            for mp in range(2):
                r0 = hd * 2 * A_DH + mp * A_DH
                qa_ref[0, c, 2 * hd + mp, mp * A_DH:(mp + 1) * A_DH, :] = aqt[r0:r0 + A_DH, cols]
                qa_ref[0, c, 2 * hd + mp, (1 - mp) * A_DH:(2 - mp) * A_DH, :] = zeros
                qa_ref[0, c, 2 * hd + mp, 2 * A_DH:, :] = slope_rows
        for hd in range(B_HEADS):
            qb_ref[0, c, hd, 0:B_DH, :] = bqt[hd * B_DH:(hd + 1) * B_DH, cols]
            qb_ref[0, c, hd, B_DH:, :] = _row_constants(
                128 - B_DH, TQ, _slope_columns(slopes_b[hd])).astype(BF16)
        for hd in range(IDX_HEADS):
            g, k = divmod(hd, idx_grp)
            qi_ref[0, c, g, 0:IDX_DH, k * TQ:(k + 1) * TQ] = iqt[hd * IDX_DH:(hd + 1) * IDX_DH, cols]
            qi_ref[0, c, g, IDX_DH:, k * TQ:(k + 1) * TQ] = zeros

    ones_rows = jnp.where(lax.broadcasted_iota(I32, (V_PAD, TK), 0) == 0, 1.0, 0.0).astype(BF16)
    a_dv = 2 * A_DH
    for c in range(TS_PROJ // TK):
        for hd in range(A_HEADS):
            r0 = hd * (a_dv + V_PAD)
            av_ref[0, c, r0:r0 + a_dv, :] = avt[hd * a_dv:(hd + 1) * a_dv, c * TK:(c + 1) * TK]
            av_ref[0, c, r0 + a_dv:r0 + a_dv + V_PAD, :] = ones_rows
        bv_ref[0, c, 0:B_DH, :] = bvt[:, c * TK:(c + 1) * TK]
        bv_ref[0, c, B_DH:B_DH + V_PAD, :] = ones_rows
        iw_ref[0, c] = iwt[:, c * TK:(c + 1) * TK]


def _in_proj(x, mod, g, ws):
    bsz, s, d = x.shape
    ts = TS_PROJ
    nk = s // TK
    nq = s // TQ
    cpt = ts // TK
    cpq = ts // TQ

    def tok(cols):
        return pl.BlockSpec((1, ts, cols), lambda b, t: (b, t, 0))

    def trs(rows):
        return pl.BlockSpec((1, cpt, rows, TK), lambda b, t: (b, t, 0, 0))

    def qop(n, rows, lanes):
        return pl.BlockSpec((1, cpq, n, rows, lanes), lambda b, t: (b, t, 0, 0, 0))

    idx_grp = IDX_HEADS // A_HEADS
    out_shape = [
        jax.ShapeDtypeStruct((bsz, nq, 2 * A_HEADS, 256, TQ), BF16),
        jax.ShapeDtypeStruct((bsz, nq, B_HEADS, 128, TQ), BF16),
        jax.ShapeDtypeStruct((bsz, nq, A_HEADS, 128, idx_grp * TQ), BF16),
        jax.ShapeDtypeStruct((bsz, s, A_HEADS * 256), BF16),
        jax.ShapeDtypeStruct((bsz, s, 128), BF16),
        jax.ShapeDtypeStruct((bsz, s, 128), BF16),
        jax.ShapeDtypeStruct((bsz, nk, A_HEADS * (2 * A_DH + V_PAD), TK), BF16),
        jax.ShapeDtypeStruct((bsz, nk, B_DH + V_PAD, TK), BF16),
        jax.ShapeDtypeStruct((bsz, nk, IDX_HEADS, TK), F32),
    ]
    out_specs = [qop(2 * A_HEADS, 256, TQ), qop(B_HEADS, 128, TQ), qop(A_HEADS, 128, idx_grp * TQ),
                 tok(A_HEADS * 256), tok(128), tok(128),
                 trs(A_HEADS * (2 * A_DH + V_PAD)), trs(B_DH + V_PAD), trs(IDX_HEADS)]
    in_specs = [
        pl.BlockSpec((1, ts, d), lambda b, t: (b, t, 0)),
        pl.BlockSpec((1, 6, d), lambda b, t: (b, 0, 0)),
        _const_spec(g.shape),
    ] + [_const_spec(w.shape) for w in ws]
    return pl.pallas_call(
        _inproj_kernel,
        grid=(bsz, s // ts),
        in_specs=in_specs,
        out_specs=out_specs,
        out_shape=out_shape,
        compiler_params=pltpu.CompilerParams(
            dimension_semantics=("arbitrary", "arbitrary"), vmem_limit_bytes=VMEM_LIMIT_BYTES),
        name="in_proj",
    )(x, mod, g, *ws)


def _softmax_step(s, m_ref, idx):
    m_prev = m_ref[idx]
    m_new = jnp.maximum(m_prev, jnp.max(s, axis=0, keepdims=True))
    m_ref[idx] = m_new
    return jnp.exp2(s - m_new).astype(BF16), jnp.exp2(m_prev - m_new)


def _accumulate(v, p, alpha, acc_ref, idx):
    acc_ref[idx] = alpha * acc_ref[idx] + jnp.dot(v, p, preferred_element_type=F32)


def _key_to_f32(key):
    return pltpu.bitcast(key ^ ((key >> 31) & jnp.int32(0x7FFFFFFF)), F32)


def _f32_to_key(x):
    bits = pltpu.bitcast(x, I32)
    return bits ^ ((bits >> 31) & jnp.int32(0x7FFFFFFF))


def _as_exact_f32(x):
    return _key_to_f32(_f32_to_key(x))


def _attn_kernel(lam_init, qa_ref, qb_ref, qi_ref, iw_ref, akp_ref, av_ref, bkp_ref, ikk_ref,
                 bv_ref, lamv_ref, g_ref, out_ref,
                 score_ref, rng_ref, sa_ref, sb_ref, w_ref,
                 acc_a_ref, m_a_ref, acc_b_ref, m_b_ref):
    i = pl.program_id(1)
    n_off = i * (TQ // TK)
    nkv = n_off + TQ // TK
    slopes_a = _alibi_slopes(A_HEADS)
    slopes_b = _alibi_slopes(B_HEADS)

    idx_grp = IDX_HEADS // A_HEADS
    for c in range(TQ // TK):
        w_ref[:, c * TK:(c + 1) * TK] = iw_ref[0, c] * ((IDX_HEADS * IDX_DH) ** -0.5)

    acc_a_ref[...] = jnp.zeros_like(acc_a_ref)
    acc_b_ref[...] = jnp.zeros_like(acc_b_ref)
    m_a_ref[...] = jnp.full_like(m_a_ref, NEG)
    m_b_ref[...] = jnp.full_like(m_b_ref, NEG)

    def diag_geometry(j):
        row = j * TK + lax.broadcasted_iota(I32, (TK, TQ), 0)
        col = i * TQ + lax.broadcasted_iota(I32, (TK, TQ), 1)
        allowed = row < ((col >> CHUNK_SHIFT) + 1) * CHUNK
        corr = (2.0 * LOG2E) * jnp.minimum((col - row).astype(F32), 0.0)
        return allowed, jnp.where(allowed, BIG, NEG), corr

    def index_block(j, diag):
        k0 = pl.multiple_of(j * TK, TK)
        kk = ikk_ref[0, pl.ds(k0, TK), :]
        score = None
        for g in range(IDX_HEADS // idx_grp):
            xg = jnp.dot(kk, qi_ref[0, 0, g], preferred_element_type=F32)
            for k in range(idx_grp):
                hh = g * idx_grp + k
                term = w_ref[hh:hh + 1, :] * jnp.maximum(xg[:, k * TQ:(k + 1) * TQ], 0.0)
                score = term if score is None else score + term
        lo_src = score
        if diag:
            allowed, _, _ = diag_geometry(j)
            lo_src = jnp.where(allowed, score, jnp.inf)
            score = jnp.where(allowed, score, -jnp.inf)
        score_ref[j] = score
        rng_ref[0] = jnp.minimum(rng_ref[0], jnp.min(lo_src.reshape(TK // 8, 8, TQ), axis=0))
        rng_ref[1] = jnp.maximum(rng_ref[1], jnp.max(score.reshape(TK // 8, 8, TQ), axis=0))

    def index_body(diag, j, carry):
        index_block(j, diag)
        return carry

    def index_group_body(jj, carry):
        for u in range(IDX_UNROLL):
            index_block(IDX_UNROLL * jj + u, False)
        return carry

    rng_ref[0] = jnp.full((8, TQ), jnp.inf, F32)
    rng_ref[1] = jnp.full((8, TQ), -jnp.inf, F32)
    lax.fori_loop(0, n_off // IDX_UNROLL, index_group_body, 0)
    lax.fori_loop(n_off // IDX_UNROLL * IDX_UNROLL, n_off, functools.partial(index_body, False), 0)
    lax.fori_loop(n_off, nkv, functools.partial(index_body, True), 0)

    qpos = i * TQ + lax.broadcasted_iota(I32, (1, TQ), 1)
    n_adm = ((qpos >> CHUNK_SHIFT) + 1) * CHUNK
    searching = n_adm > TOPK_MAX

    def count_where(pred_fn):
        def body(c, cnt):
            ind = jnp.where(pred_fn(score_ref[c], c), jnp.int32(1), jnp.int32(0))
            return cnt + jnp.sum(ind.reshape(TK // 8, 8, TQ), axis=0)
        cnt8 = lax.fori_loop(0, nkv, body, jnp.zeros((8, TQ), I32))
        return jnp.sum(cnt8, axis=0, keepdims=True)

    n_acc = 4

    def count_ge(cand):
        def body(c, accs):
            accs = list(accs)
            for r in range(TK // 8):
                a = accs[r % n_acc]
                accs[r % n_acc] = jnp.where(score_ref[c, r * 8:(r + 1) * 8, :] >= cand, a + 1, a)
            return tuple(accs)
        accs = lax.fori_loop(0, nkv, body,
                             tuple(jnp.zeros((8, TQ), I32) for _ in range(n_acc)))
        return jnp.sum(functools.reduce(lambda a, b: a + b, accs), axis=0, keepdims=True)

    def value_pass(_, carry):
        lo, hi, cnt_lo = carry
        mid = _as_exact_f32(0.5 * lo + 0.5 * hi)
        cnt = count_ge(mid)
        inside = jnp.logical_and(mid > lo, mid < hi)
        up = jnp.logical_and(inside, cnt >= TOPK_MAX)
        dn = jnp.logical_and(inside, cnt < TOPK_MAX)
        return jnp.where(up, mid, lo), jnp.where(dn, mid, hi), jnp.where(up, cnt, cnt_lo)

    lo0 = _as_exact_f32(jnp.min(rng_ref[0], axis=0, keepdims=True))
    hi0 = _key_to_f32(_f32_to_key(jnp.max(rng_ref[1], axis=0, keepdims=True)) + 1)
    lo, hi, cnt_at = lax.fori_loop(0, VALUE_PASSES, value_pass, (lo0, hi0, n_adm))

    def unresolved(lo, hi, cnt_lo):
        gap = _f32_to_key(hi) - _f32_to_key(lo)
        open_ = jnp.logical_and(searching, jnp.logical_and(cnt_lo != TOPK_MAX, gap != 1))
        return jnp.where(open_, jnp.int32(1), jnp.int32(0))

    def key_cond(carry):
        return jnp.max(carry[3]) > 0

    def key_pass(carry):
        lo, hi, cnt_lo, open_ = carry
        klo = _f32_to_key(lo)
        kmid = klo + lax.shift_right_logical(_f32_to_key(hi) - klo, jnp.int32(1))
        mid = _key_to_f32(kmid)
        cnt = count_ge(mid)
        up = jnp.logical_and(open_ > 0, cnt >= TOPK_MAX)
        dn = jnp.logical_and(open_ > 0, cnt < TOPK_MAX)
        lo, hi, cnt_lo = jnp.where(up, mid, lo), jnp.where(dn, mid, hi), jnp.where(up, cnt, cnt_lo)
        return lo, hi, cnt_lo, unresolved(lo, hi, cnt_lo)

    lo, hi, cnt_at, _ = lax.while_loop(
        key_cond, key_pass, (lo, hi, cnt_at, unresolved(lo, hi, cnt_at)))
    tau = jnp.where(searching, lo, F32_LOWEST)
    cnt_at = jnp.where(searching, cnt_at, TOPK_MAX)

    @pl.when(jnp.max(cnt_at) > TOPK_MAX)
    def _():
        def pos_of(c):
            return lax.broadcasted_iota(I32, (TK, TQ), 0) + c * TK

        need = TOPK_MAX - count_where(lambda st, c: st > tau)
        n_bits = int(score_ref.shape[0] * TK).bit_length()

        def idx_body(t, bound):
            trial = bound | jnp.left_shift(jnp.int32(1), n_bits - 1 - t)
            cnt = count_where(
                lambda st, c: jnp.where(st == tau, pos_of(c), jnp.int32(1 << 30)) < trial)
            return jnp.where(cnt <= need, trial, bound)

        bound = lax.fori_loop(0, n_bits, idx_body, jnp.zeros((1, TQ), I32))
        below = _key_to_f32(_f32_to_key(tau) - 1)

        def demote(c, carry):
            st = score_ref[c]
            drop = jnp.where(st == tau, pos_of(c), jnp.int32(-1)) >= bound
            score_ref[c] = jnp.where(drop, below, st)
            return carry

        lax.fori_loop(0, nkv, demote, 0)

    n_stage = A_HEADS
    dsa_grp = B_HEADS // n_stage
    a_dv = 2 * A_DH
    a_rows_v = a_dv + V_PAD
    n_chain = 2 + dsa_grp

    def qk_chain(h, c, kstart):
        if c < 2:
            kh = akp_ref[0, pl.ds(kstart, TK), h * 256:(h + 1) * 256]
            qh = qa_ref[0, 0, 2 * h + c]
        else:
            kh = bkp_ref[0, pl.ds(kstart, TK), :]
            qh = qb_ref[0, 0, dsa_grp * h + c - 2]
        return jnp.dot(kh, qh, preferred_element_type=F32)

    def carry_ref(c):
        return (sa_ref, c * TQ) if c < 2 else (sb_ref, (c - 2) * TQ)

    def attend_block(j, diag):
        k0 = pl.multiple_of(j * TK, TK)
        kn0 = pl.multiple_of(jnp.minimum(j + 1, nkv - 1) * TK, TK)
        slot = lax.rem(j, 2)
        cap_sel = jnp.where(score_ref[j] >= tau, BIG, NEG)
        if diag:
            _, cap_chunk, corr = diag_geometry(j)
        vb = bv_ref[0, j]
        s_next = []
        for c in range(n_chain):
            ref, off = carry_ref(c)
            s_next.append(ref[slot, :, off:off + TQ])
        for h in range(n_stage):
            vh = av_ref[0, j, h * a_rows_v:(h + 1) * a_rows_v, :]
            if diag:
                corr_a = slopes_a[h] * corr
            for c in range(n_chain):
                s = s_next[c]
                if h + 1 < n_stage:
                    s_next[c] = qk_chain(h + 1, c, k0)
                else:
                    ref, off = carry_ref(c)
                    ref[1 - slot, :, off:off + TQ] = qk_chain(0, c, kn0)
                if c < 2:
                    if diag:
                        s = jnp.minimum(s, cap_chunk) + corr_a
                    p, alpha = _softmax_step(s, m_a_ref, 2 * h + c)
                    _accumulate(vh, p, alpha, acc_a_ref, 2 * h + c)
                else:
                    hb = dsa_grp * h + c - 2
                    s = jnp.minimum(s, cap_sel)
                    if diag:
                        s = s + slopes_b[hb] * corr
                    p, alpha = _softmax_step(s, m_b_ref, hb)
                    _accumulate(vb, p, alpha, acc_b_ref, hb)

    def attend_body(diag, j, carry):
        attend_block(j, diag)
        return carry

    for c in range(n_chain):
        ref, off = carry_ref(c)
        ref[0, :, off:off + TQ] = qk_chain(0, c, 0)
    lax.fori_loop(0, n_off, functools.partial(attend_body, False), 0)
    lax.fori_loop(n_off, nkv, functools.partial(attend_body, True), 0)

    lv = lamv_ref[...]
    lam = (jnp.exp(jnp.sum(lv[0:1] * lv[1:2], axis=1, keepdims=True))
           - jnp.exp(jnp.sum(lv[2:3] * lv[3:4], axis=1, keepdims=True)) + lam_init)
    g = g_ref[...]
    for h in range(A_HEADS):
        o1 = acc_a_ref[2 * h, 0:a_dv, :] / acc_a_ref[2 * h, a_dv:a_dv + 1, :]
        o2 = acc_a_ref[2 * h + 1, 0:a_dv, :] / acc_a_ref[2 * h + 1, a_dv:a_dv + 1, :]
        o = o1 - lam * o2
        y = o * lax.rsqrt(jnp.mean(o * o, axis=0, keepdims=True) + EPS)
        out_ref[0, 0, h * 128:(h + 1) * 128, :] = ((y * g) * (1.0 - lam_init)).astype(BF16)
    a_rows = A_HEADS * 2 * A_DH
    for h in range(B_HEADS):
        ob = acc_b_ref[h, 0:B_DH, :] / acc_b_ref[h, B_DH:B_DH + 1, :]
        out_ref[0, 0, a_rows + h * B_DH:a_rows + (h + 1) * B_DH, :] = ob.astype(BF16)


def _attention(lam_init, qat, qbt, qit, iwt, akp, avt, bkp, ikk, bvt, lamv, gt):
    bsz, s, _ = akp.shape
    nq = s // TQ
    nk = s // TK
    d_mix = A_HEADS * 2 * A_DH + B_HEADS * B_DH

    def per_query_block(arr):
        shape = (1, 1) + arr.shape[2:]
        return pl.BlockSpec(shape, lambda b, i: (b, i) + (0,) * (len(shape) - 2))

    def per_batch(shape):
        return pl.BlockSpec(shape, lambda b, i: (b,) + (0,) * (len(shape) - 1))

    in_specs = [
        per_query_block(qat),
        per_query_block(qbt),
        per_query_block(qit),
        pl.BlockSpec((1, TQ // TK, IDX_HEADS, TK), lambda b, i: (b, i, 0, 0)),
        per_batch((1, s, akp.shape[2])),
        per_batch((1, nk, avt.shape[2], TK)),
        per_batch((1, s, 128)),
        per_batch((1, s, 128)),
        per_batch((1, nk, bvt.shape[2], TK)),
        _const_spec(lamv.shape),
        _const_spec(gt.shape),
    ]
    n_maps = 2 * A_HEADS
    scratch = [
        pltpu.VMEM((nk, TK, TQ), F32),
        pltpu.VMEM((2, 8, TQ), F32),
        pltpu.VMEM((2, TK, 2 * TQ), F32),
        pltpu.VMEM((2, TK, B_HEADS // A_HEADS * TQ), F32),
        pltpu.VMEM((IDX_HEADS, TQ), F32),
        pltpu.VMEM((n_maps, 2 * A_DH + V_PAD, TQ), F32),
        pltpu.VMEM((n_maps, 1, TQ), F32),
        pltpu.VMEM((B_HEADS, B_DH + V_PAD, TQ), F32),
        pltpu.VMEM((B_HEADS, 1, TQ), F32),
    ]
    return pl.pallas_call(
        functools.partial(_attn_kernel, lam_init),
        grid=(bsz, nq),
        in_specs=in_specs,
        out_specs=pl.BlockSpec((1, 1, d_mix, TQ), lambda b, i: (b, i, 0, 0)),
        out_shape=jax.ShapeDtypeStruct((bsz, nq, d_mix, TQ), BF16),
        scratch_shapes=scratch,
        compiler_params=pltpu.CompilerParams(
            dimension_semantics=("arbitrary", "arbitrary"), vmem_limit_bytes=VMEM_LIMIT_BYTES),
        name="attention",
    )(qat, qbt, qit, iwt, akp, avt, bkp, ikk, bvt, lamv, gt)


def _ffn_kernel(final, mix_ref, x_ref, mod_ref, n2_ref, fg_ref, wo_ref, w1_ref, w2_ref, o_ref):
    wo = wo_ref[...]
    y = jnp.concatenate(
        [lax.dot_general(mix_ref[0, c], wo, _TN, preferred_element_type=F32)
         for c in range(TM_FFN // TQ)], axis=0)
    g1 = mod_ref[0, 2:3, :]
    sh2 = mod_ref[0, 3:4, :]
    sc2 = mod_ref[0, 4:5, :]
    g2 = mod_ref[0, 5:6, :]
    x1 = x_ref[0] + g1 * y
    h2 = ((_rms(x1) * n2_ref[...]) * (1.0 + sc2) + sh2).astype(BF16)
    ff = jnp.zeros_like(x1)
    for c in range(w1_ref.shape[1] // FF_CHUNK):
        u = jnp.dot(h2, w1_ref[:, c * FF_CHUNK:(c + 1) * FF_CHUNK], preferred_element_type=F32)
        u = jnp.square(jnp.maximum(u, 0.0)).astype(BF16)
        ff = ff + jnp.dot(u, w2_ref[c * FF_CHUNK:(c + 1) * FF_CHUNK, :], preferred_element_type=F32)
    x2 = x1 + g2 * ff
    if final:
        x2 = _rms(x2) * fg_ref[...]
    o_ref[0] = x2


def _out_ffn(final, mixt, x, mod, n2g, fg, wo, w1, w2):
    bsz, s, d = x.shape
    d_mix = mixt.shape[2]
    tm = TM_FFN
    cpt = tm // TQ
    return pl.pallas_call(
        functools.partial(_ffn_kernel, final),
        grid=(bsz, s // tm),
        in_specs=[
            pl.BlockSpec((1, cpt, d_mix, TQ), lambda b, t: (b, t, 0, 0)),
            pl.BlockSpec((1, tm, d), lambda b, t: (b, t, 0)),
            pl.BlockSpec((1, 6, d), lambda b, t: (b, 0, 0)),
            _const_spec(n2g.shape),
            _const_spec(fg.shape),
            _const_spec(wo.shape),
            _const_spec(w1.shape),
            _const_spec(w2.shape),
        ],
        out_specs=pl.BlockSpec((1, tm, d), lambda b, t: (b, t, 0)),
        out_shape=jax.ShapeDtypeStruct((bsz, s, d), F32),
        compiler_params=pltpu.CompilerParams(
            dimension_semantics=("arbitrary", "arbitrary"), vmem_limit_bytes=VMEM_LIMIT_BYTES),
        name="out_ffn",
    )(mixt, x, mod, n2g, fg, wo, w1, w2)


def _split_w_in(w):
    d = w.shape[0]
    a = A_HEADS * 2 * A_DH
    sizes = (a, a, a, B_HEADS * B_DH, B_DH, B_DH, IDX_HEADS * IDX_DH, IDX_DH, IDX_HEADS)
    offs = np.cumsum((0,) + sizes)
    aq, ak, av, bq, bk, bv, iq, ik, iw = [w[:, offs[n]:offs[n + 1]] for n in range(len(sizes))]
    kk = jnp.concatenate([bk, jnp.zeros((d, 128 - B_DH), w.dtype), ik, ik], axis=1)
    wt = jnp.concatenate([aq, bq, iq, av, bv, iw], axis=1).T
    return [m.astype(BF16) for m in (ak, kk, wt)]


def kernel(x, c, norm1_g, norm2_g, w_ada, b_ada, w_in, lam_q1, lam_k1, lam_q2, lam_k2,
           subln_g, w_out, w_ff1, w_ff2, final_g):
    bsz, s, d = x.shape
    depth = w_in.shape[0]
    assert s % TS_PROJ == 0 and s % TM_FFN == 0 and s % TQ == 0 and TQ % TK == 0 and TM_FFN % TQ == 0
    assert min(TOPK_MAX, s // 4) == TOPK_MAX and s // POS_SPLIT <= 256 and CHUNK == 1 << CHUNK_SHIFT
    fg = final_g.reshape(1, d)
    for l in range(depth):
        lam_init = 0.8 - 0.6 * math.exp(-0.3 * l)
        mod = _adaln_mod(c, w_ada[l].astype(BF16), b_ada[l].reshape(1, -1)).reshape(bsz, 6, d)
        qat, qbt, qit, akp, bkp, ikk, avt, bvt, iwt = _in_proj(
            x, mod, norm1_g[l].reshape(1, d), _split_w_in(w_in[l]))
        lamv = jnp.stack([lam_q1[l], lam_k1[l], lam_q2[l], lam_k2[l]])
        gt = jnp.broadcast_to(subln_g[l][:, None], (2 * A_DH, TQ))
        mixt = _attention(lam_init, qat, qbt, qit, iwt, akp, avt, bkp, ikk, bvt, lamv, gt)
        x = _out_ffn(l == depth - 1, mixt, x, mod, norm2_g[l].reshape(1, d), fg,
                     w_out[l].astype(BF16), w_ff1[l].astype(BF16), w_ff2[l].astype(BF16))
    return x
```

```python
import functools
import math

import jax
import jax.numpy as jnp
import numpy as np
from jax import lax
from jax.experimental import pallas as pl
from jax.experimental.pallas import tpu as pltpu

CHUNK = 64
CHUNK_SHIFT = 6
A_HEADS = 4
A_DH = 64
B_HEADS = 8
B_DH = 64
IDX_HEADS = 16
IDX_DH = 64
TOPK_MAX = 256
EPS = 1e-6

TQ = 256
TK = 256
TS_PROJ = 512
TM_FFN = 512
FF_CHUNK = 1024
VMEM_LIMIT_BYTES = 56 * 1024 * 1024
IDX_UNROLL = 2
VALUE_PASSES = 18
V_PAD = 16
POS_SHIFT = 6
POS_SPLIT = 1 << POS_SHIFT

LOG2E = math.log2(math.e)
NEG = -1e30
BIG = 3e38
F32_LOWEST = float(np.finfo(np.float32).min)
BF16 = jnp.bfloat16
F32 = jnp.float32
I32 = jnp.int32

_NT = (((1,), (1,)), ((), ()))
_TN = (((0,), (0,)), ((), ()))


def _alibi_slopes(n):
    return [2.0 ** (-8.0 * (i + 1) / n) for i in range(n)]


def _slope_columns(slope):
    pieces, rest = [], slope * LOG2E
    for _ in range(3):
        p = float(np.asarray(rest, np.float32).astype(BF16).astype(np.float32))
        pieces.append(p)
        rest -= p
    return [p * POS_SPLIT for p in pieces] + pieces


def _const_spec(shape):
    nd = len(shape)
    return pl.BlockSpec(shape, lambda *_: (0,) * nd, pipeline_mode=pl.Buffered(1))


def _rms(x):
    return x * lax.rsqrt(jnp.mean(x * x, axis=-1, keepdims=True) + EPS)


def _mod_kernel(c_ref, w_ref, b_ref, o_ref):
    c = c_ref[...]
    cond = c * (1.0 / (1.0 + jnp.exp(-c)))
    o_ref[...] = jnp.dot(cond.astype(BF16), w_ref[...], preferred_element_type=F32) + b_ref[...]


def _adaln_mod(c, w_ada, b_ada):
    bsz, d = c.shape
    n = w_ada.shape[1]
    return pl.pallas_call(
        _mod_kernel,
        grid=(n // d,),
        in_specs=[
            pl.BlockSpec((bsz, d), lambda j: (0, 0)),
            pl.BlockSpec((d, d), lambda j: (0, j)),
            pl.BlockSpec((1, d), lambda j: (0, j)),
        ],
        out_specs=pl.BlockSpec((bsz, d), lambda j: (0, j)),
        out_shape=jax.ShapeDtypeStruct((bsz, n), F32),
        compiler_params=pltpu.CompilerParams(dimension_semantics=("arbitrary",)),
        name="adaln_mod",
    )(c, w_ada, b_ada)


def _row_constants(rows, lanes, values):
    r = lax.broadcasted_iota(I32, (rows, lanes), 0)
    out = jnp.zeros((rows, lanes), F32)
    for k, v in enumerate(values):
        out = jnp.where(r == k, v, out)
    return out


def _inproj_kernel(x_ref, mod_ref, g_ref, wak_ref, wkk_ref, wt_ref,
                   qa_ref, qb_ref, qi_ref, akp_ref, bkp_ref, ikk_ref, av_ref, bv_ref, iw_ref):
    x = x_ref[0]
    sh1 = mod_ref[0, 0:1, :]
    sc1 = mod_ref[0, 1:2, :]
    h = (_rms(x) * g_ref[...]) * (1.0 + sc1) + sh1
    hb = h.astype(BF16)

    pos = pl.program_id(1) * TS_PROJ + lax.broadcasted_iota(I32, (TS_PROJ, 128), 0)
    lane = lax.broadcasted_iota(I32, (TS_PROJ, 128), 1)
    pos_hi = (pos >> POS_SHIFT).astype(F32)
    pos_lo = (pos & (POS_SPLIT - 1)).astype(F32)

    def pos_tile(l0):
        return jnp.where((lane >= l0) & (lane < l0 + 3), pos_hi,
                         jnp.where((lane >= l0 + 3) & (lane < l0 + 6), pos_lo, 0.0))

    ak = jnp.dot(hb, wak_ref[...], preferred_element_type=F32)
    pos0 = pos_tile(0).astype(BF16)
    for hd in range(A_HEADS):
        akp_ref[0, :, hd * 256:hd * 256 + 128] = ak[:, hd * 128:(hd + 1) * 128].astype(BF16)
        akp_ref[0, :, hd * 256 + 128:(hd + 1) * 256] = pos0
    kk = jnp.dot(hb, wkk_ref[...], preferred_element_type=F32)
    bkp_ref[0] = (kk[:, 0:128] + pos_tile(B_DH)).astype(BF16)
    ikk_ref[0] = kk[:, 128:256].astype(BF16)

    t_all = lax.dot_general(wt_ref[...], hb, _NT, preferred_element_type=F32)
    a_cols = A_HEADS * 2 * A_DH
    b_cols = B_HEADS * B_DH
    i_cols = IDX_HEADS * IDX_DH
    r_bq = a_cols
    r_iq = r_bq + b_cols
    r_av = r_iq + i_cols
    r_bv = r_av + a_cols
    r_iw = r_bv + B_DH
    aqt = (t_all[0:r_bq] * (A_DH ** -0.5 * LOG2E)).astype(BF16)
    bqt = (t_all[r_bq:r_iq] * (B_DH ** -0.5 * LOG2E)).astype(BF16)
    iqt = t_all[r_iq:r_av].astype(BF16)
    avt = t_all[r_av:r_bv].astype(BF16)
    bvt = t_all[r_bv:r_iw].astype(BF16)
    iwt = t_all[r_iw:r_iw + IDX_HEADS]

    zeros = jnp.zeros((A_DH, TQ), BF16)
    idx_grp = IDX_HEADS // A_HEADS
    slopes_a = _alibi_slopes(A_HEADS)
    slopes_b = _alibi_slopes(B_HEADS)
    for c in range(TS_PROJ // TQ):
        cols = slice(c * TQ, (c + 1) * TQ)
        for hd in range(A_HEADS):
            slope_rows = _row_constants(128, TQ, _slope_columns(slopes_a[hd])).astype(BF16)
            for mp in range(2):
                r0 = hd * 2 * A_DH + mp * A_DH
                qa_ref[0, c, 2 * hd + mp, mp * A_DH:(mp + 1) * A_DH, :] = aqt[r0:r0 + A_DH, cols]
                qa_ref[0, c, 2 * hd + mp, (1 - mp) * A_DH:(2 - mp) * A_DH, :] = zeros
                qa_ref[0, c, 2 * hd + mp, 2 * A_DH:, :] = slope_rows
        for hd in range(B_HEADS):
            qb_ref[0, c, hd, 0:B_DH, :] = bqt[hd * B_DH:(hd + 1) * B_DH, cols]
            qb_ref[0, c, hd, B_DH:, :] = _row_constants(
                128 - B_DH, TQ, _slope_columns(slopes_b[hd])).astype(BF16)
        for hd in range(IDX_HEADS):
            g, k = divmod(hd, idx_grp)
            qi_ref[0, c, g, 0:IDX_DH, k * TQ:(k + 1) * TQ] = iqt[hd * IDX_DH:(hd + 1) * IDX_DH, cols]
            qi_ref[0, c, g, IDX_DH:, k * TQ:(k + 1) * TQ] = zeros

    ones_rows = jnp.where(lax.broadcasted_iota(I32, (V_PAD, TK), 0) == 0, 1.0, 0.0).astype(BF16)
    a_dv = 2 * A_DH
    for c in range(TS_PROJ // TK):
        for hd in range(A_HEADS):
            r0 = hd * (a_dv + V_PAD)
            av_ref[0, c, r0:r0 + a_dv, :] = avt[hd * a_dv:(hd + 1) * a_dv, c * TK:(c + 1) * TK]
            av_ref[0, c, r0 + a_dv:r0 + a_dv + V_PAD, :] = ones_rows
        bv_ref[0, c, 0:B_DH, :] = bvt[:, c * TK:(c + 1) * TK]
        bv_ref[0, c, B_DH:B_DH + V_PAD, :] = ones_rows
        iw_ref[0, c] = iwt[:, c * TK:(c + 1) * TK]


def _in_proj(x, mod, g, ws):
    bsz, s, d = x.shape
    ts = TS_PROJ
    nk = s // TK
    nq = s // TQ
    cpt = ts // TK
    cpq = ts // TQ

    def tok(cols):
        return pl.BlockSpec((1, ts, cols), lambda b, t: (b, t, 0))

    def trs(rows):
        return pl.BlockSpec((1, cpt, rows, TK), lambda b, t: (b, t, 0, 0))

    def qop(n, rows, lanes):
        return pl.BlockSpec((1, cpq, n, rows, lanes), lambda b, t: (b, t, 0, 0, 0))

    idx_grp = IDX_HEADS // A_HEADS
    out_shape = [
        jax.ShapeDtypeStruct((bsz, nq, 2 * A_HEADS, 256, TQ), BF16),
        jax.ShapeDtypeStruct((bsz, nq, B_HEADS, 128, TQ), BF16),
        jax.ShapeDtypeStruct((bsz, nq, A_HEADS, 128, idx_grp * TQ), BF16),
        jax.ShapeDtypeStruct((bsz, s, A_HEADS * 256), BF16),
        jax.ShapeDtypeStruct((bsz, s, 128), BF16),
        jax.ShapeDtypeStruct((bsz, s, 128), BF16),
        jax.ShapeDtypeStruct((bsz, nk, A_HEADS * (2 * A_DH + V_PAD), TK), BF16),
        jax.ShapeDtypeStruct((bsz, nk, B_DH + V_PAD, TK), BF16),
        jax.ShapeDtypeStruct((bsz, nk, IDX_HEADS, TK), F32),
    ]
    out_specs = [qop(2 * A_HEADS, 256, TQ), qop(B_HEADS, 128, TQ), qop(A_HEADS, 128, idx_grp * TQ),
                 tok(A_HEADS * 256), tok(128), tok(128),
                 trs(A_HEADS * (2 * A_DH + V_PAD)), trs(B_DH + V_PAD), trs(IDX_HEADS)]
    in_specs = [
        pl.BlockSpec((1, ts, d), lambda b, t: (b, t, 0)),
        pl.BlockSpec((1, 6, d), lambda b, t: (b, 0, 0)),
        _const_spec(g.shape),
    ] + [_const_spec(w.shape) for w in ws]
    return pl.pallas_call(
        _inproj_kernel,
        grid=(bsz, s // ts),
        in_specs=in_specs,
        out_specs=out_specs,
        out_shape=out_shape,
        compiler_params=pltpu.CompilerParams(
            dimension_semantics=("arbitrary", "arbitrary"), vmem_limit_bytes=VMEM_LIMIT_BYTES),
        name="in_proj",
    )(x, mod, g, *ws)


def _softmax_step(s, m_ref, idx):
    m_prev = m_ref[idx]
    m_new = jnp.maximum(m_prev, jnp.max(s, axis=0, keepdims=True))
    m_ref[idx] = m_new
    return jnp.exp2(s - m_new).astype(BF16), jnp.exp2(m_prev - m_new)


def _accumulate(v, p, alpha, acc_ref, idx):
    acc_ref[idx] = alpha * acc_ref[idx] + jnp.dot(v, p, preferred_element_type=F32)


def _key_to_f32(key):
    return pltpu.bitcast(key ^ ((key >> 31) & jnp.int32(0x7FFFFFFF)), F32)


def _f32_to_key(x):
    bits = pltpu.bitcast(x, I32)
    return bits ^ ((bits >> 31) & jnp.int32(0x7FFFFFFF))


def _as_exact_f32(x):
    return _key_to_f32(_f32_to_key(x))


def _attn_kernel(lam_init, qa_ref, qb_ref, qi_ref, iw_ref, akp_ref, av_ref, bkp_ref, ikk_ref,
                 bv_ref, lamv_ref, g_ref, out_ref,
                 score_ref, rng_ref, sa_ref, sb_ref, w_ref,
                 acc_a_ref, m_a_ref, acc_b_ref, m_b_ref):
    i = pl.program_id(1)
    n_off = i * (TQ // TK)
    nkv = n_off + TQ // TK
    slopes_a = _alibi_slopes(A_HEADS)
    slopes_b = _alibi_slopes(B_HEADS)

    idx_grp = IDX_HEADS // A_HEADS
    for c in range(TQ // TK):
        w_ref[:, c * TK:(c + 1) * TK] = iw_ref[0, c] * ((IDX_HEADS * IDX_DH) ** -0.5)

    acc_a_ref[...] = jnp.zeros_like(acc_a_ref)
    acc_b_ref[...] = jnp.zeros_like(acc_b_ref)
    m_a_ref[...] = jnp.full_like(m_a_ref, NEG)
    m_b_ref[...] = jnp.full_like(m_b_ref, NEG)

    def diag_geometry(j):
        row = j * TK + lax.broadcasted_iota(I32, (TK, TQ), 0)
        col = i * TQ + lax.broadcasted_iota(I32, (TK, TQ), 1)
        allowed = row < ((col >> CHUNK_SHIFT) + 1) * CHUNK
        corr = (2.0 * LOG2E) * jnp.minimum((col - row).astype(F32), 0.0)
        return allowed, jnp.where(allowed, BIG, NEG), corr

    def index_block(j, diag):
        k0 = pl.multiple_of(j * TK, TK)
        kk = ikk_ref[0, pl.ds(k0, TK), :]
        score = None
        for g in range(IDX_HEADS // idx_grp):
            xg = jnp.dot(kk, qi_ref[0, 0, g], preferred_element_type=F32)
            for k in range(idx_grp):
                hh = g * idx_grp + k
                term = w_ref[hh:hh + 1, :] * jnp.maximum(xg[:, k * TQ:(k + 1) * TQ], 0.0)
                score = term if score is None else score + term
        lo_src = score
        if diag:
            allowed, _, _ = diag_geometry(j)
            lo_src = jnp.where(allowed, score, jnp.inf)
            score = jnp.where(allowed, score, -jnp.inf)
        score_ref[j] = score
        rng_ref[0] = jnp.minimum(rng_ref[0], jnp.min(lo_src.reshape(TK // 8, 8, TQ), axis=0))
        rng_ref[1] = jnp.maximum(rng_ref[1], jnp.max(score.reshape(TK // 8, 8, TQ), axis=0))

    def index_body(diag, j, carry):
        index_block(j, diag)
        return carry

    def index_group_body(jj, carry):
        for u in range(IDX_UNROLL):
            index_block(IDX_UNROLL * jj + u, False)
        return carry

    rng_ref[0] = jnp.full((8, TQ), jnp.inf, F32)
    rng_ref[1] = jnp.full((8, TQ), -jnp.inf, F32)
    lax.fori_loop(0, n_off // IDX_UNROLL, index_group_body, 0)
    lax.fori_loop(n_off // IDX_UNROLL * IDX_UNROLL, n_off, functools.partial(index_body, False), 0)
    lax.fori_loop(n_off, nkv, functools.partial(index_body, True), 0)

    qpos = i * TQ + lax.broadcasted_iota(I32, (1, TQ), 1)
    n_adm = ((qpos >> CHUNK_SHIFT) + 1) * CHUNK
    searching = n_adm > TOPK_MAX

    def count_where(pred_fn):
        def body(c, cnt):
            ind = jnp.where(pred_fn(score_ref[c], c), jnp.int32(1), jnp.int32(0))
            return cnt + jnp.sum(ind.reshape(TK // 8, 8, TQ), axis=0)
        cnt8 = lax.fori_loop(0, nkv, body, jnp.zeros((8, TQ), I32))
        return jnp.sum(cnt8, axis=0, keepdims=True)

    n_acc = 4

    def count_ge(cand):
        def body(c, accs):
            accs = list(accs)
            for r in range(TK // 8):
                a = accs[r % n_acc]
                accs[r % n_acc] = jnp.where(score_ref[c, r * 8:(r + 1) * 8, :] >= cand, a + 1, a)
            return tuple(accs)
        accs = lax.fori_loop(0, nkv, body,
                             tuple(jnp.zeros((8, TQ), I32) for _ in range(n_acc)))
        return jnp.sum(functools.reduce(lambda a, b: a + b, accs), axis=0, keepdims=True)

    def value_pass(_, carry):
        lo, hi, cnt_lo = carry
        mid = _as_exact_f32(0.5 * lo + 0.5 * hi)
        cnt = count_ge(mid)
        inside = jnp.logical_and(mid > lo, mid < hi)
        up = jnp.logical_and(inside, cnt >= TOPK_MAX)
        dn = jnp.logical_and(inside, cnt < TOPK_MAX)
        return jnp.where(up, mid, lo), jnp.where(dn, mid, hi), jnp.where(up, cnt, cnt_lo)

    lo0 = _as_exact_f32(jnp.min(rng_ref[0], axis=0, keepdims=True))
    hi0 = _key_to_f32(_f32_to_key(jnp.max(rng_ref[1], axis=0, keepdims=True)) + 1)
    lo, hi, cnt_at = lax.fori_loop(0, VALUE_PASSES, value_pass, (lo0, hi0, n_adm))

    def status(lo, hi, cnt_lo):
        gap = _f32_to_key(hi) - _f32_to_key(lo)
        open_ = jnp.logical_and(searching, jnp.logical_and(cnt_lo != TOPK_MAX, gap != 1))
        open_ = jnp.where(open_, jnp.int32(1), jnp.int32(0))
        tied = jnp.where(jnp.logical_and(searching, cnt_lo > TOPK_MAX), jnp.int32(1), jnp.int32(0))
        return open_, jnp.max(open_), jnp.max(tied)

    def key_cond(carry):
        return carry[4] > 0

    def key_pass(carry):
        lo, hi, cnt_lo, open_ = carry[:4]
        klo = _f32_to_key(lo)
        kmid = klo + lax.shift_right_logical(_f32_to_key(hi) - klo, jnp.int32(1))
        mid = _key_to_f32(kmid)
        cnt = count_ge(mid)
        up = jnp.logical_and(open_ > 0, cnt >= TOPK_MAX)
        dn = jnp.logical_and(open_ > 0, cnt < TOPK_MAX)
        lo, hi, cnt_lo = jnp.where(up, mid, lo), jnp.where(dn, mid, hi), jnp.where(up, cnt, cnt_lo)
        return (lo, hi, cnt_lo) + status(lo, hi, cnt_lo)

    lo, _, _, _, _, any_tied = lax.while_loop(
        key_cond, key_pass, (lo, hi, cnt_at) + status(lo, hi, cnt_at))
    tau = jnp.where(searching, lo, F32_LOWEST)

    @pl.when(any_tied > 0)
    def _():
        def pos_of(c):
            return lax.broadcasted_iota(I32, (TK, TQ), 0) + c * TK

        need = TOPK_MAX - count_where(lambda st, c: st > tau)
        n_bits = int(score_ref.shape[0] * TK).bit_length()

        def idx_body(t, bound):
            trial = bound | jnp.left_shift(jnp.int32(1), n_bits - 1 - t)
            cnt = count_where(
                lambda st, c: jnp.where(st == tau, pos_of(c), jnp.int32(1 << 30)) < trial)
            return jnp.where(cnt <= need, trial, bound)

        bound = lax.fori_loop(0, n_bits, idx_body, jnp.zeros((1, TQ), I32))
        below = _key_to_f32(_f32_to_key(tau) - 1)

        def demote(c, carry):
            st = score_ref[c]
            drop = jnp.where(st == tau, pos_of(c), jnp.int32(-1)) >= bound
            score_ref[c] = jnp.where(drop, below, st)
            return carry

        lax.fori_loop(0, nkv, demote, 0)

    n_stage = A_HEADS
    dsa_grp = B_HEADS // n_stage
    a_dv = 2 * A_DH
    a_rows_v = a_dv + V_PAD
    n_chain = 2 + dsa_grp

    def qk_chain(h, c, kstart):
        if c < 2:
            kh = akp_ref[0, pl.ds(kstart, TK), h * 256:(h + 1) * 256]
            qh = qa_ref[0, 0, 2 * h + c]
        else:
            kh = bkp_ref[0, pl.ds(kstart, TK), :]
            qh = qb_ref[0, 0, dsa_grp * h + c - 2]
        return jnp.dot(kh, qh, preferred_element_type=F32)

    def carry_ref(c):
        return (sa_ref, c * TQ) if c < 2 else (sb_ref, (c - 2) * TQ)

    def attend_block(j, diag):
        k0 = pl.multiple_of(j * TK, TK)
        kn0 = pl.multiple_of(jnp.minimum(j + 1, nkv - 1) * TK, TK)
        slot = lax.rem(j, 2)
        cap_sel = jnp.where(score_ref[j] >= tau, BIG, NEG)
        if diag:
            _, cap_chunk, corr = diag_geometry(j)
        vb = bv_ref[0, j]
        s_next = []
        for c in range(n_chain):
            ref, off = carry_ref(c)
            s_next.append(ref[slot, :, off:off + TQ])
        for h in range(n_stage):
            vh = av_ref[0, j, h * a_rows_v:(h + 1) * a_rows_v, :]
            if diag:
                corr_a = slopes_a[h] * corr
            for c in range(n_chain):
                s = s_next[c]
                if h + 1 < n_stage:
                    s_next[c] = qk_chain(h + 1, c, k0)
                else:
                    ref, off = carry_ref(c)
                    ref[1 - slot, :, off:off + TQ] = qk_chain(0, c, kn0)
                if c < 2:
                    if diag:
                        s = jnp.minimum(s, cap_chunk) + corr_a
                    p, alpha = _softmax_step(s, m_a_ref, 2 * h + c)
                    _accumulate(vh, p, alpha, acc_a_ref, 2 * h + c)
                else:
                    hb = dsa_grp * h + c - 2
                    s = jnp.minimum(s, cap_sel)
                    if diag:
                        s = s + slopes_b[hb] * corr
                    p, alpha = _softmax_step(s, m_b_ref, hb)
                    _accumulate(vb, p, alpha, acc_b_ref, hb)

    def attend_body(diag, j, carry):
        attend_block(j, diag)
        return carry

    for c in range(n_chain):
        ref, off = carry_ref(c)
        ref[0, :, off:off + TQ] = qk_chain(0, c, 0)
    lax.fori_loop(0, n_off, functools.partial(attend_body, False), 0)
    lax.fori_loop(n_off, nkv, functools.partial(attend_body, True), 0)

    lv = lamv_ref[...]
    lam = (jnp.exp(jnp.sum(lv[0:1] * lv[1:2], axis=1, keepdims=True))
           - jnp.exp(jnp.sum(lv[2:3] * lv[3:4], axis=1, keepdims=True)) + lam_init)
    g = g_ref[...]
    for h in range(A_HEADS):
        o1 = acc_a_ref[2 * h, 0:a_dv, :] / acc_a_ref[2 * h, a_dv:a_dv + 1, :]
        o2 = acc_a_ref[2 * h + 1, 0:a_dv, :] / acc_a_ref[2 * h + 1, a_dv:a_dv + 1, :]
        o = o1 - lam * o2
        y = o * lax.rsqrt(jnp.mean(o * o, axis=0, keepdims=True) + EPS)
        out_ref[0, 0, h * 128:(h + 1) * 128, :] = ((y * g) * (1.0 - lam_init)).astype(BF16)
    a_rows = A_HEADS * 2 * A_DH
    for h in range(B_HEADS):
        ob = acc_b_ref[h, 0:B_DH, :] / acc_b_ref[h, B_DH:B_DH + 1, :]
        out_ref[0, 0, a_rows + h * B_DH:a_rows + (h + 1) * B_DH, :] = ob.astype(BF16)


def _attention(lam_init, qat, qbt, qit, iwt, akp, avt, bkp, ikk, bvt, lamv, gt):
    bsz, s, _ = akp.shape
    nq = s // TQ
    nk = s // TK
    d_mix = A_HEADS * 2 * A_DH + B_HEADS * B_DH

    def per_query_block(arr):
        shape = (1, 1) + arr.shape[2:]
        return pl.BlockSpec(shape, lambda b, i: (b, i) + (0,) * (len(shape) - 2))

    def per_batch(shape):
        return pl.BlockSpec(shape, lambda b, i: (b,) + (0,) * (len(shape) - 1))

    in_specs = [
        per_query_block(qat),
        per_query_block(qbt),
        per_query_block(qit),
        pl.BlockSpec((1, TQ // TK, IDX_HEADS, TK), lambda b, i: (b, i, 0, 0)),
        per_batch((1, s, akp.shape[2])),
        per_batch((1, nk, avt.shape[2], TK)),
        per_batch((1, s, 128)),
        per_batch((1, s, 128)),
        per_batch((1, nk, bvt.shape[2], TK)),
        _const_spec(lamv.shape),
        _const_spec(gt.shape),
    ]
    n_maps = 2 * A_HEADS
    scratch = [
        pltpu.VMEM((nk, TK, TQ), F32),
        pltpu.VMEM((2, 8, TQ), F32),
        pltpu.VMEM((2, TK, 2 * TQ), F32),
        pltpu.VMEM((2, TK, B_HEADS // A_HEADS * TQ), F32),
        pltpu.VMEM((IDX_HEADS, TQ), F32),
        pltpu.VMEM((n_maps, 2 * A_DH + V_PAD, TQ), F32),
        pltpu.VMEM((n_maps, 1, TQ), F32),
        pltpu.VMEM((B_HEADS, B_DH + V_PAD, TQ), F32),
        pltpu.VMEM((B_HEADS, 1, TQ), F32),
    ]
    return pl.pallas_call(
        functools.partial(_attn_kernel, lam_init),
        grid=(bsz, nq),
        in_specs=in_specs,
        out_specs=pl.BlockSpec((1, 1, d_mix, TQ), lambda b, i: (b, i, 0, 0)),
        out_shape=jax.ShapeDtypeStruct((bsz, nq, d_mix, TQ), BF16),
        scratch_shapes=scratch,
        compiler_params=pltpu.CompilerParams(
            dimension_semantics=("arbitrary", "arbitrary"), vmem_limit_bytes=VMEM_LIMIT_BYTES),
        name="attention",
    )(qat, qbt, qit, iwt, akp, avt, bkp, ikk, bvt, lamv, gt)


def _ffn_kernel(final, mix_ref, x_ref, mod_ref, n2_ref, fg_ref, wo_ref, w1_ref, w2_ref, o_ref):
    wo = wo_ref[...]
    y = jnp.concatenate(
        [lax.dot_general(mix_ref[0, c], wo, _TN, preferred_element_type=F32)
         for c in range(TM_FFN // TQ)], axis=0)
    g1 = mod_ref[0, 2:3, :]
    sh2 = mod_ref[0, 3:4, :]
    sc2 = mod_ref[0, 4:5, :]
    g2 = mod_ref[0, 5:6, :]
    x1 = x_ref[0] + g1 * y
    h2 = ((_rms(x1) * n2_ref[...]) * (1.0 + sc2) + sh2).astype(BF16)
    ff = jnp.zeros_like(x1)
    for c in range(w1_ref.shape[1] // FF_CHUNK):
        u = jnp.dot(h2, w1_ref[:, c * FF_CHUNK:(c + 1) * FF_CHUNK], preferred_element_type=F32)
        u = jnp.square(jnp.maximum(u, 0.0)).astype(BF16)
        ff = ff + jnp.dot(u, w2_ref[c * FF_CHUNK:(c + 1) * FF_CHUNK, :], preferred_element_type=F32)
    x2 = x1 + g2 * ff
    if final:
        x2 = _rms(x2) * fg_ref[...]
    o_ref[0] = x2


def _out_ffn(final, mixt, x, mod, n2g, fg, wo, w1, w2):
    bsz, s, d = x.shape
    d_mix = mixt.shape[2]
    tm = TM_FFN
    cpt = tm // TQ
    return pl.pallas_call(
        functools.partial(_ffn_kernel, final),
        grid=(bsz, s // tm),
        in_specs=[
            pl.BlockSpec((1, cpt, d_mix, TQ), lambda b, t: (b, t, 0, 0)),
            pl.BlockSpec((1, tm, d), lambda b, t: (b, t, 0)),
            pl.BlockSpec((1, 6, d), lambda b, t: (b, 0, 0)),
            _const_spec(n2g.shape),
            _const_spec(fg.shape),
            _const_spec(wo.shape),
            _const_spec(w1.shape),
            _const_spec(w2.shape),
        ],
        out_specs=pl.BlockSpec((1, tm, d), lambda b, t: (b, t, 0)),
        out_shape=jax.ShapeDtypeStruct((bsz, s, d), F32),
        compiler_params=pltpu.CompilerParams(
            dimension_semantics=("arbitrary", "arbitrary"), vmem_limit_bytes=VMEM_LIMIT_BYTES),
        name="out_ffn",
    )(mixt, x, mod, n2g, fg, wo, w1, w2)


def _split_w_in(w):
    d = w.shape[0]
    a = A_HEADS * 2 * A_DH
    sizes = (a, a, a, B_HEADS * B_DH, B_DH, B_DH, IDX_HEADS * IDX_DH, IDX_DH, IDX_HEADS)
    offs = np.cumsum((0,) + sizes)
    aq, ak, av, bq, bk, bv, iq, ik, iw = [w[:, offs[n]:offs[n + 1]] for n in range(len(sizes))]
    kk = jnp.concatenate([bk, jnp.zeros((d, 128 - B_DH), w.dtype), ik, ik], axis=1)
    wt = jnp.concatenate([aq, bq, iq, av, bv, iw], axis=1).T
    return [m.astype(BF16) for m in (ak, kk, wt)]


def kernel(x, c, norm1_g, norm2_g, w_ada, b_ada, w_in, lam_q1, lam_k1, lam_q2, lam_k2,
           subln_g, w_out, w_ff1, w_ff2, final_g):
    bsz, s, d = x.shape
    depth = w_in.shape[0]
    assert s % TS_PROJ == 0 and s % TM_FFN == 0 and s % TQ == 0 and TQ % TK == 0 and TM_FFN % TQ == 0
    assert min(TOPK_MAX, s // 4) == TOPK_MAX and s // POS_SPLIT <= 256 and CHUNK == 1 << CHUNK_SHIFT
    fg = final_g.reshape(1, d)
    for l in range(depth):
        lam_init = 0.8 - 0.6 * math.exp(-0.3 * l)
        mod = _adaln_mod(c, w_ada[l].astype(BF16), b_ada[l].reshape(1, -1)).reshape(bsz, 6, d)
        qat, qbt, qit, akp, bkp, ikk, avt, bvt, iwt = _in_proj(
            x, mod, norm1_g[l].reshape(1, d), _split_w_in(w_in[l]))
        lamv = jnp.stack([lam_q1[l], lam_k1[l], lam_q2[l], lam_k2[l]])
        gt = jnp.broadcast_to(subln_g[l][:, None], (2 * A_DH, TQ))
        mixt = _attention(lam_init, qat, qbt, qit, iwt, akp, avt, bkp, ikk, bvt, lamv, gt)
        x = _out_ffn(l == depth - 1, mixt, x, mod, norm2_g[l].reshape(1, d), fg,
                     w_out[l].astype(BF16), w_ff1[l].astype(BF16), w_ff2[l].astype(BF16))
    return x
```

```python
import functools
import math

import jax
import jax.numpy as jnp
import numpy as np
from jax import lax
from jax.experimental import pallas as pl
from jax.experimental.pallas import tpu as pltpu

CHUNK = 64
CHUNK_SHIFT = 6
A_HEADS = 4
A_DH = 64
B_HEADS = 8
B_DH = 64
IDX_HEADS = 16
IDX_DH = 64
TOPK_MAX = 256
EPS = 1e-6

TQ = 256
TK = 256
TS_PROJ = 512
TM_FFN = 512
FF_CHUNK = 1024
VMEM_LIMIT_BYTES = 56 * 1024 * 1024
IDX_UNROLL = 2
VALUE_PASSES = 18
V_PAD = 16
POS_SHIFT = 6
POS_SPLIT = 1 << POS_SHIFT

LOG2E = math.log2(math.e)
NEG = -1e30
BIG = 3e38
F32_LOWEST = float(np.finfo(np.float32).min)
BF16 = jnp.bfloat16
F32 = jnp.float32
I32 = jnp.int32

_NT = (((1,), (1,)), ((), ()))
_TN = (((0,), (0,)), ((), ()))


def _alibi_slopes(n):
    return [2.0 ** (-8.0 * (i + 1) / n) for i in range(n)]


def _slope_columns(slope):
    pieces, rest = [], slope * LOG2E
    for _ in range(3):
        p = float(np.asarray(rest, np.float32).astype(BF16).astype(np.float32))
        pieces.append(p)
        rest -= p
    return [p * POS_SPLIT for p in pieces] + pieces


def _const_spec(shape):
    nd = len(shape)
    return pl.BlockSpec(shape, lambda *_: (0,) * nd, pipeline_mode=pl.Buffered(1))


def _rms(x):
    return x * lax.rsqrt(jnp.mean(x * x, axis=-1, keepdims=True) + EPS)


def _mod_kernel(c_ref, w_ref, b_ref, o_ref):
    c = c_ref[...]
    cond = c * (1.0 / (1.0 + jnp.exp(-c)))
    o_ref[...] = jnp.dot(cond.astype(BF16), w_ref[...], preferred_element_type=F32) + b_ref[...]


def _adaln_mod(c, w_ada, b_ada):
    bsz, d = c.shape
    n = w_ada.shape[1]
    return pl.pallas_call(
        _mod_kernel,
        grid=(n // d,),
        in_specs=[
            pl.BlockSpec((bsz, d), lambda j: (0, 0)),
            pl.BlockSpec((d, d), lambda j: (0, j)),
            pl.BlockSpec((1, d), lambda j: (0, j)),
        ],
        out_specs=pl.BlockSpec((bsz, d), lambda j: (0, j)),
        out_shape=jax.ShapeDtypeStruct((bsz, n), F32),
        compiler_params=pltpu.CompilerParams(dimension_semantics=("arbitrary",)),
        name="adaln_mod",
    )(c, w_ada, b_ada)


def _row_constants(rows, lanes, values):
    r = lax.broadcasted_iota(I32, (rows, lanes), 0)
    out = jnp.zeros((rows, lanes), F32)
    for k, v in enumerate(values):
        out = jnp.where(r == k, v, out)
    return out


def _inproj_kernel(x_ref, mod_ref, g_ref, wak_ref, wkk_ref, wt_ref,
                   qa_ref, qb_ref, qi_ref, akp_ref, bkp_ref, ikk_ref, av_ref, bv_ref, iw_ref):
    x = x_ref[0]
    sh1 = mod_ref[0, 0:1, :]
    sc1 = mod_ref[0, 1:2, :]
    h = (_rms(x) * g_ref[...]) * (1.0 + sc1) + sh1
    hb = h.astype(BF16)

    pos = pl.program_id(1) * TS_PROJ + lax.broadcasted_iota(I32, (TS_PROJ, 128), 0)
    lane = lax.broadcasted_iota(I32, (TS_PROJ, 128), 1)
    pos_hi = (pos >> POS_SHIFT).astype(F32)
    pos_lo = (pos & (POS_SPLIT - 1)).astype(F32)

    def pos_tile(l0):
        return jnp.where((lane >= l0) & (lane < l0 + 3), pos_hi,
                         jnp.where((lane >= l0 + 3) & (lane < l0 + 6), pos_lo, 0.0))

    ak = jnp.dot(hb, wak_ref[...], preferred_element_type=F32)
    pos0 = pos_tile(0).astype(BF16)
    for hd in range(A_HEADS):
        akp_ref[0, :, hd * 256:hd * 256 + 128] = ak[:, hd * 128:(hd + 1) * 128].astype(BF16)
        akp_ref[0, :, hd * 256 + 128:(hd + 1) * 256] = pos0
    kk = jnp.dot(hb, wkk_ref[...], preferred_element_type=F32)
    bkp_ref[0] = (kk[:, 0:128] + pos_tile(B_DH)).astype(BF16)
    ikk_ref[0] = kk[:, 128:256].astype(BF16)

    t_all = lax.dot_general(wt_ref[...], hb, _NT, preferred_element_type=F32)
    a_cols = A_HEADS * 2 * A_DH
    b_cols = B_HEADS * B_DH
    i_cols = IDX_HEADS * IDX_DH
    r_bq = a_cols
    r_iq = r_bq + b_cols
    r_av = r_iq + i_cols
    r_bv = r_av + a_cols
    r_iw = r_bv + B_DH
    aqt = (t_all[0:r_bq] * (A_DH ** -0.5 * LOG2E)).astype(BF16)
    bqt = (t_all[r_bq:r_iq] * (B_DH ** -0.5 * LOG2E)).astype(BF16)
    iqt = t_all[r_iq:r_av].astype(BF16)
    avt = t_all[r_av:r_bv].astype(BF16)
    bvt = t_all[r_bv:r_iw].astype(BF16)
    iwt = t_all[r_iw:r_iw + IDX_HEADS]

    zeros = jnp.zeros((A_DH, TQ), BF16)
    idx_grp = IDX_HEADS // A_HEADS
    slopes_a = _alibi_slopes(A_HEADS)
    slopes_b = _alibi_slopes(B_HEADS)
    for c in range(TS_PROJ // TQ):
        cols = slice(c * TQ, (c + 1) * TQ)
        for hd in range(A_HEADS):
            slope_rows = _row_constants(128, TQ, _slope_columns(slopes_a[hd])).astype(BF16)
            for mp in range(2):
                r0 = hd * 2 * A_DH + mp * A_DH
                qa_ref[0, c, 2 * hd + mp, mp * A_DH:(mp + 1) * A_DH, :] = aqt[r0:r0 + A_DH, cols]
                qa_ref[0, c, 2 * hd + mp, (1 - mp) * A_DH:(2 - mp) * A_DH, :] = zeros
                qa_ref[0, c, 2 * hd + mp, 2 * A_DH:, :] = slope_rows
        for hd in range(B_HEADS):
            qb_ref[0, c, hd, 0:B_DH, :] = bqt[hd * B_DH:(hd + 1) * B_DH, cols]
            qb_ref[0, c, hd, B_DH:, :] = _row_constants(
                128 - B_DH, TQ, _slope_columns(slopes_b[hd])).astype(BF16)
        for hd in range(IDX_HEADS):
            g, k = divmod(hd, idx_grp)
            qi_ref[0, c, g, 0:IDX_DH, k * TQ:(k + 1) * TQ] = iqt[hd * IDX_DH:(hd + 1) * IDX_DH, cols]
            qi_ref[0, c, g, IDX_DH:, k * TQ:(k + 1) * TQ] = zeros

    ones_rows = jnp.where(lax.broadcasted_iota(I32, (V_PAD, TK), 0) == 0, 1.0, 0.0).astype(BF16)
    a_dv = 2 * A_DH
    for c in range(TS_PROJ // TK):
        for hd in range(A_HEADS):
            r0 = hd * (a_dv + V_PAD)
            av_ref[0, c, r0:r0 + a_dv, :] = avt[hd * a_dv:(hd + 1) * a_dv, c * TK:(c + 1) * TK]
            av_ref[0, c, r0 + a_dv:r0 + a_dv + V_PAD, :] = ones_rows
        bv_ref[0, c, 0:B_DH, :] = bvt[:, c * TK:(c + 1) * TK]
        bv_ref[0, c, B_DH:B_DH + V_PAD, :] = ones_rows
        iw_ref[0, c] = iwt[:, c * TK:(c + 1) * TK]


def _in_proj(x, mod, g, ws):
    bsz, s, d = x.shape
    ts = TS_PROJ
    nk = s // TK
    nq = s // TQ
    cpt = ts // TK
    cpq = ts // TQ

    def tok(cols):
        return pl.BlockSpec((1, ts, cols), lambda b, t: (b, t, 0))

    def trs(rows):
        return pl.BlockSpec((1, cpt, rows, TK), lambda b, t: (b, t, 0, 0))

    def qop(n, rows, lanes):
        return pl.BlockSpec((1, cpq, n, rows, lanes), lambda b, t: (b, t, 0, 0, 0))

    idx_grp = IDX_HEADS // A_HEADS
    out_shape = [
        jax.ShapeDtypeStruct((bsz, nq, 2 * A_HEADS, 256, TQ), BF16),
        jax.ShapeDtypeStruct((bsz, nq, B_HEADS, 128, TQ), BF16),
        jax.ShapeDtypeStruct((bsz, nq, A_HEADS, 128, idx_grp * TQ), BF16),
        jax.ShapeDtypeStruct((bsz, s, A_HEADS * 256), BF16),
        jax.ShapeDtypeStruct((bsz, s, 128), BF16),
        jax.ShapeDtypeStruct((bsz, s, 128), BF16),
        jax.ShapeDtypeStruct((bsz, nk, A_HEADS * (2 * A_DH + V_PAD), TK), BF16),
        jax.ShapeDtypeStruct((bsz, nk, B_DH + V_PAD, TK), BF16),
        jax.ShapeDtypeStruct((bsz, nk, IDX_HEADS, TK), F32),
    ]
    out_specs = [qop(2 * A_HEADS, 256, TQ), qop(B_HEADS, 128, TQ), qop(A_HEADS, 128, idx_grp * TQ),
                 tok(A_HEADS * 256), tok(128), tok(128),
                 trs(A_HEADS * (2 * A_DH + V_PAD)), trs(B_DH + V_PAD), trs(IDX_HEADS)]
    in_specs = [
        pl.BlockSpec((1, ts, d), lambda b, t: (b, t, 0)),
        pl.BlockSpec((1, 6, d), lambda b, t: (b, 0, 0)),
        _const_spec(g.shape),
    ] + [_const_spec(w.shape) for w in ws]
    return pl.pallas_call(
        _inproj_kernel,
        grid=(bsz, s // ts),
        in_specs=in_specs,
        out_specs=out_specs,
        out_shape=out_shape,
        compiler_params=pltpu.CompilerParams(
            dimension_semantics=("arbitrary", "arbitrary"), vmem_limit_bytes=VMEM_LIMIT_BYTES),
        name="in_proj",
    )(x, mod, g, *ws)


def _softmax_step(s, m_ref, idx):
    m_prev = m_ref[idx]
    m_new = jnp.maximum(m_prev, jnp.max(s, axis=0, keepdims=True))
    m_ref[idx] = m_new
    return jnp.exp2(s - m_new).astype(BF16), jnp.exp2(m_prev - m_new)


def _accumulate(v, p, alpha, acc_ref, idx):
    acc_ref[idx] = alpha * acc_ref[idx] + jnp.dot(v, p, preferred_element_type=F32)


def _key_to_f32(key):
    return pltpu.bitcast(key ^ ((key >> 31) & jnp.int32(0x7FFFFFFF)), F32)


def _f32_to_key(x):
    bits = pltpu.bitcast(x, I32)
    return bits ^ ((bits >> 31) & jnp.int32(0x7FFFFFFF))


def _as_exact_f32(x):
    return _key_to_f32(_f32_to_key(x))


def _attn_kernel(lam_init, qa_ref, qb_ref, qi_ref, iw_ref, akp_ref, av_ref, bkp_ref, ikk_ref,
                 bv_ref, lamv_ref, g_ref, out_ref,
                 score_ref, rng_ref, sa_ref, sb_ref, w_ref,
                 acc_a_ref, m_a_ref, acc_b_ref, m_b_ref):
    i = pl.program_id(1)
    n_off = i * (TQ // TK)
    nkv = n_off + TQ // TK
    slopes_a = _alibi_slopes(A_HEADS)
    slopes_b = _alibi_slopes(B_HEADS)

    idx_grp = IDX_HEADS // A_HEADS
    for c in range(TQ // TK):
        w_ref[:, c * TK:(c + 1) * TK] = iw_ref[0, c] * ((IDX_HEADS * IDX_DH) ** -0.5)

    acc_a_ref[...] = jnp.zeros_like(acc_a_ref)
    acc_b_ref[...] = jnp.zeros_like(acc_b_ref)
    m_a_ref[...] = jnp.full_like(m_a_ref, NEG)
    m_b_ref[...] = jnp.full_like(m_b_ref, NEG)

    def diag_geometry(j):
        row = j * TK + lax.broadcasted_iota(I32, (TK, TQ), 0)
        col = i * TQ + lax.broadcasted_iota(I32, (TK, TQ), 1)
        allowed = row < ((col >> CHUNK_SHIFT) + 1) * CHUNK
        corr = (2.0 * LOG2E) * jnp.minimum((col - row).astype(F32), 0.0)
        return allowed, jnp.where(allowed, BIG, NEG), corr

    def index_block(j, diag):
        k0 = pl.multiple_of(j * TK, TK)
        kk = ikk_ref[0, pl.ds(k0, TK), :]
        score = None
        for g in range(IDX_HEADS // idx_grp):
            xg = jnp.dot(kk, qi_ref[0, 0, g], preferred_element_type=F32)
            for k in range(idx_grp):
                hh = g * idx_grp + k
                term = w_ref[hh:hh + 1, :] * jnp.maximum(xg[:, k * TQ:(k + 1) * TQ], 0.0)
                score = term if score is None else score + term
        lo_src = score
        if diag:
            allowed, _, _ = diag_geometry(j)
            lo_src = jnp.where(allowed, score, jnp.inf)
            score = jnp.where(allowed, score, -jnp.inf)
        score_ref[j] = score
        rng_ref[0] = jnp.minimum(rng_ref[0], jnp.min(lo_src.reshape(TK // 8, 8, TQ), axis=0))
        rng_ref[1] = jnp.maximum(rng_ref[1], jnp.max(score.reshape(TK // 8, 8, TQ), axis=0))

    def index_body(diag, j, carry):
        index_block(j, diag)
        return carry

    def index_group_body(jj, carry):
        for u in range(IDX_UNROLL):
            index_block(IDX_UNROLL * jj + u, False)
        return carry

    rng_ref[0] = jnp.full((8, TQ), jnp.inf, F32)
    rng_ref[1] = jnp.full((8, TQ), -jnp.inf, F32)
    lax.fori_loop(0, n_off // IDX_UNROLL, index_group_body, 0)
    lax.fori_loop(n_off // IDX_UNROLL * IDX_UNROLL, n_off, functools.partial(index_body, False), 0)
    lax.fori_loop(n_off, nkv, functools.partial(index_body, True), 0)

    qpos = i * TQ + lax.broadcasted_iota(I32, (1, TQ), 1)
    n_adm = ((qpos >> CHUNK_SHIFT) + 1) * CHUNK
    searching = n_adm > TOPK_MAX

    def count_where(pred_fn):
        def body(c, cnt):
            ind = jnp.where(pred_fn(score_ref[c], c), jnp.int32(1), jnp.int32(0))
            return cnt + jnp.sum(ind.reshape(TK // 8, 8, TQ), axis=0)
        cnt8 = lax.fori_loop(0, nkv, body, jnp.zeros((8, TQ), I32))
        return jnp.sum(cnt8, axis=0, keepdims=True)

    n_acc = 4

    def count_ge(cand):
        def body(c, accs):
            accs = list(accs)
            for r in range(TK // 8):
                a = accs[r % n_acc]
                accs[r % n_acc] = jnp.where(score_ref[c, r * 8:(r + 1) * 8, :] >= cand, a + 1, a)
            return tuple(accs)
        accs = lax.fori_loop(0, nkv, body,
                             tuple(jnp.zeros((8, TQ), I32) for _ in range(n_acc)))
        return jnp.sum(functools.reduce(lambda a, b: a + b, accs), axis=0, keepdims=True)

    def value_pass(_, carry):
        lo, hi, cnt_lo = carry
        mid = _as_exact_f32(0.5 * lo + 0.5 * hi)
        cnt = count_ge(mid)
        inside = jnp.logical_and(mid > lo, mid < hi)
        up = jnp.logical_and(inside, cnt >= TOPK_MAX)
        dn = jnp.logical_and(inside, cnt < TOPK_MAX)
        return jnp.where(up, mid, lo), jnp.where(dn, mid, hi), jnp.where(up, cnt, cnt_lo)

    lo0 = _as_exact_f32(jnp.min(rng_ref[0], axis=0, keepdims=True))
    hi0 = _key_to_f32(_f32_to_key(jnp.max(rng_ref[1], axis=0, keepdims=True)) + 1)
    lo, hi, cnt_at = lax.fori_loop(0, VALUE_PASSES, value_pass, (lo0, hi0, n_adm))

    def status(lo, hi, cnt_lo):
        gap = _f32_to_key(hi) - _f32_to_key(lo)
        open_ = jnp.logical_and(searching, jnp.logical_and(cnt_lo != TOPK_MAX, gap != 1))
        open_ = jnp.where(open_, jnp.int32(1), jnp.int32(0))
        tied = jnp.where(jnp.logical_and(searching, cnt_lo > TOPK_MAX), jnp.int32(1), jnp.int32(0))
        return open_, jnp.max(open_), jnp.max(tied)

    def key_cond(carry):
        return carry[4] > 0

    def key_pass(carry):
        lo, hi, cnt_lo, open_ = carry[:4]
        klo = _f32_to_key(lo)
        kmid = klo + lax.shift_right_logical(_f32_to_key(hi) - klo, jnp.int32(1))
        mid = _key_to_f32(kmid)
        cnt = count_ge(mid)
        up = jnp.logical_and(open_ > 0, cnt >= TOPK_MAX)
        dn = jnp.logical_and(open_ > 0, cnt < TOPK_MAX)
        lo, hi, cnt_lo = jnp.where(up, mid, lo), jnp.where(dn, mid, hi), jnp.where(up, cnt, cnt_lo)
        return (lo, hi, cnt_lo) + status(lo, hi, cnt_lo)

    n_stage = A_HEADS
    dsa_grp = B_HEADS // n_stage
    a_dv = 2 * A_DH
    a_rows_v = a_dv + V_PAD
    n_chain = 2 + dsa_grp

    def qk_chain(h, c, kstart):
        if c < 2:
            kh = akp_ref[0, pl.ds(kstart, TK), h * 256:(h + 1) * 256]
            qh = qa_ref[0, 0, 2 * h + c]
        else:
            kh = bkp_ref[0, pl.ds(kstart, TK), :]
            qh = qb_ref[0, 0, dsa_grp * h + c - 2]
        return jnp.dot(kh, qh, preferred_element_type=F32)

    def carry_ref(c):
        return (sa_ref, c * TQ) if c < 2 else (sb_ref, (c - 2) * TQ)

    for c in range(n_chain):
        ref, off = carry_ref(c)
        ref[0, :, off:off + TQ] = qk_chain(0, c, 0)

    lo, _, _, _, _, any_tied = lax.while_loop(
        key_cond, key_pass, (lo, hi, cnt_at) + status(lo, hi, cnt_at))
    tau = jnp.where(searching, lo, F32_LOWEST)

    @pl.when(any_tied > 0)
    def _():
        def pos_of(c):
            return lax.broadcasted_iota(I32, (TK, TQ), 0) + c * TK

        need = TOPK_MAX - count_where(lambda st, c: st > tau)
        n_bits = int(score_ref.shape[0] * TK).bit_length()

        def idx_body(t, bound):
            trial = bound | jnp.left_shift(jnp.int32(1), n_bits - 1 - t)
            cnt = count_where(
                lambda st, c: jnp.where(st == tau, pos_of(c), jnp.int32(1 << 30)) < trial)
            return jnp.where(cnt <= need, trial, bound)

        bound = lax.fori_loop(0, n_bits, idx_body, jnp.zeros((1, TQ), I32))
        below = _key_to_f32(_f32_to_key(tau) - 1)

        def demote(c, carry):
            st = score_ref[c]
            drop = jnp.where(st == tau, pos_of(c), jnp.int32(-1)) >= bound
            score_ref[c] = jnp.where(drop, below, st)
            return carry

        lax.fori_loop(0, nkv, demote, 0)

    def attend_block(j, diag):
        k0 = pl.multiple_of(j * TK, TK)
        kn0 = pl.multiple_of(jnp.minimum(j + 1, nkv - 1) * TK, TK)
        slot = lax.rem(j, 2)
        cap_sel = jnp.where(score_ref[j] >= tau, BIG, NEG)
        if diag:
            _, cap_chunk, corr = diag_geometry(j)
        vb = bv_ref[0, j]
        s_next = []
        for c in range(n_chain):
            ref, off = carry_ref(c)
            s_next.append(ref[slot, :, off:off + TQ])
        for h in range(n_stage):
            vh = av_ref[0, j, h * a_rows_v:(h + 1) * a_rows_v, :]
            if diag:
                corr_a = slopes_a[h] * corr
            for c in range(n_chain):
                s = s_next[c]
                if h + 1 < n_stage:
                    s_next[c] = qk_chain(h + 1, c, k0)
                else:
                    ref, off = carry_ref(c)
                    ref[1 - slot, :, off:off + TQ] = qk_chain(0, c, kn0)
                if c < 2:
                    if diag:
                        s = jnp.minimum(s, cap_chunk) + corr_a
                    p, alpha = _softmax_step(s, m_a_ref, 2 * h + c)
                    _accumulate(vh, p, alpha, acc_a_ref, 2 * h + c)
                else:
                    hb = dsa_grp * h + c - 2
                    s = jnp.minimum(s, cap_sel)
                    if diag:
                        s = s + slopes_b[hb] * corr
                    p, alpha = _softmax_step(s, m_b_ref, hb)
                    _accumulate(vb, p, alpha, acc_b_ref, hb)

    def attend_body(diag, j, carry):
        attend_block(j, diag)
        return carry

    lax.fori_loop(0, n_off, functools.partial(attend_body, False), 0)
    lax.fori_loop(n_off, nkv, functools.partial(attend_body, True), 0)

    lv = lamv_ref[...]
    lam = (jnp.exp(jnp.sum(lv[0:1] * lv[1:2], axis=1, keepdims=True))
           - jnp.exp(jnp.sum(lv[2:3] * lv[3:4], axis=1, keepdims=True)) + lam_init)
    g = g_ref[...]
    for h in range(A_HEADS):
        o1 = acc_a_ref[2 * h, 0:a_dv, :] / acc_a_ref[2 * h, a_dv:a_dv + 1, :]
        o2 = acc_a_ref[2 * h + 1, 0:a_dv, :] / acc_a_ref[2 * h + 1, a_dv:a_dv + 1, :]
        o = o1 - lam * o2
        y = o * lax.rsqrt(jnp.mean(o * o, axis=0, keepdims=True) + EPS)
        out_ref[0, 0, h * 128:(h + 1) * 128, :] = ((y * g) * (1.0 - lam_init)).astype(BF16)
    a_rows = A_HEADS * 2 * A_DH
    for h in range(B_HEADS):
        ob = acc_b_ref[h, 0:B_DH, :] / acc_b_ref[h, B_DH:B_DH + 1, :]
        out_ref[0, 0, a_rows + h * B_DH:a_rows + (h + 1) * B_DH, :] = ob.astype(BF16)


def _attention(lam_init, qat, qbt, qit, iwt, akp, avt, bkp, ikk, bvt, lamv, gt):
    bsz, s, _ = akp.shape
    nq = s // TQ
    nk = s // TK
    d_mix = A_HEADS * 2 * A_DH + B_HEADS * B_DH

    def per_query_block(arr):
        shape = (1, 1) + arr.shape[2:]
        return pl.BlockSpec(shape, lambda b, i: (b, i) + (0,) * (len(shape) - 2))

    def per_batch(shape):
        return pl.BlockSpec(shape, lambda b, i: (b,) + (0,) * (len(shape) - 1))

    in_specs = [
        per_query_block(qat),
        per_query_block(qbt),
        per_query_block(qit),
        pl.BlockSpec((1, TQ // TK, IDX_HEADS, TK), lambda b, i: (b, i, 0, 0)),
        per_batch((1, s, akp.shape[2])),
        per_batch((1, nk, avt.shape[2], TK)),
        per_batch((1, s, 128)),
        per_batch((1, s, 128)),
        per_batch((1, nk, bvt.shape[2], TK)),
        _const_spec(lamv.shape),
        _const_spec(gt.shape),
    ]
    n_maps = 2 * A_HEADS
    scratch = [
        pltpu.VMEM((nk, TK, TQ), F32),
        pltpu.VMEM((2, 8, TQ), F32),
        pltpu.VMEM((2, TK, 2 * TQ), F32),
        pltpu.VMEM((2, TK, B_HEADS // A_HEADS * TQ), F32),
        pltpu.VMEM((IDX_HEADS, TQ), F32),
        pltpu.VMEM((n_maps, 2 * A_DH + V_PAD, TQ), F32),
        pltpu.VMEM((n_maps, 1, TQ), F32),
        pltpu.VMEM((B_HEADS, B_DH + V_PAD, TQ), F32),
        pltpu.VMEM((B_HEADS, 1, TQ), F32),
    ]
    return pl.pallas_call(
        functools.partial(_attn_kernel, lam_init),
        grid=(bsz, nq),
        in_specs=in_specs,
        out_specs=pl.BlockSpec((1, 1, d_mix, TQ), lambda b, i: (b, i, 0, 0)),
        out_shape=jax.ShapeDtypeStruct((bsz, nq, d_mix, TQ), BF16),
        scratch_shapes=scratch,
        compiler_params=pltpu.CompilerParams(
            dimension_semantics=("arbitrary", "arbitrary"), vmem_limit_bytes=VMEM_LIMIT_BYTES),
        name="attention",
    )(qat, qbt, qit, iwt, akp, avt, bkp, ikk, bvt, lamv, gt)


def _ffn_kernel(final, mix_ref, x_ref, mod_ref, n2_ref, fg_ref, wo_ref, w1_ref, w2_ref, o_ref):
    wo = wo_ref[...]
    y = jnp.concatenate(
        [lax.dot_general(mix_ref[0, c], wo, _TN, preferred_element_type=F32)
         for c in range(TM_FFN // TQ)], axis=0)
    g1 = mod_ref[0, 2:3, :]
    sh2 = mod_ref[0, 3:4, :]
    sc2 = mod_ref[0, 4:5, :]
    g2 = mod_ref[0, 5:6, :]
    x1 = x_ref[0] + g1 * y
    h2 = ((_rms(x1) * n2_ref[...]) * (1.0 + sc2) + sh2).astype(BF16)
    ff = jnp.zeros_like(x1)
    for c in range(w1_ref.shape[1] // FF_CHUNK):
        u = jnp.dot(h2, w1_ref[:, c * FF_CHUNK:(c + 1) * FF_CHUNK], preferred_element_type=F32)
        u = jnp.square(jnp.maximum(u, 0.0)).astype(BF16)
        ff = ff + jnp.dot(u, w2_ref[c * FF_CHUNK:(c + 1) * FF_CHUNK, :], preferred_element_type=F32)
    x2 = x1 + g2 * ff
    if final:
        x2 = _rms(x2) * fg_ref[...]
    o_ref[0] = x2


def _out_ffn(final, mixt, x, mod, n2g, fg, wo, w1, w2):
    bsz, s, d = x.shape
    d_mix = mixt.shape[2]
    tm = TM_FFN
    cpt = tm // TQ
    return pl.pallas_call(
        functools.partial(_ffn_kernel, final),
        grid=(bsz, s // tm),
        in_specs=[
            pl.BlockSpec((1, cpt, d_mix, TQ), lambda b, t: (b, t, 0, 0)),
            pl.BlockSpec((1, tm, d), lambda b, t: (b, t, 0)),
            pl.BlockSpec((1, 6, d), lambda b, t: (b, 0, 0)),
            _const_spec(n2g.shape),
            _const_spec(fg.shape),
            _const_spec(wo.shape),
            _const_spec(w1.shape),
            _const_spec(w2.shape),
        ],
        out_specs=pl.BlockSpec((1, tm, d), lambda b, t: (b, t, 0)),
        out_shape=jax.ShapeDtypeStruct((bsz, s, d), F32),
        compiler_params=pltpu.CompilerParams(
            dimension_semantics=("arbitrary", "arbitrary"), vmem_limit_bytes=VMEM_LIMIT_BYTES),
        name="out_ffn",
    )(mixt, x, mod, n2g, fg, wo, w1, w2)


def _split_w_in(w):
    d = w.shape[0]
    a = A_HEADS * 2 * A_DH
    sizes = (a, a, a, B_HEADS * B_DH, B_DH, B_DH, IDX_HEADS * IDX_DH, IDX_DH, IDX_HEADS)
    offs = np.cumsum((0,) + sizes)
    aq, ak, av, bq, bk, bv, iq, ik, iw = [w[:, offs[n]:offs[n + 1]] for n in range(len(sizes))]
    kk = jnp.concatenate([bk, jnp.zeros((d, 128 - B_DH), w.dtype), ik, ik], axis=1)
    wt = jnp.concatenate([aq, bq, iq, av, bv, iw], axis=1).T
    return [m.astype(BF16) for m in (ak, kk, wt)]


def kernel(x, c, norm1_g, norm2_g, w_ada, b_ada, w_in, lam_q1, lam_k1, lam_q2, lam_k2,
           subln_g, w_out, w_ff1, w_ff2, final_g):
    bsz, s, d = x.shape
    depth = w_in.shape[0]
    assert s % TS_PROJ == 0 and s % TM_FFN == 0 and s % TQ == 0 and TQ % TK == 0 and TM_FFN % TQ == 0
    assert min(TOPK_MAX, s // 4) == TOPK_MAX and s // POS_SPLIT <= 256 and CHUNK == 1 << CHUNK_SHIFT
    fg = final_g.reshape(1, d)
    for l in range(depth):
        lam_init = 0.8 - 0.6 * math.exp(-0.3 * l)
        mod = _adaln_mod(c, w_ada[l].astype(BF16), b_ada[l].reshape(1, -1)).reshape(bsz, 6, d)
        qat, qbt, qit, akp, bkp, ikk, avt, bvt, iwt = _in_proj(
            x, mod, norm1_g[l].reshape(1, d), _split_w_in(w_in[l]))
        lamv = jnp.stack([lam_q1[l], lam_k1[l], lam_q2[l], lam_k2[l]])
        gt = jnp.broadcast_to(subln_g[l][:, None], (2 * A_DH, TQ))
        mixt = _attention(lam_init, qat, qbt, qit, iwt, akp, avt, bkp, ikk, bvt, lamv, gt)
        x = _out_ffn(l == depth - 1, mixt, x, mod, norm2_g[l].reshape(1, d), fg,
                     w_out[l].astype(BF16), w_ff1[l].astype(BF16), w_ff2[l].astype(BF16))
    return x
```

```python
import functools
import math

import jax
import jax.numpy as jnp
import numpy as np
from jax import lax
from jax.experimental import pallas as pl
from jax.experimental.pallas import tpu as pltpu

CHUNK = 64
CHUNK_SHIFT = 6
A_HEADS = 4
A_DH = 64
B_HEADS = 8
B_DH = 64
IDX_HEADS = 16
IDX_DH = 64
TOPK_MAX = 256
EPS = 1e-6

TQ = 256
TK = 256
TS_PROJ = 512
TM_FFN = 512
FF_CHUNK = 1024
VMEM_LIMIT_BYTES = 56 * 1024 * 1024
IDX_UNROLL = 2
VALUE_PASSES = 18
V_PAD = 16
POS_SHIFT = 6
POS_SPLIT = 1 << POS_SHIFT

LOG2E = math.log2(math.e)
NEG = -1e30
BIG = 3e38
F32_LOWEST = float(np.finfo(np.float32).min)
BF16 = jnp.bfloat16
F32 = jnp.float32
I32 = jnp.int32

_NT = (((1,), (1,)), ((), ()))
_TN = (((0,), (0,)), ((), ()))


def _alibi_slopes(n):
    return [2.0 ** (-8.0 * (i + 1) / n) for i in range(n)]


def _slope_columns(slope):
    pieces, rest = [], slope * LOG2E
    for _ in range(3):
        p = float(np.asarray(rest, np.float32).astype(BF16).astype(np.float32))
        pieces.append(p)
        rest -= p
    return [p * POS_SPLIT for p in pieces] + pieces


def _const_spec(shape):
    nd = len(shape)
    return pl.BlockSpec(shape, lambda *_: (0,) * nd, pipeline_mode=pl.Buffered(1))


def _rms(x):
    return x * lax.rsqrt(jnp.mean(x * x, axis=-1, keepdims=True) + EPS)


def _mod_kernel(c_ref, w_ref, b_ref, o_ref):
    c = c_ref[...]
    cond = c * (1.0 / (1.0 + jnp.exp(-c)))
    o_ref[...] = jnp.dot(cond.astype(BF16), w_ref[...], preferred_element_type=F32) + b_ref[...]


def _adaln_mod(c, w_ada, b_ada):
    bsz, d = c.shape
    n = w_ada.shape[1]
    return pl.pallas_call(
        _mod_kernel,
        grid=(n // d,),
        in_specs=[
            pl.BlockSpec((bsz, d), lambda j: (0, 0)),
            pl.BlockSpec((d, d), lambda j: (0, j)),
            pl.BlockSpec((1, d), lambda j: (0, j)),
        ],
        out_specs=pl.BlockSpec((bsz, d), lambda j: (0, j)),
        out_shape=jax.ShapeDtypeStruct((bsz, n), F32),
        compiler_params=pltpu.CompilerParams(dimension_semantics=("arbitrary",)),
        name="adaln_mod",
    )(c, w_ada, b_ada)


def _row_constants(rows, lanes, values):
    r = lax.broadcasted_iota(I32, (rows, lanes), 0)
    out = jnp.zeros((rows, lanes), F32)
    for k, v in enumerate(values):
        out = jnp.where(r == k, v, out)
    return out


def _inproj_kernel(x_ref, mod_ref, g_ref, wak_ref, wkk_ref, wt_ref,
                   qa_ref, qb_ref, qi_ref, akp_ref, bkp_ref, ikk_ref, av_ref, bv_ref, iw_ref):
    x = x_ref[0]
    sh1 = mod_ref[0, 0:1, :]
    sc1 = mod_ref[0, 1:2, :]
    h = (_rms(x) * g_ref[...]) * (1.0 + sc1) + sh1
    hb = h.astype(BF16)

    pos = pl.program_id(1) * TS_PROJ + lax.broadcasted_iota(I32, (TS_PROJ, 128), 0)
    lane = lax.broadcasted_iota(I32, (TS_PROJ, 128), 1)
    pos_hi = (pos >> POS_SHIFT).astype(F32)
    pos_lo = (pos & (POS_SPLIT - 1)).astype(F32)

    def pos_tile(l0):
        return jnp.where((lane >= l0) & (lane < l0 + 3), pos_hi,
                         jnp.where((lane >= l0 + 3) & (lane < l0 + 6), pos_lo, 0.0))

    ak = jnp.dot(hb, wak_ref[...], preferred_element_type=F32)
    pos0 = pos_tile(0).astype(BF16)
    for hd in range(A_HEADS):
        akp_ref[0, :, hd * 256:hd * 256 + 128] = ak[:, hd * 128:(hd + 1) * 128].astype(BF16)
        akp_ref[0, :, hd * 256 + 128:(hd + 1) * 256] = pos0
    kk = jnp.dot(hb, wkk_ref[...], preferred_element_type=F32)
    bkp_ref[0] = (kk[:, 0:128] + pos_tile(B_DH)).astype(BF16)
    ikk_ref[0] = kk[:, 128:256].astype(BF16)

    t_all = lax.dot_general(wt_ref[...], hb, _NT, preferred_element_type=F32)
    a_cols = A_HEADS * 2 * A_DH
    b_cols = B_HEADS * B_DH
    i_cols = IDX_HEADS * IDX_DH
    r_bq = a_cols
    r_iq = r_bq + b_cols
    r_av = r_iq + i_cols
    r_bv = r_av + a_cols
    r_iw = r_bv + B_DH
    aqt = (t_all[0:r_bq] * (A_DH ** -0.5 * LOG2E)).astype(BF16)
    bqt = (t_all[r_bq:r_iq] * (B_DH ** -0.5 * LOG2E)).astype(BF16)
    iqt = t_all[r_iq:r_av].astype(BF16)
    avt = t_all[r_av:r_bv].astype(BF16)
    bvt = t_all[r_bv:r_iw].astype(BF16)
    iwt = t_all[r_iw:r_iw + IDX_HEADS]

    zeros = jnp.zeros((A_DH, TQ), BF16)
    idx_grp = IDX_HEADS // A_HEADS
    slopes_a = _alibi_slopes(A_HEADS)
    slopes_b = _alibi_slopes(B_HEADS)
    for c in range(TS_PROJ // TQ):
        cols = slice(c * TQ, (c + 1) * TQ)
        for hd in range(A_HEADS):
            slope_rows = _row_constants(128, TQ, _slope_columns(slopes_a[hd])).astype(BF16)
            for mp in range(2):
                r0 = hd * 2 * A_DH + mp * A_DH
                qa_ref[0, c, 2 * hd + mp, mp * A_DH:(mp + 1) * A_DH, :] = aqt[r0:r0 + A_DH, cols]
                qa_ref[0, c, 2 * hd + mp, (1 - mp) * A_DH:(2 - mp) * A_DH, :] = zeros
                qa_ref[0, c, 2 * hd + mp, 2 * A_DH:, :] = slope_rows
        for hd in range(B_HEADS):
            qb_ref[0, c, hd, 0:B_DH, :] = bqt[hd * B_DH:(hd + 1) * B_DH, cols]
            qb_ref[0, c, hd, B_DH:, :] = _row_constants(
                128 - B_DH, TQ, _slope_columns(slopes_b[hd])).astype(BF16)
        for hd in range(IDX_HEADS):
            g, k = divmod(hd, idx_grp)
            qi_ref[0, c, g, 0:IDX_DH, k * TQ:(k + 1) * TQ] = iqt[hd * IDX_DH:(hd + 1) * IDX_DH, cols]
            qi_ref[0, c, g, IDX_DH:, k * TQ:(k + 1) * TQ] = zeros

    ones_rows = jnp.where(lax.broadcasted_iota(I32, (V_PAD, TK), 0) == 0, 1.0, 0.0).astype(BF16)
    a_dv = 2 * A_DH
    for c in range(TS_PROJ // TK):
        for hd in range(A_HEADS):
            r0 = hd * (a_dv + V_PAD)
            av_ref[0, c, r0:r0 + a_dv, :] = avt[hd * a_dv:(hd + 1) * a_dv, c * TK:(c + 1) * TK]
            av_ref[0, c, r0 + a_dv:r0 + a_dv + V_PAD, :] = ones_rows
        bv_ref[0, c, 0:B_DH, :] = bvt[:, c * TK:(c + 1) * TK]
        bv_ref[0, c, B_DH:B_DH + V_PAD, :] = ones_rows
        iw_ref[0, c] = iwt[:, c * TK:(c + 1) * TK]


def _in_proj(x, mod, g, ws):
    bsz, s, d = x.shape
    ts = TS_PROJ
    nk = s // TK
    nq = s // TQ
    cpt = ts // TK
    cpq = ts // TQ

    def tok(cols):
        return pl.BlockSpec((1, ts, cols), lambda b, t: (b, t, 0))

    def trs(rows):
        return pl.BlockSpec((1, cpt, rows, TK), lambda b, t: (b, t, 0, 0))

    def qop(n, rows, lanes):
        return pl.BlockSpec((1, cpq, n, rows, lanes), lambda b, t: (b, t, 0, 0, 0))

    idx_grp = IDX_HEADS // A_HEADS
    out_shape = [
        jax.ShapeDtypeStruct((bsz, nq, 2 * A_HEADS, 256, TQ), BF16),
        jax.ShapeDtypeStruct((bsz, nq, B_HEADS, 128, TQ), BF16),
        jax.ShapeDtypeStruct((bsz, nq, A_HEADS, 128, idx_grp * TQ), BF16),
        jax.ShapeDtypeStruct((bsz, s, A_HEADS * 256), BF16),
        jax.ShapeDtypeStruct((bsz, s, 128), BF16),
        jax.ShapeDtypeStruct((bsz, s, 128), BF16),
        jax.ShapeDtypeStruct((bsz, nk, A_HEADS * (2 * A_DH + V_PAD), TK), BF16),
        jax.ShapeDtypeStruct((bsz, nk, B_DH + V_PAD, TK), BF16),
        jax.ShapeDtypeStruct((bsz, nk, IDX_HEADS, TK), F32),
    ]
    out_specs = [qop(2 * A_HEADS, 256, TQ), qop(B_HEADS, 128, TQ), qop(A_HEADS, 128, idx_grp * TQ),
                 tok(A_HEADS * 256), tok(128), tok(128),
                 trs(A_HEADS * (2 * A_DH + V_PAD)), trs(B_DH + V_PAD), trs(IDX_HEADS)]
    in_specs = [
        pl.BlockSpec((1, ts, d), lambda b, t: (b, t, 0)),
        pl.BlockSpec((1, 6, d), lambda b, t: (b, 0, 0)),
        _const_spec(g.shape),
    ] + [_const_spec(w.shape) for w in ws]
    return pl.pallas_call(
        _inproj_kernel,
        grid=(bsz, s // ts),
        in_specs=in_specs,
        out_specs=out_specs,
        out_shape=out_shape,
        compiler_params=pltpu.CompilerParams(
            dimension_semantics=("arbitrary", "arbitrary"), vmem_limit_bytes=VMEM_LIMIT_BYTES),
        name="in_proj",
    )(x, mod, g, *ws)


def _softmax_step(s, m_ref, idx):
    m_prev = m_ref[idx]
    m_new = jnp.maximum(m_prev, jnp.max(s, axis=0, keepdims=True))
    m_ref[idx] = m_new
    return jnp.exp2(s - m_new).astype(BF16), jnp.exp2(m_prev - m_new)


def _accumulate(v, p, alpha, acc_ref, idx):
    acc_ref[idx] = alpha * acc_ref[idx] + jnp.dot(v, p, preferred_element_type=F32)


def _key_to_f32(key):
    return pltpu.bitcast(key ^ ((key >> 31) & jnp.int32(0x7FFFFFFF)), F32)


def _f32_to_key(x):
    bits = pltpu.bitcast(x, I32)
    return bits ^ ((bits >> 31) & jnp.int32(0x7FFFFFFF))


def _as_exact_f32(x):
    return _key_to_f32(_f32_to_key(x))


def _attn_kernel(lam_init, qa_ref, qb_ref, qi_ref, iw_ref, akp_ref, av_ref, bkp_ref, ikk_ref,
                 bv_ref, lamv_ref, g_ref, out_ref,
                 score_ref, rng_ref, sa_ref, sb_ref, w_ref,
                 acc_a_ref, m_a_ref, acc_b_ref, m_b_ref):
    i = pl.program_id(1)
    n_off = i * (TQ // TK)
    nkv = n_off + TQ // TK
    slopes_a = _alibi_slopes(A_HEADS)
    slopes_b = _alibi_slopes(B_HEADS)

    idx_grp = IDX_HEADS // A_HEADS
    for c in range(TQ // TK):
        w_ref[:, c * TK:(c + 1) * TK] = iw_ref[0, c] * ((IDX_HEADS * IDX_DH) ** -0.5)

    def diag_geometry(j):
        row = j * TK + lax.broadcasted_iota(I32, (TK, TQ), 0)
        col = i * TQ + lax.broadcasted_iota(I32, (TK, TQ), 1)
        allowed = row < ((col >> CHUNK_SHIFT) + 1) * CHUNK
        corr = (2.0 * LOG2E) * jnp.minimum((col - row).astype(F32), 0.0)
        return allowed, jnp.where(allowed, BIG, NEG), corr

    def index_block(j, diag):
        k0 = pl.multiple_of(j * TK, TK)
        kk = ikk_ref[0, pl.ds(k0, TK), :]
        score = None
        for g in range(IDX_HEADS // idx_grp):
            xg = jnp.dot(kk, qi_ref[0, 0, g], preferred_element_type=F32)
            for k in range(idx_grp):
                hh = g * idx_grp + k
                term = w_ref[hh:hh + 1, :] * jnp.maximum(xg[:, k * TQ:(k + 1) * TQ], 0.0)
                score = term if score is None else score + term
        lo_src = score
        if diag:
            allowed, _, _ = diag_geometry(j)
            lo_src = jnp.where(allowed, score, jnp.inf)
            score = jnp.where(allowed, score, -jnp.inf)
        score_ref[j] = score
        rng_ref[0] = jnp.minimum(rng_ref[0], jnp.min(lo_src.reshape(TK // 8, 8, TQ), axis=0))
        rng_ref[1] = jnp.maximum(rng_ref[1], jnp.max(score.reshape(TK // 8, 8, TQ), axis=0))

    def index_body(diag, j, carry):
        index_block(j, diag)
        return carry

    def index_group_body(jj, carry):
        for u in range(IDX_UNROLL):
            index_block(IDX_UNROLL * jj + u, False)
        return carry

    rng_ref[0] = jnp.full((8, TQ), jnp.inf, F32)
    rng_ref[1] = jnp.full((8, TQ), -jnp.inf, F32)
    lax.fori_loop(0, n_off // IDX_UNROLL, index_group_body, 0)
    lax.fori_loop(n_off // IDX_UNROLL * IDX_UNROLL, n_off, functools.partial(index_body, False), 0)
    lax.fori_loop(n_off, nkv, functools.partial(index_body, True), 0)

    qpos = i * TQ + lax.broadcasted_iota(I32, (1, TQ), 1)
    n_adm = ((qpos >> CHUNK_SHIFT) + 1) * CHUNK
    searching = n_adm > TOPK_MAX

    def count_where(pred_fn):
        def body(c, cnt):
            ind = jnp.where(pred_fn(score_ref[c], c), jnp.int32(1), jnp.int32(0))
            return cnt + jnp.sum(ind.reshape(TK // 8, 8, TQ), axis=0)
        cnt8 = lax.fori_loop(0, nkv, body, jnp.zeros((8, TQ), I32))
        return jnp.sum(cnt8, axis=0, keepdims=True)

    n_acc = 4

    def count_ge(cand):
        def body(c, accs):
            accs = list(accs)
            for r in range(TK // 8):
                a = accs[r % n_acc]
                accs[r % n_acc] = jnp.where(score_ref[c, r * 8:(r + 1) * 8, :] >= cand, a + 1, a)
            return tuple(accs)
        accs = lax.fori_loop(0, nkv, body,
                             tuple(jnp.zeros((8, TQ), I32) for _ in range(n_acc)))
        return jnp.sum(functools.reduce(lambda a, b: a + b, accs), axis=0, keepdims=True)

    def value_pass(_, carry):
        lo, hi, cnt_lo = carry
        mid = _as_exact_f32(0.5 * lo + 0.5 * hi)
        cnt = count_ge(mid)
        inside = jnp.logical_and(mid > lo, mid < hi)
        up = jnp.logical_and(inside, cnt >= TOPK_MAX)
        dn = jnp.logical_and(inside, cnt < TOPK_MAX)
        return jnp.where(up, mid, lo), jnp.where(dn, mid, hi), jnp.where(up, cnt, cnt_lo)

    lo0 = _as_exact_f32(jnp.min(rng_ref[0], axis=0, keepdims=True))
    hi0 = _key_to_f32(_f32_to_key(jnp.max(rng_ref[1], axis=0, keepdims=True)) + 1)
    lo, hi, cnt_at = lax.fori_loop(0, VALUE_PASSES, value_pass, (lo0, hi0, n_adm))

    def status(lo, hi, cnt_lo):
        gap = _f32_to_key(hi) - _f32_to_key(lo)
        open_ = jnp.logical_and(searching, jnp.logical_and(cnt_lo != TOPK_MAX, gap != 1))
        open_ = jnp.where(open_, jnp.int32(1), jnp.int32(0))
        tied = jnp.where(jnp.logical_and(searching, cnt_lo > TOPK_MAX), jnp.int32(1), jnp.int32(0))
        return open_, jnp.max(open_), jnp.max(tied)

    def key_cond(carry):
        return carry[4] > 0

    def key_pass(carry):
        lo, hi, cnt_lo, open_ = carry[:4]
        klo = _f32_to_key(lo)
        kmid = klo + lax.shift_right_logical(_f32_to_key(hi) - klo, jnp.int32(1))
        mid = _key_to_f32(kmid)
        cnt = count_ge(mid)
        up = jnp.logical_and(open_ > 0, cnt >= TOPK_MAX)
        dn = jnp.logical_and(open_ > 0, cnt < TOPK_MAX)
        lo, hi, cnt_lo = jnp.where(up, mid, lo), jnp.where(dn, mid, hi), jnp.where(up, cnt, cnt_lo)
        return (lo, hi, cnt_lo) + status(lo, hi, cnt_lo)

    n_stage = A_HEADS
    dsa_grp = B_HEADS // n_stage
    a_dv = 2 * A_DH
    a_rows_v = a_dv + V_PAD
    n_chain = 2 + dsa_grp

    def qk_chain(h, c, kstart):
        if c < 2:
            kh = akp_ref[0, pl.ds(kstart, TK), h * 256:(h + 1) * 256]
            qh = qa_ref[0, 0, 2 * h + c]
        else:
            kh = bkp_ref[0, pl.ds(kstart, TK), :]
            qh = qb_ref[0, 0, dsa_grp * h + c - 2]
        return jnp.dot(kh, qh, preferred_element_type=F32)

    def carry_ref(c):
        return (sa_ref, c * TQ) if c < 2 else (sb_ref, (c - 2) * TQ)

    for c in range(n_chain):
        ref, off = carry_ref(c)
        ref[0, :, off:off + TQ] = qk_chain(0, c, 0)
    acc_a_ref[...] = jnp.zeros_like(acc_a_ref)
    acc_b_ref[...] = jnp.zeros_like(acc_b_ref)
    m_a_ref[...] = jnp.full_like(m_a_ref, NEG)
    m_b_ref[...] = jnp.full_like(m_b_ref, NEG)

    lo, _, _, _, _, any_tied = lax.while_loop(
        key_cond, key_pass, (lo, hi, cnt_at) + status(lo, hi, cnt_at))
    tau = jnp.where(searching, lo, F32_LOWEST)

    @pl.when(any_tied > 0)
    def _():
        def pos_of(c):
            return lax.broadcasted_iota(I32, (TK, TQ), 0) + c * TK

        need = TOPK_MAX - count_where(lambda st, c: st > tau)
        n_bits = int(score_ref.shape[0] * TK).bit_length()

        def idx_body(t, bound):
            trial = bound | jnp.left_shift(jnp.int32(1), n_bits - 1 - t)
            cnt = count_where(
                lambda st, c: jnp.where(st == tau, pos_of(c), jnp.int32(1 << 30)) < trial)
            return jnp.where(cnt <= need, trial, bound)

        bound = lax.fori_loop(0, n_bits, idx_body, jnp.zeros((1, TQ), I32))
        below = _key_to_f32(_f32_to_key(tau) - 1)

        def demote(c, carry):
            st = score_ref[c]
            drop = jnp.where(st == tau, pos_of(c), jnp.int32(-1)) >= bound
            score_ref[c] = jnp.where(drop, below, st)
            return carry

        lax.fori_loop(0, nkv, demote, 0)

    def attend_block(j, diag):
        k0 = pl.multiple_of(j * TK, TK)
        kn0 = pl.multiple_of(jnp.minimum(j + 1, nkv - 1) * TK, TK)
        slot = lax.rem(j, 2)
        cap_sel = jnp.where(score_ref[j] >= tau, BIG, NEG)
        if diag:
            _, cap_chunk, corr = diag_geometry(j)
        vb = bv_ref[0, j]
        s_next = []
        for c in range(n_chain):
            ref, off = carry_ref(c)
            s_next.append(ref[slot, :, off:off + TQ])
        for h in range(n_stage):
            vh = av_ref[0, j, h * a_rows_v:(h + 1) * a_rows_v, :]
            if diag:
                corr_a = slopes_a[h] * corr
            for c in range(n_chain):
                s = s_next[c]
                if h + 1 < n_stage:
                    s_next[c] = qk_chain(h + 1, c, k0)
                else:
                    ref, off = carry_ref(c)
                    ref[1 - slot, :, off:off + TQ] = qk_chain(0, c, kn0)
                if c < 2:
                    if diag:
                        s = jnp.minimum(s, cap_chunk) + corr_a
                    p, alpha = _softmax_step(s, m_a_ref, 2 * h + c)
                    _accumulate(vh, p, alpha, acc_a_ref, 2 * h + c)
                else:
                    hb = dsa_grp * h + c - 2
                    s = jnp.minimum(s, cap_sel)
                    if diag:
                        s = s + slopes_b[hb] * corr
                    p, alpha = _softmax_step(s, m_b_ref, hb)
                    _accumulate(vb, p, alpha, acc_b_ref, hb)

    def attend_body(diag, j, carry):
        attend_block(j, diag)
        return carry

    lax.fori_loop(0, n_off, functools.partial(attend_body, False), 0)
    lax.fori_loop(n_off, nkv, functools.partial(attend_body, True), 0)

    lv = lamv_ref[...]
    lam = (jnp.exp(jnp.sum(lv[0:1] * lv[1:2], axis=1, keepdims=True))
           - jnp.exp(jnp.sum(lv[2:3] * lv[3:4], axis=1, keepdims=True)) + lam_init)
    g = g_ref[...]
    for h in range(A_HEADS):
        o1 = acc_a_ref[2 * h, 0:a_dv, :] / acc_a_ref[2 * h, a_dv:a_dv + 1, :]
        o2 = acc_a_ref[2 * h + 1, 0:a_dv, :] / acc_a_ref[2 * h + 1, a_dv:a_dv + 1, :]
        o = o1 - lam * o2
        y = o * lax.rsqrt(jnp.mean(o * o, axis=0, keepdims=True) + EPS)
        out_ref[0, 0, h * 128:(h + 1) * 128, :] = ((y * g) * (1.0 - lam_init)).astype(BF16)
    a_rows = A_HEADS * 2 * A_DH
    for h in range(B_HEADS):
        ob = acc_b_ref[h, 0:B_DH, :] / acc_b_ref[h, B_DH:B_DH + 1, :]
        out_ref[0, 0, a_rows + h * B_DH:a_rows + (h + 1) * B_DH, :] = ob.astype(BF16)


def _attention(lam_init, qat, qbt, qit, iwt, akp, avt, bkp, ikk, bvt, lamv, gt):
    bsz, s, _ = akp.shape
    nq = s // TQ
    nk = s // TK
    d_mix = A_HEADS * 2 * A_DH + B_HEADS * B_DH

    def per_query_block(arr):
        shape = (1, 1) + arr.shape[2:]
        return pl.BlockSpec(shape, lambda b, i: (b, i) + (0,) * (len(shape) - 2))

    def per_batch(shape):
        return pl.BlockSpec(shape, lambda b, i: (b,) + (0,) * (len(shape) - 1))

    in_specs = [
        per_query_block(qat),
        per_query_block(qbt),
        per_query_block(qit),
        pl.BlockSpec((1, TQ // TK, IDX_HEADS, TK), lambda b, i: (b, i, 0, 0)),
        per_batch((1, s, akp.shape[2])),
        per_batch((1, nk, avt.shape[2], TK)),
        per_batch((1, s, 128)),
        per_batch((1, s, 128)),
        per_batch((1, nk, bvt.shape[2], TK)),
        _const_spec(lamv.shape),
        _const_spec(gt.shape),
    ]
    n_maps = 2 * A_HEADS
    scratch = [
        pltpu.VMEM((nk, TK, TQ), F32),
        pltpu.VMEM((2, 8, TQ), F32),
        pltpu.VMEM((2, TK, 2 * TQ), F32),
        pltpu.VMEM((2, TK, B_HEADS // A_HEADS * TQ), F32),
        pltpu.VMEM((IDX_HEADS, TQ), F32),
        pltpu.VMEM((n_maps, 2 * A_DH + V_PAD, TQ), F32),
        pltpu.VMEM((n_maps, 1, TQ), F32),
        pltpu.VMEM((B_HEADS, B_DH + V_PAD, TQ), F32),
        pltpu.VMEM((B_HEADS, 1, TQ), F32),
    ]
    return pl.pallas_call(
        functools.partial(_attn_kernel, lam_init),
        grid=(bsz, nq),
        in_specs=in_specs,
        out_specs=pl.BlockSpec((1, 1, d_mix, TQ), lambda b, i: (b, i, 0, 0)),
        out_shape=jax.ShapeDtypeStruct((bsz, nq, d_mix, TQ), BF16),
        scratch_shapes=scratch,
        compiler_params=pltpu.CompilerParams(
            dimension_semantics=("arbitrary", "arbitrary"), vmem_limit_bytes=VMEM_LIMIT_BYTES),
        name="attention",
    )(qat, qbt, qit, iwt, akp, avt, bkp, ikk, bvt, lamv, gt)


def _ffn_kernel(final, mix_ref, x_ref, mod_ref, n2_ref, fg_ref, wo_ref, w1_ref, w2_ref, o_ref):
    wo = wo_ref[...]
    y = jnp.concatenate(
        [lax.dot_general(mix_ref[0, c], wo, _TN, preferred_element_type=F32)
         for c in range(TM_FFN // TQ)], axis=0)
    g1 = mod_ref[0, 2:3, :]
    sh2 = mod_ref[0, 3:4, :]
    sc2 = mod_ref[0, 4:5, :]
    g2 = mod_ref[0, 5:6, :]
    x1 = x_ref[0] + g1 * y
    h2 = ((_rms(x1) * n2_ref[...]) * (1.0 + sc2) + sh2).astype(BF16)
    ff = jnp.zeros_like(x1)
    for c in range(w1_ref.shape[1] // FF_CHUNK):
        u = jnp.dot(h2, w1_ref[:, c * FF_CHUNK:(c + 1) * FF_CHUNK], preferred_element_type=F32)
        u = jnp.square(jnp.maximum(u, 0.0)).astype(BF16)
        ff = ff + jnp.dot(u, w2_ref[c * FF_CHUNK:(c + 1) * FF_CHUNK, :], preferred_element_type=F32)
    x2 = x1 + g2 * ff
    if final:
        x2 = _rms(x2) * fg_ref[...]
    o_ref[0] = x2


def _out_ffn(final, mixt, x, mod, n2g, fg, wo, w1, w2):
    bsz, s, d = x.shape
    d_mix = mixt.shape[2]
    tm = TM_FFN
    cpt = tm // TQ
    return pl.pallas_call(
        functools.partial(_ffn_kernel, final),
        grid=(bsz, s // tm),
        in_specs=[
            pl.BlockSpec((1, cpt, d_mix, TQ), lambda b, t: (b, t, 0, 0)),
            pl.BlockSpec((1, tm, d), lambda b, t: (b, t, 0)),
            pl.BlockSpec((1, 6, d), lambda b, t: (b, 0, 0)),
            _const_spec(n2g.shape),
            _const_spec(fg.shape),
            _const_spec(wo.shape),
            _const_spec(w1.shape),
            _const_spec(w2.shape),
        ],
        out_specs=pl.BlockSpec((1, tm, d), lambda b, t: (b, t, 0)),
        out_shape=jax.ShapeDtypeStruct((bsz, s, d), F32),
        compiler_params=pltpu.CompilerParams(
            dimension_semantics=("arbitrary", "arbitrary"), vmem_limit_bytes=VMEM_LIMIT_BYTES),
        name="out_ffn",
    )(mixt, x, mod, n2g, fg, wo, w1, w2)


def _split_w_in(w):
    d = w.shape[0]
    a = A_HEADS * 2 * A_DH
    sizes = (a, a, a, B_HEADS * B_DH, B_DH, B_DH, IDX_HEADS * IDX_DH, IDX_DH, IDX_HEADS)
    offs = np.cumsum((0,) + sizes)
    aq, ak, av, bq, bk, bv, iq, ik, iw = [w[:, offs[n]:offs[n + 1]] for n in range(len(sizes))]
    kk = jnp.concatenate([bk, jnp.zeros((d, 128 - B_DH), w.dtype), ik, ik], axis=1)
    wt = jnp.concatenate([aq, bq, iq, av, bv, iw], axis=1).T
    return [m.astype(BF16) for m in (ak, kk, wt)]


def kernel(x, c, norm1_g, norm2_g, w_ada, b_ada, w_in, lam_q1, lam_k1, lam_q2, lam_k2,
           subln_g, w_out, w_ff1, w_ff2, final_g):
    bsz, s, d = x.shape
    depth = w_in.shape[0]
    assert s % TS_PROJ == 0 and s % TM_FFN == 0 and s % TQ == 0 and TQ % TK == 0 and TM_FFN % TQ == 0
    assert min(TOPK_MAX, s // 4) == TOPK_MAX and s // POS_SPLIT <= 256 and CHUNK == 1 << CHUNK_SHIFT
    fg = final_g.reshape(1, d)
    for l in range(depth):
        lam_init = 0.8 - 0.6 * math.exp(-0.3 * l)
        mod = _adaln_mod(c, w_ada[l].astype(BF16), b_ada[l].reshape(1, -1)).reshape(bsz, 6, d)
        qat, qbt, qit, akp, bkp, ikk, avt, bvt, iwt = _in_proj(
            x, mod, norm1_g[l].reshape(1, d), _split_w_in(w_in[l]))
        lamv = jnp.stack([lam_q1[l], lam_k1[l], lam_q2[l], lam_k2[l]])
        gt = jnp.broadcast_to(subln_g[l][:, None], (2 * A_DH, TQ))
        mixt = _attention(lam_init, qat, qbt, qit, iwt, akp, avt, bkp, ikk, bvt, lamv, gt)
        x = _out_ffn(l == depth - 1, mixt, x, mod, norm2_g[l].reshape(1, d), fg,
                     w_out[l].astype(BF16), w_ff1[l].astype(BF16), w_ff2[l].astype(BF16))
    return x
```
